```python
import math
import numpy as np
import jax
import jax.numpy as jnp
from jax import lax

D_MODEL = 1024
BATCH = 32
SEQ = 256
DEPTH = 4
DEC_BATCH = 2
DEC_SEQ = 4096
PAST_LEN = 256

GRID_W = 64
MIX_W = D_MODEL
GROUP_W = MIX_W // 4
HEAD_DIM = 64
POOL_GROUPS = 4
POOL_GC = GROUP_W // POOL_GROUPS
POOL_WINDOWS = (2, 4, 8, 16)
NA_HEADS = GROUP_W // HEAD_DIM
NA_ROWS = 8
NA_COLS = 16
ATTN_BLOCK = 128
RET_HEADS = 4
RET_DK = GROUP_W // RET_HEADS
RET_DV = GROUP_W // RET_HEADS
RET_CHUNK = 128
GLA_HEADS = 4
GLA_DK = GROUP_W // (2 * GLA_HEADS)
GLA_DV = GROUP_W // GLA_HEADS
GLA_LOWRANK = 16
GLA_TAU = 16.0
GLA_CHUNK = 64
D_FF = -(-8 * D_MODEL // (3 * 256)) * 256
P_IN = GROUP_W + 3 * NA_HEADS * HEAD_DIM + 2 * RET_HEADS * (RET_DK + RET_DV) + 2 * GLA_HEADS * (GLA_DK + GLA_DV) + GLA_LOWRANK
ROPE_BASE = 10000.0
RMS_EPS = 1e-6
GN_EPS = 1e-5

kernel_name = 'hybrid_pool_na_ret_gla_diffusion_step'


def _split_points():
    sizes = (GROUP_W,
             NA_HEADS * HEAD_DIM, NA_HEADS * HEAD_DIM, NA_HEADS * HEAD_DIM,
             RET_HEADS * RET_DK, RET_HEADS * RET_DK, RET_HEADS * RET_DV, RET_HEADS * RET_DV,
             GLA_HEADS * GLA_DK, GLA_HEADS * GLA_DK, GLA_HEADS * GLA_DV, GLA_HEADS * GLA_DV,
             GLA_LOWRANK)
    return [int(s) for s in np.cumsum(sizes)[:-1]]


def _rmsnorm(x, g):
    xf = x.astype(jnp.float32)
    y = xf * lax.rsqrt(jnp.mean(xf * xf, axis=-1, keepdims=True) + RMS_EPS)
    return (y * g.astype(jnp.float32)).astype(x.dtype)


def _head_groupnorm(o):
    of = o.astype(jnp.float32)
    mu = jnp.mean(of, axis=-1, keepdims=True)
    var = jnp.mean(jnp.square(of - mu), axis=-1, keepdims=True)
    return ((of - mu) * lax.rsqrt(var + GN_EPS)).astype(o.dtype)


def _heads(t, n):
    return t.reshape(t.shape[:-1] + (n, t.shape[-1] // n))


def _rope_axis(x, pos):
    nf = x.shape[-1] // 2
    inv = (ROPE_BASE ** (-np.arange(nf, dtype=np.float32) / nf)).astype(np.float32)
    ang = pos.astype(np.float32)[:, None] * inv[None, :]
    cos = jnp.asarray(np.cos(ang), dtype=x.dtype)[None, :, None, :]
    sin = jnp.asarray(np.sin(ang), dtype=x.dtype)[None, :, None, :]
    x1, x2 = x[..., :nf], x[..., nf:]
    return jnp.concatenate([x1 * cos - x2 * sin, x1 * sin + x2 * cos], axis=-1)


def _rope_2d(x):
    n = x.shape[1]
    t = np.arange(n)
    half = x.shape[-1] // 2
    return jnp.concatenate([_rope_axis(x[..., :half], t // GRID_W),
                            _rope_axis(x[..., half:], t % GRID_W)], axis=-1)


def _pool_mix(v, w_pool, scale):
    L = v.shape[-2]
    vf = v.astype(jnp.float32)
    cs = jnp.cumsum(vf, axis=-2)
    cs = jnp.concatenate([jnp.zeros_like(cs[..., :1, :]), cs], axis=-2)
    t = np.arange(L)
    means = []
    for gi, win in enumerate(POOL_WINDOWS):
        left = win // 2
        lo = np.clip(t - left, 0, L).astype(np.int32)
        hi = np.clip(t - left + win, 0, L).astype(np.int32)
        csg = cs[..., gi * POOL_GC:(gi + 1) * POOL_GC]
        cnt = jnp.asarray((hi - lo).astype(np.float32))[:, None]
        means.append((jnp.take(csg, hi, axis=-2) - jnp.take(csg, lo, axis=-2)) / cnt)
    d = (jnp.concatenate(means, axis=-1) - vf).astype(v.dtype)
    d = d.reshape(d.shape[:-1] + (POOL_GROUPS, POOL_GC))
    y = jnp.einsum('...gc,gcd->...gd', d, w_pool)
    return y.reshape(y.shape[:-2] + (GROUP_W,)) * scale


def _ctx_attention(q, k, v):
    B, L, H, dh = q.shape
    nb = L // ATTN_BLOCK
    qb = q.reshape(B, nb, ATTN_BLOCK, H, dh).transpose(1, 0, 2, 3, 4)
    scale = dh ** -0.5

    def block(qi):
        s = jnp.einsum('bqhd,bkhd->bhqk', qi, k).astype(jnp.float32) * scale
        p = jax.nn.softmax(s, axis=-1).astype(v.dtype)
        return jnp.einsum('bhqk,bkhd->bqhd', p, v)

    o = lax.map(block, qb)
    return o.transpose(1, 0, 2, 3, 4).reshape(B, L, H * dh)


def _na_latent(q, k, v, k_ctx, v_ctx, rpb):
    B, N, H, dh = q.shape
    rows = N // GRID_W
    kr = min(NA_ROWS, rows)
    kc = NA_COLS
    cols = np.arange(GRID_W)
    c0 = np.clip(cols - kc // 2, 0, GRID_W - kc)
    col_idx = (c0[:, None] + np.arange(kc)[None, :]).astype(np.int32)
    dc = (col_idx - cols[:, None] + (NA_COLS - 1)).astype(np.int32)
    scale = dh ** -0.5
    qr = q.reshape(B, rows, GRID_W, H, dh).transpose(1, 0, 2, 3, 4)

    def one_row(args):
        r, q_row = args
        r0 = jnp.clip(r - kr // 2, 0, rows - kr)
        key_rows = r0 + jnp.arange(kr, dtype=jnp.int32)
        tok = key_rows[None, :, None] * GRID_W + col_idx[:, None, :]
        kw = k[:, tok]
        vw = v[:, tok]
        dr = key_rows - r + (NA_ROWS - 1)
        bias = rpb[:, dr[None, :, None], dc[:, None, :]].astype(jnp.float32)
        s_loc = jnp.einsum('bwhd,bwrchd->bhwrc', q_row, kw).astype(jnp.float32) * scale + bias
        s_ctx = jnp.einsum('bwhd,bmhd->bhwm', q_row, k_ctx).astype(jnp.float32) * scale
        s = jnp.concatenate([s_loc.reshape(B, H, GRID_W, kr * kc), s_ctx], axis=-1)
        p = jax.nn.softmax(s, axis=-1).astype(v.dtype)
        p_loc = p[..., :kr * kc].reshape(B, H, GRID_W, kr, kc)
        return (jnp.einsum('bhwrc,bwrchd->bwhd', p_loc, vw)
                + jnp.einsum('bhwm,bmhd->bwhd', p[..., kr * kc:], v_ctx))

    o = lax.map(one_row, (jnp.arange(rows, dtype=jnp.int32), qr))
    return o.transpose(1, 0, 2, 3, 4).reshape(B, N, H * dh)


def _retention_scan(q, k, v, log_g, s0):
    B, L, H, dk = q.shape
    dv = v.shape[-1]
    C = RET_CHUNK
    n = L // C

    def chunks(t):
        return t.reshape(B, n, C, H, t.shape[-1]).transpose(1, 0, 3, 2, 4)

    idx = np.arange(C, dtype=np.float32)
    diff = idx[:, None] - idx[None, :]
    lg = log_g[:, None, None]
    d_intra = jnp.where(diff >= 0, jnp.exp(np.maximum(diff, 0.0) * lg), 0.0).astype(q.dtype)
    d_q = jnp.exp((idx + 1.0)[None, :, None] * lg).astype(q.dtype)
    d_k = jnp.exp((C - 1.0 - idx)[None, :, None] * lg).astype(q.dtype)
    d_c = jnp.exp(C * lg).astype(q.dtype)

    def step(S, inp):
        qc, kc, vc = inp
        a = jnp.einsum('bhid,bhjd->bhij', qc, kc) * d_intra
        o = jnp.einsum('bhij,bhjv->bhiv', a, vc) + jnp.einsum('bhid,bhdv->bhiv', qc, S) * d_q
        S = S * d_c + jnp.einsum('bhjd,bhjv->bhdv', kc * d_k, vc)
        return S.astype(s0.dtype), o

    S, o = lax.scan(step, s0, (chunks(q), chunks(k), chunks(v)))
    return o.transpose(1, 0, 3, 2, 4).reshape(B, L, H, dv), S


def _retention(q, k, v, g, decay_logit, s0):
    q = q * (RET_DK ** -0.5)
    log_g = jax.nn.log_sigmoid(decay_logit.astype(jnp.float32))
    o_f, s_f = _retention_scan(q, k, v, log_g[0], s0[:, 0])
    o_b, s_b = _retention_scan(q[:, ::-1], k[:, ::-1], v[:, ::-1], log_g[1], s0[:, 1])
    o = _head_groupnorm(o_f + o_b[:, ::-1])
    o = o.reshape(g.shape) * jax.nn.silu(g)
    return o, jnp.stack([s_f, s_b], axis=1)


def _gla_scan(q, k, v, log_a, s0):
    B, L, H, dk = q.shape
    dv = v.shape[-1]
    C = GLA_CHUNK
    n = L // C

    def chunks(t):
        return t.reshape(B, n, C, H, t.shape[-1]).transpose(1, 0, 3, 2, 4)

    mask = np.tril(np.ones((C, C), dtype=bool))[:, :, None]

    def step(S, inp):
        qc, kc, vc, ac = inp
        b = jnp.cumsum(ac, axis=2)
        rel = jnp.where(mask, b[:, :, :, None, :] - b[:, :, None, :, :], -jnp.inf)
        a = jnp.einsum('bhid,bhjd,bhijd->bhij', qc, kc, jnp.exp(rel).astype(qc.dtype))
        o = (jnp.einsum('bhij,bhjv->bhiv', a, vc)
             + jnp.einsum('bhid,bhdv->bhiv', qc * jnp.exp(b).astype(qc.dtype), S))
        bl = b[:, :, -1:, :]
        S = (S * jnp.exp(bl[:, :, 0, :, None]).astype(S.dtype)
             + jnp.einsum('bhjd,bhjv->bhdv', kc * jnp.exp(bl - b).astype(kc.dtype), vc))
        return S.astype(s0.dtype), o

    S, o = lax.scan(step, s0, (chunks(q), chunks(k), chunks(v), chunks(log_a)))
    return o.transpose(1, 0, 3, 2, 4).reshape(B, L, H, dv), S


def _gla(q, k, v, g, lr, gate_up, gate_b, norm_g, s0):
    q = q * (GLA_DK ** -0.5)

    def log_gate(d):
        z = (lr @ gate_up[d] + gate_b[d]).astype(jnp.float32)
        return _heads(jax.nn.log_sigmoid(z) / GLA_TAU, GLA_HEADS)

    o_f, s_f = _gla_scan(q, k, v, log_gate(0), s0[:, 0])
    o_b, s_b = _gla_scan(q[:, ::-1], k[:, ::-1], v[:, ::-1], log_gate(1)[:, ::-1], s0[:, 1])
    o = _rmsnorm(o_f + o_b[:, ::-1], norm_g)
    o = o.reshape(g.shape) * jax.nn.silu(g)
    return o, jnp.stack([s_f, s_b], axis=1)


def _token_mixers(h, latent, w_in_l, pool_w_l, pool_scale_l, rpb_l, ret_logit_l, gla_up_l, gla_b_l,
                  gla_ng_l, ctx_k, ctx_v, ret_s0, gla_s0):
    B, L, _ = h.shape
    parts = jnp.split(h @ w_in_l, _split_points(), axis=-1)
    v_pool, na_q, na_k, na_v, r_q, r_k, r_v, r_g, a_q, a_k, a_v, a_g, a_lr = parts
    na_q, na_k, na_v = _heads(na_q, NA_HEADS), _heads(na_k, NA_HEADS), _heads(na_v, NA_HEADS)
    r_q, r_k, r_v = _heads(r_q, RET_HEADS), _heads(r_k, RET_HEADS), _heads(r_v, RET_HEADS)
    a_q, a_k, a_v = _heads(a_q, GLA_HEADS), _heads(a_k, GLA_HEADS), _heads(a_v, GLA_HEADS)
    if latent:
        rows = L // GRID_W
        o_pool = _pool_mix(v_pool.reshape(B, rows, GRID_W, GROUP_W), pool_w_l, pool_scale_l).reshape(B, L, GROUP_W)
        o_na = _na_latent(na_q, na_k, na_v, ctx_k, ctx_v, rpb_l)
        r_q, r_k = _rope_2d(r_q), _rope_2d(r_k)
    else:
        o_pool = _pool_mix(v_pool, pool_w_l, pool_scale_l)
        o_na = _ctx_attention(na_q, na_k, na_v)
        ret_s0 = jnp.zeros((B, 2, RET_HEADS, RET_DK, RET_DV), h.dtype)
        gla_s0 = jnp.zeros((B, 2, GLA_HEADS, GLA_DK, GLA_DV), h.dtype)
    o_ret, s_ret = _retention(r_q, r_k, r_v, r_g, ret_logit_l, ret_s0)
    o_gla, s_gla = _gla(a_q, a_k, a_v, a_g, a_lr, gla_up_l, gla_b_l, gla_ng_l, gla_s0)
    o = jnp.concatenate([o_pool, o_na, o_ret, o_gla], axis=-1)
    if latent:
        return o, None, None, None, None
    return o, na_k, na_v, s_ret, s_gla


def setup_inputs(seed: int = 0) -> dict:
    key = jax.random.key(seed)
    ks = jax.random.split(key, 26)
    f32 = jnp.float32
    D = D_MODEL

    def nrm(k, shape, s):
        return jax.random.normal(k, shape, f32) * s

    ret_base = np.log(2.0 ** (5 + np.arange(RET_HEADS)) - 1.0).astype(np.float32)
    return {
        'x_prompt': nrm(ks[0], (BATCH, SEQ, D), 1.0),
        'x_sample': nrm(ks[1], (DEC_BATCH, DEC_SEQ, D), 1.0),
        'cache_na_k': nrm(ks[2], (DEC_BATCH, DEPTH, PAST_LEN, NA_HEADS, HEAD_DIM), 1.0),
        'cache_na_v': nrm(ks[3], (DEC_BATCH, DEPTH, PAST_LEN, NA_HEADS, HEAD_DIM), 1.0),
        'state_ret': nrm(ks[4], (DEC_BATCH, DEPTH, 2, RET_HEADS, RET_DK, RET_DV), 1.0),
        'state_gla': nrm(ks[5], (DEC_BATCH, DEPTH, 2, GLA_HEADS, GLA_DK, GLA_DV), 1.0),
        'c': nrm(ks[6], (DEC_BATCH, D), 1.0),
        'c_ctx': nrm(ks[7], (D,), 1.0),
        'w_mod': nrm(ks[8], (DEPTH, D, 6 * D), D ** -0.5),
        'b_mod': nrm(ks[9], (DEPTH, 6 * D), 0.01),
        'g_pre_mix': 1.0 + nrm(ks[10], (DEPTH, D), 0.05),
        'g_post_mix': 1.0 + nrm(ks[11], (DEPTH, D), 0.05),
        'g_pre_ffn': 1.0 + nrm(ks[12], (DEPTH, D), 0.05),
        'g_post_ffn': 1.0 + nrm(ks[13], (DEPTH, D), 0.05),
        'w_in': nrm(ks[14], (DEPTH, D, P_IN), D ** -0.5),
        'w_out': nrm(ks[15], (DEPTH, MIX_W, D), MIX_W ** -0.5),
        'pool_w': nrm(ks[16], (DEPTH, POOL_GROUPS, POOL_GC, POOL_GC), POOL_GC ** -0.5),
        'pool_scale': 1.0 + nrm(ks[17], (DEPTH, GROUP_W), 0.05),
        'na_rpb': nrm(ks[18], (DEPTH, NA_HEADS, 2 * NA_ROWS - 1, 2 * NA_COLS - 1), 0.02),
        'ret_decay_logit': jnp.asarray(ret_base)[None, None, :] + nrm(ks[19], (DEPTH, 2, RET_HEADS), 0.1),
        'gla_gate_up': nrm(ks[20], (DEPTH, 2, GLA_LOWRANK, GLA_HEADS * GLA_DK), GLA_LOWRANK ** -0.5),
        'gla_gate_b': nrm(ks[21], (DEPTH, 2, GLA_HEADS * GLA_DK), 0.01),
        'gla_norm_g': 1.0 + nrm(ks[22], (DEPTH, GLA_DV), 0.05),
        'w_ffn_gate': nrm(ks[23], (DEPTH, D, D_FF), D ** -0.5),
        'w_ffn_up': nrm(ks[24], (DEPTH, D, D_FF), D ** -0.5),
        'w_ffn_down': nrm(ks[25], (DEPTH, D_FF, D), D_FF ** -0.5),
    }


def reference(x_prompt, x_sample, cache_na_k, cache_na_v, state_ret, state_gla, c, c_ctx,
              w_mod, b_mod, g_pre_mix, g_post_mix, g_pre_ffn, g_post_ffn, w_in, w_out,
              pool_w, pool_scale, na_rpb, ret_decay_logit, gla_gate_up, gla_gate_b, gla_norm_g,
              w_ffn_gate, w_ffn_up, w_ffn_down):

    def run_layer(l, x, cvec, latent, ctx_k, ctx_v, ret_s0, gla_s0):
        mod = jax.nn.silu(cvec) @ w_mod[l] + b_mod[l]
        sh1, sc1, gt1, sh2, sc2, gt2 = jnp.split(mod[:, None, :], 6, axis=-1)
        h = _rmsnorm(x, g_pre_mix[l]) * (1.0 + sc1) + sh1
        o, k_na, v_na, s_ret, s_gla = _token_mixers(
            h, latent, w_in[l], pool_w[l], pool_scale[l], na_rpb[l], ret_decay_logit[l],
            gla_gate_up[l], gla_gate_b[l], gla_norm_g[l], ctx_k, ctx_v, ret_s0, gla_s0)
        x = x + gt1 * _rmsnorm(o @ w_out[l], g_post_mix[l])
        h = _rmsnorm(x, g_pre_ffn[l]) * (1.0 + sc2) + sh2
        y = (jax.nn.silu(h @ w_ffn_gate[l]) * (h @ w_ffn_up[l])) @ w_ffn_down[l]
        x = x + gt2 * _rmsnorm(y, g_post_ffn[l])
        return x, k_na, v_na, s_ret, s_gla

    xp = x_prompt
    cvec_ctx = c_ctx[None, :]
    ks_list, vs_list, sr_list, sg_list = [], [], [], []
    for l in range(DEPTH):
        xp, k_na, v_na, s_r, s_g = run_layer(l, xp, cvec_ctx, False, None, None, None, None)
        ks_list.append(k_na)
        vs_list.append(v_na)
        sr_list.append(s_r)
        sg_list.append(s_g)

    xs = x_sample
    for l in range(DEPTH):
        xs, _, _, _, _ = run_layer(l, xs, c, True, cache_na_k[:, l], cache_na_v[:, l],
                                   state_ret[:, l], state_gla[:, l])

    new_cache_na_k = jnp.stack(ks_list, axis=1)
    new_cache_na_v = jnp.stack(vs_list, axis=1)
    new_state_ret = jnp.stack(sr_list, axis=1)
    new_state_gla = jnp.stack(sg_list, axis=1)
    return (xp, xs, new_cache_na_k, new_cache_na_v, new_state_ret, new_state_gla)
```

```python
import functools

import numpy as np
import jax
import jax.numpy as jnp
from jax import lax
from jax.experimental import pallas as pl
from jax.experimental.pallas import tpu as pltpu

F32 = jnp.float32
BF16 = jnp.bfloat16

D = 1024
B_CTX, L_CTX = 32, 256
B_LAT, L_LAT = 2, 4096
DEPTH = 4
PAST = 256
GRID_W = 64
GRID_H = L_LAT // GRID_W
T_CTX = B_CTX * L_CTX
T_LAT = B_LAT * L_LAT
T = T_CTX + T_LAT
GW = 256
HEADS = 4
HD = 64
POOL_WINDOWS = (2, 4, 8, 16)
NA_ROWS, NA_COLS = 8, 16
RET_DK = 64
GLA_DK = 32
GLA_LOWRANK = 16
GLA_TAU = 16.0
D_FF = 2816
P_IN = 2832
P_PAD = 2944
ROPE_BASE = 10000.0
RMS_EPS = 1e-6
GN_EPS = 1e-5
NEG = -1e30

CB_POOL, CB_NAQ, CB_NAK, CB_NAV, CB_RQ, CB_RK, CB_RV, CB_RG, CB_AQK, CB_AV, CB_AG = range(11)
CB_LR128 = P_IN // 128

SEQ_TILE = 256
GLA_CHUNK = 64
GLA_SUB = 16
VMEM_LIMIT = 56 * 1024 * 1024


def _cparams(sem):
    return pltpu.CompilerParams(dimension_semantics=sem, vmem_limit_bytes=VMEM_LIMIT)


def _silu(x):
    return x / (1.0 + jnp.exp(-x))


def _log_sigmoid(z):
    return jnp.minimum(z, 0.0) - jnp.log1p(jnp.exp(-jnp.abs(z)))


def _rms(x, g):
    return x * lax.rsqrt(jnp.mean(x * x, axis=-1, keepdims=True) + RMS_EPS) * g


def _dot(a, b):
    return jnp.dot(a, b, preferred_element_type=F32)


def _dot_nt(a, b):
    return lax.dot_general(a, b, (((1,), (1,)), ((), ())), preferred_element_type=F32)


def _dot_tn(a, b):
    return lax.dot_general(a, b, (((0,), (0,)), ((), ())), preferred_element_type=F32)


def _split_hi_lo(x):
    hi = x.astype(BF16)
    return hi, (x - hi.astype(F32)).astype(BF16)


def _dot_exact01(a01, x):
    hi, lo = _split_hi_lo(x)
    return _dot(a01, hi) + _dot(a01, lo)


def _iota(shape, dim):
    return lax.broadcasted_iota(jnp.int32, shape, dim)


def _expand_heads(x, head_w):
    n, w = x.shape
    xe = jnp.concatenate([x] * HEADS, axis=0)
    rowh = _iota((HEADS * n, w), 0) // n
    laneh = (_iota((HEADS * n, w), 1) // head_w) % HEADS
    return jnp.where(rowh == laneh, xe, jnp.zeros_like(xe))


def _extract_heads(p, n):
    laneh = _iota((n, GW), 1) // HD
    out = p[0:n]
    for h in range(1, HEADS):
        out = jnp.where(laneh == h, p[h * n:(h + 1) * n], out)
    return out


def _head_mean(x, avg):
    hi, lo = _split_hi_lo(x)
    return _dot(hi, avg) + _dot(lo, avg)


def _mod_row(i, tm):
    return jnp.where(i < T_CTX // tm, 0, 1 + (i * tm - T_CTX) // L_LAT)


def _mod_kernel(cv_ref, w_ref, b_ref, o_ref):
    s = _silu(cv_ref[...]).astype(BF16)
    o_ref[0] = _dot(s, w_ref[0].astype(BF16)) + b_ref[0]


def _modulation(cv8, w_mod, b_mod):
    tn = 1536
    return pl.pallas_call(
        _mod_kernel,
        grid=(DEPTH, 6 * D // tn),
        in_specs=[
            pl.BlockSpec((8, D), lambda l, j: (0, 0)),
            pl.BlockSpec((1, D, tn), lambda l, j: (l, 0, j)),
            pl.BlockSpec((1, 1, tn), lambda l, j: (l, 0, j)),
        ],
        out_specs=pl.BlockSpec((1, 8, tn), lambda l, j: (l, 0, j)),
        out_shape=jax.ShapeDtypeStruct((DEPTH, 8, 6 * D), F32),
        compiler_params=_cparams(("arbitrary", "arbitrary")),
        name="modulation",
    )(cv8, w_mod, b_mod.reshape(DEPTH, 1, 6 * D))


IN_TM = 512


def _inproj_kernel(x_ref, mod_ref, g_ref, w_ref, o_ref):
    h = _rms(x_ref[...], g_ref[...]) * (1.0 + mod_ref[0, 1:2, :]) + mod_ref[0, 0:1, :]
    hb = h.astype(BF16)
    for a in range(0, P_PAD, 1024):
        b = min(a + 1024, P_PAD)
        o_ref[:, a:b] = _dot(hb, w_ref[:, a:b])


def _inproj(x, mod_l, g_pre, w_in_b):
    tm = IN_TM
    return pl.pallas_call(
        _inproj_kernel,
        grid=(T // tm,),
        in_specs=[
            pl.BlockSpec((tm, D), lambda i: (i, 0)),
            pl.BlockSpec((1, 6, D), lambda i: (_mod_row(i, tm), 0, 0)),
            pl.BlockSpec((1, D), lambda i: (0, 0)),
            pl.BlockSpec((D, P_PAD), lambda i: (0, 0)),
        ],
        out_specs=pl.BlockSpec((tm, P_PAD), lambda i: (i, 0)),
        out_shape=jax.ShapeDtypeStruct((T, P_PAD), F32),
        compiler_params=_cparams(("parallel",)),
        name="inproj",
    )(x, mod_l, g_pre, w_in_b)


def _pool_kernel(v_ref, w_ref, scale_ref, o_ref):
    i = pl.program_id(0)
    n = SEQ_TILE
    seg_mask = jnp.where(i < T_CTX // n, ~(L_CTX - 1), ~(GRID_W - 1))
    seg_len = jnp.where(i < T_CTX // n, L_CTX, GRID_W)
    v = v_ref[...]
    vh = v.astype(BF16)
    vl = (v - vh.astype(F32)).astype(BF16)
    t = _iota((n, n), 0)
    s = _iota((n, n), 1)
    seg0 = t & seg_mask
    seg1 = seg0 + seg_len
    lane_g = _iota((n, GW), 1) // HD
    mean = jnp.zeros((n, GW), F32)
    for gi, win in enumerate(POOL_WINDOWS):
        lo = jnp.maximum(t - win // 2, seg0)
        hi = jnp.minimum(t - win // 2 + win, seg1)
        w01 = jnp.where(s >= lo, jnp.where(s < hi, 1.0, 0.0), 0.0).astype(BF16)
        cnt = (hi - lo).astype(F32)
        m = (_dot(w01, vh) + _dot(w01, vl)) / cnt
        mean = jnp.where(lane_g == gi, m, mean)
    d = (mean - v).astype(BF16)
    o_ref[...] = _dot(d, w_ref[...]) * scale_ref[...]


def _pool(parts, w_bd, scale):
    n = SEQ_TILE
    return pl.pallas_call(
        _pool_kernel,
        grid=(T // n,),
        in_specs=[
            pl.BlockSpec((n, GW), lambda i: (i, CB_POOL)),
            pl.BlockSpec((GW, GW), lambda i: (0, 0)),
            pl.BlockSpec((1, GW), lambda i: (0, 0)),
        ],
        out_specs=pl.BlockSpec((n, GW), lambda i: (i, 0)),
        out_shape=jax.ShapeDtypeStruct((T, GW), F32),
        compiler_params=_cparams(("parallel",)),
        name="pool",
    )(parts, w_bd, scale)


def _softmax_rows(s):
    m = jnp.max(s, axis=-1, keepdims=True)
    p = jnp.exp(s - m)
    return p / jnp.sum(p, axis=-1, keepdims=True)


def _ctx_attn_kernel(q_ref, k_ref, v_ref, o_ref):
    n = L_CTX
    qe = _expand_heads(q_ref[...].astype(BF16), HD)
    s = _dot_nt(qe, k_ref[...].astype(BF16)) * (HD ** -0.5)
    p = _softmax_rows(s).astype(BF16)
    o_ref[...] = _extract_heads(_dot(p, v_ref[...].astype(BF16)), n)


def _ctx_attn(parts):
    n = L_CTX
    return pl.pallas_call(
        _ctx_attn_kernel,
        grid=(B_CTX,),
        in_specs=[
            pl.BlockSpec((n, GW), lambda b: (b, CB_NAQ)),
            pl.BlockSpec((n, GW), lambda b: (b, CB_NAK)),
            pl.BlockSpec((n, GW), lambda b: (b, CB_NAV)),
        ],
        out_specs=pl.BlockSpec((n, GW), lambda b: (b, 0)),
        out_shape=jax.ShapeDtypeStruct((T_CTX, GW), F32),
        compiler_params=_cparams(("parallel",)),
        name="ctx_attn",
    )(parts, parts, parts)


NA_ROWS_PER_STEP = 8
NA_WIN = NA_ROWS * GRID_W


def _na_kernel(q_ref, k_ref, v_ref, ck_ref, cv_ref, e2_ref, o_ref, kb_ref, vb_ref):
    step = pl.program_id(1)

    @pl.when(step == 0)
    def _():
        kb_ref[...] = k_ref[...].astype(BF16)
        vb_ref[...] = v_ref[...].astype(BF16)

    ckb = ck_ref[0, 0].astype(BF16)
    cvb = cv_ref[0, 0].astype(BF16)
    scale = HD ** -0.5

    def one_row(rr, carry):
        r = step * NA_ROWS_PER_STEP + rr
        r0 = jnp.clip(r - NA_ROWS // 2, 0, GRID_H - NA_ROWS)
        base = r0 - r + (NA_ROWS - 1)
        q0 = pl.multiple_of(rr * GRID_W, GRID_W)
        k0 = pl.multiple_of(r0 * GRID_W, GRID_W)
        qe = _expand_heads(q_ref[pl.ds(q0, GRID_W), :].astype(BF16), HD)
        kw = kb_ref[pl.ds(k0, NA_WIN), :]
        vw = vb_ref[pl.ds(k0, NA_WIN), :]
        bias = jnp.concatenate(
            [jnp.concatenate([e2_ref[h, base + 2 * p] for p in range(NA_ROWS // 2)], axis=1)
             for h in range(HEADS)], axis=0)
        s_loc = _dot_nt(qe, kw) * scale + bias
        s_ctx = _dot_nt(qe, ckb) * scale
        m = jnp.maximum(jnp.max(s_loc, axis=-1, keepdims=True), jnp.max(s_ctx, axis=-1, keepdims=True))
        p_loc = jnp.exp(s_loc - m)
        p_ctx = jnp.exp(s_ctx - m)
        inv = 1.0 / (jnp.sum(p_loc, axis=-1, keepdims=True) + jnp.sum(p_ctx, axis=-1, keepdims=True))
        pv = _dot((p_loc * inv).astype(BF16), vw) + _dot((p_ctx * inv).astype(BF16), cvb)
        o_ref[pl.ds(q0, GRID_W), :] = _extract_heads(pv, GRID_W)
        return carry

    lax.fori_loop(0, NA_ROWS_PER_STEP, one_row, 0)


def _na_bias_table(rpb):
    qc = np.arange(GRID_W)[:, None]
    kc = np.arange(GRID_W)[None, :]
    c0 = np.clip(qc - NA_COLS // 2, 0, GRID_W - NA_COLS)
    inwin = (kc >= c0) & (kc < c0 + NA_COLS)
    dc = np.clip(kc - qc + (NA_COLS - 1), 0, 2 * NA_COLS - 2)
    e = jnp.where(jnp.asarray(inwin)[None, None], rpb[:, :, dc], NEG)
    return jnp.concatenate([e[:, :-1], e[:, 1:]], axis=-1)


def _na_latent(parts, ck, cv, e2):
    rows = NA_ROWS_PER_STEP * GRID_W
    steps = L_LAT // rows
    row_blk0 = T_CTX // rows
    seq_blk0 = T_CTX // L_LAT
    return pl.pallas_call(
        _na_kernel,
        grid=(B_LAT, steps),
        in_specs=[
            pl.BlockSpec((rows, GW), lambda b, s: (row_blk0 + b * steps + s, CB_NAQ)),
            pl.BlockSpec((L_LAT, GW), lambda b, s: (seq_blk0 + b, CB_NAK)),
            pl.BlockSpec((L_LAT, GW), lambda b, s: (seq_blk0 + b, CB_NAV)),
            pl.BlockSpec((1, 1, PAST, GW), lambda b, s: (b, 0, 0, 0)),
            pl.BlockSpec((1, 1, PAST, GW), lambda b, s: (b, 0, 0, 0)),
            pl.BlockSpec((HEADS, 2 * NA_ROWS - 2, GRID_W, 2 * GRID_W), lambda b, s: (0, 0, 0, 0)),
        ],
        out_specs=pl.BlockSpec((rows, GW), lambda b, s: (b * steps + s, 0)),
        out_shape=jax.ShapeDtypeStruct((T_LAT, GW), F32),
        scratch_shapes=[pltpu.VMEM((L_LAT, GW), BF16), pltpu.VMEM((L_LAT, GW), BF16)],
        compiler_params=_cparams(("arbitrary", "arbitrary")),
        name="na_latent",
    )(parts, parts, parts, ck, cv, e2)


def _rope(x, cos, sin_signed):
    lane = _iota(x.shape, 1)
    partner = jnp.where(lane % 32 < 16, pltpu.roll(x, GW - 16, 1), pltpu.roll(x, 16, 1))
    return x * cos + partner * sin_signed


def _ret_kernel(*refs, rope, has_s0, n_tiles):
    it = iter(refs)
    qf_ref, kf_ref, vf_ref, qb_ref, kb_ref, vb_ref = (next(it) for _ in range(6))
    if rope:
        cf_ref, sf_ref, cb_ref, sb_ref = (next(it) for _ in range(4))
    lg_ref = next(it)
    if has_s0:
        s0_ref = next(it)
    of_ref, ob_ref, st_ref, sf_scr, sb_scr = (next(it) for _ in range(5))

    t = pl.program_id(1)
    n = SEQ_TILE
    lane_h = _iota((GW, GW), 1) // HD
    row_h = _iota((GW, GW), 0) // HD
    blockdiag = row_h == lane_h

    @pl.when(t == 0)
    def _():
        if has_s0:
            for d, scr in ((0, sf_scr), (1, sb_scr)):
                s0 = jnp.concatenate([s0_ref[0, d]] * HEADS, axis=1)
                scr[...] = jnp.where(blockdiag, s0, 0.0)
        else:
            sf_scr[...] = jnp.zeros((GW, GW), F32)
            sb_scr[...] = jnp.zeros((GW, GW), F32)

    lg = _log_sigmoid(lg_ref[...])
    ti = _iota((n, GW), 0).astype(F32)
    i_ = _iota((n, n), 0).astype(F32)
    j_ = _iota((n, n), 1).astype(F32)

    def stream(d, q_ref, k_ref, v_ref, c_ref, s_ref, o_ref, scr):
        lgd = lg[d:d + 1, :]
        q = q_ref[...] * (RET_DK ** -0.5)
        k = k_ref[...]
        if rope:
            q = _rope(q, c_ref[...], s_ref[...])
            k = _rope(k, c_ref[...], s_ref[...])
        vb16 = v_ref[...].astype(BF16)
        diff = (i_ - j_) if d == 0 else (j_ - i_)
        pos = ti if d == 0 else (n - 1.0) - ti
        w = jnp.concatenate(
            [jnp.where(diff >= 0, jnp.exp(jnp.maximum(diff, 0.0) * lgd[:, h * HD:h * HD + 1]), 0.0)
             for h in range(HEADS)], axis=0)
        a = _dot_nt(_expand_heads(q.astype(BF16), HD), k.astype(BF16)) * w
        o = _extract_heads(_dot(a.astype(BF16), vb16), n)
        s_old = scr[...]
        o = o + _dot(q.astype(BF16), s_old.astype(BF16)) * jnp.exp((pos + 1.0) * lgd)
        o_ref[...] = o
        kd = (k * jnp.exp((n - 1.0 - pos) * lgd)).astype(BF16)
        upd = _dot_tn(kd, vb16)
        scr[...] = s_old * jnp.exp(float(n) * lgd) + jnp.where(blockdiag, upd, 0.0)

    stream(0, qf_ref, kf_ref, vf_ref, cf_ref if rope else None, sf_ref if rope else None, of_ref, sf_scr)
    stream(1, qb_ref, kb_ref, vb_ref, cb_ref if rope else None, sb_ref if rope else None, ob_ref, sb_scr)

    @pl.when(t == n_tiles - 1)
    def _():
        for d, scr in ((0, sf_scr), (1, sb_scr)):
            s = scr[...]
            st_ref[0, d] = s[:, 0:HD] + s[:, HD:2 * HD] + s[:, 2 * HD:3 * HD] + s[:, 3 * HD:4 * HD]


def _retention(parts, lg_lanes, *, tok0, batch, seq_len, rope_tabs=None, s0=None):
    n = SEQ_TILE
    nt = seq_len // n
    blk0 = tok0 // n
    fwd = lambda b, t: blk0 + b * nt + t
    bwd = lambda b, t: blk0 + b * nt + (nt - 1 - t)
    in_specs = [pl.BlockSpec((n, GW), lambda b, t, c=c: (fwd(b, t), c)) for c in (CB_RQ, CB_RK, CB_RV)]
    in_specs += [pl.BlockSpec((n, GW), lambda b, t, c=c: (bwd(b, t), c)) for c in (CB_RQ, CB_RK, CB_RV)]
    args = [parts] * 6
    if rope_tabs is not None:
        in_specs += [pl.BlockSpec((n, GW), lambda b, t: (t, 0))] * 2
        in_specs += [pl.BlockSpec((n, GW), lambda b, t: (nt - 1 - t, 0))] * 2
        args += [rope_tabs[0], rope_tabs[1], rope_tabs[0], rope_tabs[1]]
    in_specs.append(pl.BlockSpec((2, GW), lambda b, t: (0, 0)))
    args.append(lg_lanes)
    if s0 is not None:
        in_specs.append(pl.BlockSpec((1, 2, GW, HD), lambda b, t: (b, 0, 0, 0)))
        args.append(s0)
    kern = functools.partial(_ret_kernel, rope=rope_tabs is not None, has_s0=s0 is not None, n_tiles=nt)
    return pl.pallas_call(
        kern,
        grid=(batch, nt),
        in_specs=in_specs,
        out_specs=[
            pl.BlockSpec((n, GW), lambda b, t: (b * nt + t, 0)),
            pl.BlockSpec((n, GW), lambda b, t: (b * nt + (nt - 1 - t), 0)),
            pl.BlockSpec((1, 2, GW, HD), lambda b, t: (b, 0, 0, 0)),
        ],
        out_shape=[
            jax.ShapeDtypeStruct((batch * seq_len, GW), F32),
            jax.ShapeDtypeStruct((batch * seq_len, GW), F32),
            jax.ShapeDtypeStruct((batch, 2, GW, HD), F32),
        ],
        scratch_shapes=[pltpu.VMEM((GW, GW), F32), pltpu.VMEM((GW, GW), F32)],
        compiler_params=_cparams(("arbitrary", "arbitrary")),
        name="retention",
    )(*args)


GLA_QK = HEADS * GLA_DK
N_SUB = GLA_CHUNK // GLA_SUB


def _gla_chunk(q, k, v, la, st, rev):
    c = GLA_CHUNK
    ri = _iota((c, c), 0)
    ci = _iota((c, c), 1)
    tri = jnp.where((ci >= ri) if rev else (ci <= ri), 1.0, 0.0).astype(BF16)
    b = _dot_exact01(tri, la)
    b_end = b[0:1, :] if rev else b[c - 1:c, :]
    row = _iota((c, GLA_QK), 0)
    sub = row // GLA_SUB
    off = row % GLA_SUB

    o = _dot_nt((q * jnp.exp(b)).astype(BF16), st.astype(BF16))
    kt = (k * jnp.exp(b_end - b)).astype(BF16)
    lane_h = _iota((GW, GLA_QK), 1) // GLA_DK
    row_h = _iota((GW, GLA_QK), 0) // HD
    st_new = st * jnp.exp(b_end) + jnp.where(row_h == lane_h, _dot_tn(v.astype(BF16), kt), 0.0)

    q_parts, k_parts = [], []
    for s in range(1, N_SUB):
        if rev:
            qsub, brow = N_SUB - 1 - s, b[(N_SUB - s) * GLA_SUB:(N_SUB - s) * GLA_SUB + 1, :]
            kvalid = sub > qsub
        else:
            qsub, brow = s, b[s * GLA_SUB - 1:s * GLA_SUB, :]
            kvalid = sub < qsub
        q_parts.append(jnp.where(sub == qsub, q * jnp.exp(jnp.where(sub == qsub, b - brow, 0.0)), 0.0))
        k_parts.append(jnp.where(kvalid, k * jnp.exp(jnp.where(kvalid, brow - b, 0.0)), 0.0))
    q_cat = _expand_heads(jnp.concatenate(q_parts, axis=1).astype(BF16), GLA_DK)
    k_cat = jnp.concatenate(k_parts, axis=1).astype(BF16)
    a_off = _dot_nt(q_cat, k_cat)
    o = o + _extract_heads(_dot(a_off.astype(BF16), v.astype(BF16)), c)

    red = jnp.where(_iota((GLA_QK, GW), 0) // GLA_DK == _iota((GLA_QK, GW), 1) // HD, 1.0, 0.0).astype(BF16)
    rowv = _iota((c, GW), 0) % GLA_SUB
    for dl in range(GLA_SUB):
        if dl == 0:
            x = q * k
            vs = v
        else:
            sh = dl if not rev else c - dl
            valid = (off + dl < GLA_SUB) if rev else (off >= dl)
            ks = pltpu.roll(k, sh, 0)
            bs = pltpu.roll(b, sh, 0)
            vs = pltpu.roll(v, sh, 0)
            x = jnp.where(valid, q * ks * jnp.exp(jnp.where(valid, b - bs, 0.0)), 0.0)
            validv = (rowv + dl < GLA_SUB) if rev else (rowv >= dl)
            vs = jnp.where(validv, vs, 0.0)
        o = o + _dot(x.astype(BF16), red) * vs
    return o, st_new


def _gla_kernel(*refs, has_s0, n_tiles):
    it = iter(refs)
    qkf_ref, vf_ref, lrf_ref, qkb_ref, vb_ref, lrb_ref, gu_ref, gb_ref = (next(it) for _ in range(8))
    if has_s0:
        s0_ref = next(it)
    of_ref, ob_ref, st_ref, sf_scr, sb_scr = (next(it) for _ in range(5))

    t = pl.program_id(1)
    lane_h = _iota((GW, GLA_QK), 1) // GLA_DK
    row_h = _iota((GW, GLA_QK), 0) // HD
    blockdiag = row_h == lane_h

    @pl.when(t == 0)
    def _():
        if has_s0:
            for d, scr in ((0, sf_scr), (1, sb_scr)):
                s0t = jnp.concatenate([s0_ref[0, d].T] * HEADS, axis=0)
                scr[...] = jnp.where(blockdiag, s0t, 0.0)
        else:
            sf_scr[...] = jnp.zeros((GW, GLA_QK), F32)
            sb_scr[...] = jnp.zeros((GW, GLA_QK), F32)

    c = GLA_CHUNK
    n_chunks = SEQ_TILE // c

    def stream(d, qk_ref, v_ref, lr_ref, o_ref, scr):
        z = _dot(lr_ref[...].astype(BF16), gu_ref[d].astype(BF16)) + gb_ref[d]
        la = _log_sigmoid(z) / GLA_TAU
        st = scr[...]
        for cc in (range(n_chunks) if d == 0 else reversed(range(n_chunks))):
            rows = slice(cc * c, (cc + 1) * c)
            q = qk_ref[rows, 0:GLA_QK] * (GLA_DK ** -0.5)
            k = qk_ref[rows, GLA_QK:2 * GLA_QK]
            o, st = _gla_chunk(q, k, v_ref[rows, :], la[rows, :], st, rev=(d == 1))
            o_ref[rows, :] = o
        scr[...] = st

    stream(0, qkf_ref, vf_ref, lrf_ref, of_ref, sf_scr)
    stream(1, qkb_ref, vb_ref, lrb_ref, ob_ref, sb_scr)

    @pl.when(t == n_tiles - 1)
    def _():
        for d, scr in ((0, sf_scr), (1, sb_scr)):
            s = scr[...].T
            st_ref[0, d] = s[:, 0:HD] + s[:, HD:2 * HD] + s[:, 2 * HD:3 * HD] + s[:, 3 * HD:4 * HD]


def _gla(parts, gate_up_pad, gate_b, *, tok0, batch, seq_len, s0=None):
    n = SEQ_TILE
    nt = seq_len // n
    blk0 = tok0 // n
    fwd = lambda b, t: blk0 + b * nt + t
    bwd = lambda b, t: blk0 + b * nt + (nt - 1 - t)
    in_specs, args = [], []
    for m in (fwd, bwd):
        in_specs += [
            pl.BlockSpec((n, GW), lambda b, t, m=m: (m(b, t), CB_AQK)),
            pl.BlockSpec((n, GW), lambda b, t, m=m: (m(b, t), CB_AV)),
            pl.BlockSpec((n, 128), lambda b, t, m=m: (m(b, t), CB_LR128)),
        ]
        args += [parts] * 3
    in_specs += [pl.BlockSpec((2, 128, GLA_QK), lambda b, t: (0, 0, 0)),
                 pl.BlockSpec((2, 1, GLA_QK), lambda b, t: (0, 0, 0))]
    args += [gate_up_pad, gate_b]
    if s0 is not None:
        in_specs.append(pl.BlockSpec((1, 2, GLA_QK, HD), lambda b, t: (b, 0, 0, 0)))
        args.append(s0)
    kern = functools.partial(_gla_kernel, has_s0=s0 is not None, n_tiles=nt)
    return pl.pallas_call(
        kern,
        grid=(batch, nt),
        in_specs=in_specs,
        out_specs=[
            pl.BlockSpec((n, GW), lambda b, t: (b * nt + t, 0)),
            pl.BlockSpec((n, GW), lambda b, t: (b * nt + (nt - 1 - t), 0)),
            pl.BlockSpec((1, 2, GLA_QK, HD), lambda b, t: (b, 0, 0, 0)),
        ],
        out_shape=[
            jax.ShapeDtypeStruct((batch * seq_len, GW), F32),
            jax.ShapeDtypeStruct((batch * seq_len, GW), F32),
            jax.ShapeDtypeStruct((batch, 2, GLA_QK, HD), F32),
        ],
        scratch_shapes=[pltpu.VMEM((GW, GLA_QK), F32), pltpu.VMEM((GW, GLA_QK), F32)],
        compiler_params=_cparams(("arbitrary", "arbitrary")),
        name="gla",
    )(*args)


OUT_TM = 512


def _outproj_kernel(x_ref, mod_ref, gpost_ref, pool_ref, na_ref, rf_ref, rb_ref, rg_ref,
                    af_ref, ab_ref, ag_ref, ng_ref, w_ref, o_ref):
    avg = jnp.where(_iota((GW, GW), 0) // HD == _iota((GW, GW), 1) // HD, 1.0 / HD, 0.0).astype(BF16)
    r = rf_ref[...] + rb_ref[...]
    r = r - _head_mean(r, avg)
    r = r * lax.rsqrt(_head_mean(r * r, avg) + GN_EPS) * _silu(rg_ref[...])
    a = af_ref[...] + ab_ref[...]
    a = a * lax.rsqrt(_head_mean(a * a, avg) + RMS_EPS) * ng_ref[...] * _silu(ag_ref[...])
    y = _dot(pool_ref[...].astype(BF16), w_ref[0:GW, :])
    y = y + _dot(na_ref[...].astype(BF16), w_ref[GW:2 * GW, :])
    y = y + _dot(r.astype(BF16), w_ref[2 * GW:3 * GW, :])
    y = y + _dot(a.astype(BF16), w_ref[3 * GW:4 * GW, :])
    o_ref[...] = x_ref[...] + mod_ref[0, 2:3, :] * _rms(y, gpost_ref[...])


def _outproj(x, mod_l, g_post, o_pool, o_na, ret_f, ret_b, gla_f, gla_b, parts, ng_lanes, w_out_b):
    tm = OUT_TM
    tile = lambda i: (i, 0)
    act = pl.BlockSpec((tm, GW), tile)
    return pl.pallas_call(
        _outproj_kernel,
        grid=(T // tm,),
        in_specs=[
            pl.BlockSpec((tm, D), tile),
            pl.BlockSpec((1, 6, D), lambda i: (_mod_row(i, tm), 0, 0)),
            pl.BlockSpec((1, D), lambda i: (0, 0)),
            act, act, act, act,
            pl.BlockSpec((tm, GW), lambda i: (i, CB_RG)),
            act, act,
            pl.BlockSpec((tm, GW), lambda i: (i, CB_AG)),
            pl.BlockSpec((1, GW), lambda i: (0, 0)),
            pl.BlockSpec((D, D), lambda i: (0, 0)),
        ],
        out_specs=pl.BlockSpec((tm, D), tile),
        out_shape=jax.ShapeDtypeStruct((T, D), F32),
        compiler_params=_cparams(("parallel",)),
        name="outproj",
    )(x, mod_l, g_post, o_pool, o_na, ret_f, ret_b, parts, gla_f, gla_b, parts, ng_lanes, w_out_b)


FFN_TM = 512
FFN_TF = 1408


def _ffn_kernel(x_ref, mod_ref, gpre_ref, gpost_ref, wg_ref, wu_ref, wd_ref, o_ref, h_scr, acc_scr):
    j = pl.program_id(1)

    @pl.when(j == 0)
    def _():
        h = _rms(x_ref[...], gpre_ref[...]) * (1.0 + mod_ref[0, 4:5, :]) + mod_ref[0, 3:4, :]
        h_scr[...] = h.astype(BF16)
        acc_scr[...] = jnp.zeros_like(acc_scr)

    hb = h_scr[...]
    act = (_silu(_dot(hb, wg_ref[...])) * _dot(hb, wu_ref[...])).astype(BF16)
    acc_scr[...] += _dot(act, wd_ref[...])

    @pl.when(j == pl.num_programs(1) - 1)
    def _():
        o_ref[...] = x_ref[...] + mod_ref[0, 5:6, :] * _rms(acc_scr[...], gpost_ref[...])


def _ffn(x, mod_l, g_pre, g_post, wg, wu, wd):
    tm, tf = FFN_TM, FFN_TF
    return pl.pallas_call(
        _ffn_kernel,
        grid=(T // tm, D_FF // tf),
        in_specs=[
            pl.BlockSpec((tm, D), lambda i, j: (i, 0)),
            pl.BlockSpec((1, 6, D), lambda i, j: (_mod_row(i, tm), 0, 0)),
            pl.BlockSpec((1, D), lambda i, j: (0, 0)),
            pl.BlockSpec((1, D), lambda i, j: (0, 0)),
            pl.BlockSpec((D, tf), lambda i, j: (0, j)),
            pl.BlockSpec((D, tf), lambda i, j: (0, j)),
            pl.BlockSpec((tf, D), lambda i, j: (j, 0)),
        ],
        out_specs=pl.BlockSpec((tm, D), lambda i, j: (i, 0)),
        out_shape=jax.ShapeDtypeStruct((T, D), F32),
        scratch_shapes=[pltpu.VMEM((tm, D), BF16), pltpu.VMEM((tm, D), F32)],
        compiler_params=_cparams(("parallel", "arbitrary")),
        name="ffn",
    )(x, mod_l, g_pre, g_post, wg, wu, wd)


def _rope_tables():
    nf = 16
    inv = (ROPE_BASE ** (-np.arange(nf, dtype=np.float32) / nf)).astype(np.float32)
    tok = np.arange(L_LAT)
    cos = np.zeros((L_LAT, HD), np.float32)
    sin = np.zeros((L_LAT, HD), np.float32)
    for axis, pos in enumerate((tok // GRID_W, tok % GRID_W)):
        ang = pos.astype(np.float32)[:, None] * inv[None, :]
        c, s = np.cos(ang), np.sin(ang)
        cos[:, axis * 32:axis * 32 + 32] = np.concatenate([c, c], axis=1)
        sin[:, axis * 32:axis * 32 + 32] = np.concatenate([-s, s], axis=1)
    return jnp.asarray(np.tile(cos, (1, HEADS))), jnp.asarray(np.tile(sin, (1, HEADS)))


def _block_diag(w):
    g, c, _ = w.shape
    out = jnp.zeros((g * c, g * c), w.dtype)
    for i in range(g):
        out = out.at[i * c:(i + 1) * c, i * c:(i + 1) * c].set(w[i])
    return out


def kernel(x_prompt, x_sample, cache_na_k, cache_na_v, state_ret, state_gla, c, c_ctx, w_mod, b_mod,
           g_pre_mix, g_post_mix, g_pre_ffn, g_post_ffn, w_in, w_out, pool_w, pool_scale, na_rpb,
           ret_decay_logit, gla_gate_up, gla_gate_b, gla_norm_g, w_ffn_gate, w_ffn_up, w_ffn_down):
    x = jnp.concatenate([x_prompt.reshape(T_CTX, D), x_sample.reshape(T_LAT, D)], axis=0)
    cv8 = jnp.concatenate([c_ctx[None, :], c, jnp.zeros((8 - 1 - B_LAT, D), F32)], axis=0)
    mods = _modulation(cv8, w_mod, b_mod).reshape(DEPTH, 8, 6, D)

    w_in_b = jnp.pad(w_in, ((0, 0), (0, 0), (0, P_PAD - P_IN))).astype(BF16)
    w_out_b = w_out.astype(BF16)
    wg_b, wu_b, wd_b = w_ffn_gate.astype(BF16), w_ffn_up.astype(BF16), w_ffn_down.astype(BF16)
    gate_up_pad = jnp.pad(gla_gate_up, ((0, 0), (0, 0), (0, 128 - GLA_LOWRANK), (0, 0)))
    rope_tabs = _rope_tables()
    ck = cache_na_k.reshape(B_LAT, DEPTH, PAST, GW)
    cv = cache_na_v.reshape(B_LAT, DEPTH, PAST, GW)
    s0_ret = state_ret.reshape(B_LAT, DEPTH, 2, GW, HD)
    s0_gla = state_gla.reshape(B_LAT, DEPTH, 2, GLA_QK, HD)

    ks, vs, srs, sgs = [], [], [], []
    for l in range(DEPTH):
        mod_l = mods[l]
        parts = _inproj(x, mod_l, g_pre_mix[l][None, :], w_in_b[l])
        o_pool = _pool(parts, _block_diag(pool_w[l]).astype(BF16), pool_scale[l][None, :])
        o_na = jnp.concatenate([
            _ctx_attn(parts),
            _na_latent(parts, ck[:, l:l + 1], cv[:, l:l + 1], _na_bias_table(na_rpb[l])),
        ], axis=0)
        lg_lanes = jnp.repeat(ret_decay_logit[l], HD, axis=1)
        rf_c, rb_c, s_ret = _retention(parts, lg_lanes, tok0=0, batch=B_CTX, seq_len=L_CTX)
        rf_l, rb_l, _ = _retention(parts, lg_lanes, tok0=T_CTX, batch=B_LAT, seq_len=L_LAT,
                                   rope_tabs=rope_tabs, s0=s0_ret[:, l])
        gb = gla_gate_b[l][:, None, :]
        gf_c, gb_c, s_gla = _gla(parts, gate_up_pad[l], gb, tok0=0, batch=B_CTX, seq_len=L_CTX)
        gf_l, gb_l, _ = _gla(parts, gate_up_pad[l], gb, tok0=T_CTX, batch=B_LAT, seq_len=L_LAT,
                             s0=s0_gla[:, l])
        x = _outproj(x, mod_l, g_post_mix[l][None, :], o_pool, o_na,
                     jnp.concatenate([rf_c, rf_l], axis=0), jnp.concatenate([rb_c, rb_l], axis=0),
                     jnp.concatenate([gf_c, gf_l], axis=0), jnp.concatenate([gb_c, gb_l], axis=0),
                     parts, jnp.tile(gla_norm_g[l], HEADS)[None, :], w_out_b[l])
        x = _ffn(x, mod_l, g_pre_ffn[l][None, :], g_post_ffn[l][None, :], wg_b[l], wu_b[l], wd_b[l])
        ks.append(parts[:T_CTX, CB_NAK * GW:(CB_NAK + 1) * GW].reshape(B_CTX, L_CTX, HEADS, HD))
        vs.append(parts[:T_CTX, CB_NAV * GW:(CB_NAV + 1) * GW].reshape(B_CTX, L_CTX, HEADS, HD))
        srs.append(s_ret.reshape(B_CTX, 2, HEADS, RET_DK, HD))
        sgs.append(s_gla.reshape(B_CTX, 2, HEADS, GLA_DK, HD))

    return (x[:T_CTX].reshape(B_CTX, L_CTX, D), x[T_CTX:].reshape(B_LAT, L_LAT, D),
            jnp.stack(ks, axis=1), jnp.stack(vs, axis=1), jnp.stack(srs, axis=1), jnp.stack(sgs, axis=1))
```

```python
import functools

import numpy as np
import jax
import jax.numpy as jnp
from jax import lax
from jax.experimental import pallas as pl
from jax.experimental.pallas import tpu as pltpu

F32 = jnp.float32
BF16 = jnp.bfloat16

D = 1024
B_CTX, L_CTX = 32, 256
B_LAT, L_LAT = 2, 4096
DEPTH = 4
PAST = 256
GRID_W = 64
GRID_H = L_LAT // GRID_W
T_CTX = B_CTX * L_CTX
T_LAT = B_LAT * L_LAT
T = T_CTX + T_LAT
GW = 256
HEADS = 4
HD = 64
POOL_WINDOWS = (2, 4, 8, 16)
NA_ROWS, NA_COLS = 8, 16
RET_DK = 64
GLA_DK = 32
GLA_LOWRANK = 16
GLA_TAU = 16.0
D_FF = 2816
P_IN = 2832
P_PAD = 2944
ROPE_BASE = 10000.0
RMS_EPS = 1e-6
GN_EPS = 1e-5
NEG = -1e30

CB_POOL, CB_NAQ, CB_NAK, CB_NAV, CB_RQ, CB_RK, CB_RV, CB_RG, CB_AQK, CB_AV, CB_AG = range(11)
CB_LR128 = P_IN // 128

SEQ_TILE = 256
GLA_CHUNK = 64
GLA_SUB = 16
VMEM_LIMIT = 56 * 1024 * 1024


def _cparams(sem):
    return pltpu.CompilerParams(dimension_semantics=sem, vmem_limit_bytes=VMEM_LIMIT)


def _silu(x):
    return x / (1.0 + jnp.exp(-x))


def _log_sigmoid(z):
    return jnp.minimum(z, 0.0) - jnp.log1p(jnp.exp(-jnp.abs(z)))


def _rms(x, g):
    return x * lax.rsqrt(jnp.mean(x * x, axis=-1, keepdims=True) + RMS_EPS) * g


def _dot(a, b):
    return jnp.dot(a, b, preferred_element_type=F32)


def _dot_nt(a, b):
    return lax.dot_general(a, b, (((1,), (1,)), ((), ())), preferred_element_type=F32)


def _dot_tn(a, b):
    return lax.dot_general(a, b, (((0,), (0,)), ((), ())), preferred_element_type=F32)


def _split_hi_lo(x):
    hi = x.astype(BF16)
    return hi, (x - hi.astype(F32)).astype(BF16)


def _dot_exact01(a01, x):
    hi, lo = _split_hi_lo(x)
    return _dot(a01, hi) + _dot(a01, lo)


def _iota(shape, dim):
    return lax.broadcasted_iota(jnp.int32, shape, dim)


def _expand_heads(x, head_w):
    n, w = x.shape
    xe = jnp.concatenate([x] * HEADS, axis=0)
    rowh = _iota((HEADS * n, w), 0) // n
    laneh = (_iota((HEADS * n, w), 1) // head_w) % HEADS
    return jnp.where(rowh == laneh, xe, jnp.zeros_like(xe))


def _extract_heads(p, n):
    laneh = _iota((n, GW), 1) // HD
    out = p[0:n]
    for h in range(1, HEADS):
        out = jnp.where(laneh == h, p[h * n:(h + 1) * n], out)
    return out


def _head_mean(x, avg):
    hi, lo = _split_hi_lo(x)
    return _dot(hi, avg) + _dot(lo, avg)


def _mod_row(i, tm):
    return jnp.where(i < T_CTX // tm, 0, 1 + (i * tm - T_CTX) // L_LAT)


def _mod_kernel(cv_ref, w_ref, b_ref, o_ref):
    s = _silu(cv_ref[...]).astype(BF16)
    o_ref[0] = _dot(s, w_ref[0].astype(BF16)) + b_ref[0]


def _modulation(cv8, w_mod, b_mod):
    tn = 1536
    return pl.pallas_call(
        _mod_kernel,
        grid=(DEPTH, 6 * D // tn),
        in_specs=[
            pl.BlockSpec((8, D), lambda l, j: (0, 0)),
            pl.BlockSpec((1, D, tn), lambda l, j: (l, 0, j)),
            pl.BlockSpec((1, 1, tn), lambda l, j: (l, 0, j)),
        ],
        out_specs=pl.BlockSpec((1, 8, tn), lambda l, j: (l, 0, j)),
        out_shape=jax.ShapeDtypeStruct((DEPTH, 8, 6 * D), F32),
        compiler_params=_cparams(("arbitrary", "arbitrary")),
        name="modulation",
    )(cv8, w_mod, b_mod.reshape(DEPTH, 1, 6 * D))


IN_TM = 512


def _inproj_kernel(x_ref, mod_ref, g_ref, w_ref, o_ref):
    h = _rms(x_ref[...], g_ref[...]) * (1.0 + mod_ref[0, 1:2, :]) + mod_ref[0, 0:1, :]
    hb = h.astype(BF16)
    for a in range(0, P_PAD, 1024):
        b = min(a + 1024, P_PAD)
        o_ref[:, a:b] = _dot(hb, w_ref[:, a:b])


def _inproj(x, mod_l, g_pre, w_in_b):
    tm = IN_TM
    return pl.pallas_call(
        _inproj_kernel,
        grid=(T // tm,),
        in_specs=[
            pl.BlockSpec((tm, D), lambda i: (i, 0)),
            pl.BlockSpec((1, 6, D), lambda i: (_mod_row(i, tm), 0, 0)),
            pl.BlockSpec((1, D), lambda i: (0, 0)),
            pl.BlockSpec((D, P_PAD), lambda i: (0, 0)),
        ],
        out_specs=pl.BlockSpec((tm, P_PAD), lambda i: (i, 0)),
        out_shape=jax.ShapeDtypeStruct((T, P_PAD), F32),
        compiler_params=_cparams(("parallel",)),
        name="inproj",
    )(x, mod_l, g_pre, w_in_b)


def _pool_kernel(v_ref, w_ref, scale_ref, o_ref):
    i = pl.program_id(0)
    n = SEQ_TILE
    seg_mask = jnp.where(i < T_CTX // n, ~(L_CTX - 1), ~(GRID_W - 1))
    seg_len = jnp.where(i < T_CTX // n, L_CTX, GRID_W)
    v = v_ref[...]
    vh = v.astype(BF16)
    vl = (v - vh.astype(F32)).astype(BF16)
    t = _iota((n, n), 0)
    s = _iota((n, n), 1)
    seg0 = t & seg_mask
    seg1 = seg0 + seg_len
    lane_g = _iota((n, GW), 1) // HD
    mean = jnp.zeros((n, GW), F32)
    for gi, win in enumerate(POOL_WINDOWS):
        lo = jnp.maximum(t - win // 2, seg0)
        hi = jnp.minimum(t - win // 2 + win, seg1)
        w01 = jnp.where(s >= lo, jnp.where(s < hi, 1.0, 0.0), 0.0).astype(BF16)
        cnt = (hi - lo).astype(F32)
        m = (_dot(w01, vh) + _dot(w01, vl)) / cnt
        mean = jnp.where(lane_g == gi, m, mean)
    d = (mean - v).astype(BF16)
    o_ref[...] = _dot(d, w_ref[...]) * scale_ref[...]


def _pool(parts, w_bd, scale):
    n = SEQ_TILE
    return pl.pallas_call(
        _pool_kernel,
        grid=(T // n,),
        in_specs=[
            pl.BlockSpec((n, GW), lambda i: (i, CB_POOL)),
            pl.BlockSpec((GW, GW), lambda i: (0, 0)),
            pl.BlockSpec((1, GW), lambda i: (0, 0)),
        ],
        out_specs=pl.BlockSpec((n, GW), lambda i: (i, 0)),
        out_shape=jax.ShapeDtypeStruct((T, GW), F32),
        compiler_params=_cparams(("parallel",)),
        name="pool",
    )(parts, w_bd, scale)


def _softmax_rows(s):
    m = jnp.max(s, axis=-1, keepdims=True)
    p = jnp.exp(s - m)
    return p / jnp.sum(p, axis=-1, keepdims=True)


def _ctx_attn_kernel(q_ref, k_ref, v_ref, o_ref):
    n = L_CTX
    qe = _expand_heads(q_ref[...].astype(BF16), HD)
    s = _dot_nt(qe, k_ref[...].astype(BF16)) * (HD ** -0.5)
    p = _softmax_rows(s).astype(BF16)
    o_ref[...] = _extract_heads(_dot(p, v_ref[...].astype(BF16)), n)


def _ctx_attn(parts):
    n = L_CTX
    return pl.pallas_call(
        _ctx_attn_kernel,
        grid=(B_CTX,),
        in_specs=[
            pl.BlockSpec((n, GW), lambda b: (b, CB_NAQ)),
            pl.BlockSpec((n, GW), lambda b: (b, CB_NAK)),
            pl.BlockSpec((n, GW), lambda b: (b, CB_NAV)),
        ],
        out_specs=pl.BlockSpec((n, GW), lambda b: (b, 0)),
        out_shape=jax.ShapeDtypeStruct((T, GW), F32),
        compiler_params=_cparams(("parallel",)),
        name="ctx_attn",
    )(parts, parts, parts)


NA_ROWS_PER_STEP = 8
NA_WIN = NA_ROWS * GRID_W
NA_DR = 2 * NA_ROWS - 1
NA_DC = 2 * NA_COLS - 1


def _na_bias_table(rpb_ref, e2_ref):
    shape = (GRID_W, 2 * GRID_W)
    qc = _iota(shape, 0)
    lane = _iota(shape, 1)
    kc = lane % GRID_W
    upper = lane >= GRID_W
    c0 = jnp.clip(qc - NA_COLS // 2, 0, GRID_W - NA_COLS)
    dc = jnp.where((kc >= c0) & (kc < c0 + NA_COLS), kc - qc + (NA_COLS - 1), -1)

    def one(ha, carry):
        h = ha // (NA_DR - 1)
        a = ha % (NA_DR - 1)
        acc = jnp.full(shape, NEG, F32)
        for j in range(NA_DC):
            val = jnp.where(upper, rpb_ref[h * NA_DR + a + 1, j], rpb_ref[h * NA_DR + a, j])
            acc = jnp.where(dc == j, val, acc)
        e2_ref[h, a] = acc
        return carry

    lax.fori_loop(0, HEADS * (NA_DR - 1), one, 0)


def _na_kernel(q_ref, k_ref, v_ref, ck_ref, cv_ref, rpb_ref, prev_ref, o_ref, kb_ref, vb_ref, e2_ref):
    del prev_ref
    step = pl.program_id(1)

    @pl.when(step == 0)
    def _():
        kb_ref[...] = k_ref[...].astype(BF16)
        vb_ref[...] = v_ref[...].astype(BF16)
        _na_bias_table(rpb_ref, e2_ref)

    ckb = ck_ref[0, 0].astype(BF16)
    cvb = cv_ref[0, 0].astype(BF16)
    scale = HD ** -0.5

    def one_row(rr, carry):
        r = step * NA_ROWS_PER_STEP + rr
        r0 = jnp.clip(r - NA_ROWS // 2, 0, GRID_H - NA_ROWS)
        base = r0 - r + (NA_ROWS - 1)
        q0 = pl.multiple_of(rr * GRID_W, GRID_W)
        k0 = pl.multiple_of(r0 * GRID_W, GRID_W)
        qe = _expand_heads(q_ref[pl.ds(q0, GRID_W), :].astype(BF16), HD)
        kw = kb_ref[pl.ds(k0, NA_WIN), :]
        vw = vb_ref[pl.ds(k0, NA_WIN), :]
        bias = jnp.concatenate(
            [jnp.concatenate([e2_ref[h, base + 2 * p] for p in range(NA_ROWS // 2)], axis=1)
             for h in range(HEADS)], axis=0)
        s_loc = _dot_nt(qe, kw) * scale + bias
        s_ctx = _dot_nt(qe, ckb) * scale
        m = jnp.maximum(jnp.max(s_loc, axis=-1, keepdims=True), jnp.max(s_ctx, axis=-1, keepdims=True))
        p_loc = jnp.exp(s_loc - m)
        p_ctx = jnp.exp(s_ctx - m)
        inv = 1.0 / (jnp.sum(p_loc, axis=-1, keepdims=True) + jnp.sum(p_ctx, axis=-1, keepdims=True))
        pv = _dot((p_loc * inv).astype(BF16), vw) + _dot((p_ctx * inv).astype(BF16), cvb)
        o_ref[pl.ds(q0, GRID_W), :] = _extract_heads(pv, GRID_W)
        return carry

    lax.fori_loop(0, NA_ROWS_PER_STEP, one_row, 0)


def _na_latent(parts, ck, cv, rpb, layer, prev):
    rows = NA_ROWS_PER_STEP * GRID_W
    steps = L_LAT // rows
    row_blk0 = T_CTX // rows
    seq_blk0 = T_CTX // L_LAT
    return pl.pallas_call(
        _na_kernel,
        grid=(B_LAT, steps),
        in_specs=[
            pl.BlockSpec((rows, GW), lambda b, s: (row_blk0 + b * steps + s, CB_NAQ)),
            pl.BlockSpec((L_LAT, GW), lambda b, s: (seq_blk0 + b, CB_NAK)),
            pl.BlockSpec((L_LAT, GW), lambda b, s: (seq_blk0 + b, CB_NAV)),
            pl.BlockSpec((1, 1, PAST, GW), lambda b, s: (b, layer, 0, 0)),
            pl.BlockSpec((1, 1, PAST, GW), lambda b, s: (b, layer, 0, 0)),
            pl.BlockSpec(memory_space=pltpu.SMEM),
            pl.BlockSpec(memory_space=pl.ANY),
        ],
        out_specs=pl.BlockSpec((rows, GW), lambda b, s: (row_blk0 + b * steps + s, 0)),
        out_shape=jax.ShapeDtypeStruct((T, GW), F32),
        input_output_aliases={6: 0},
        scratch_shapes=[pltpu.VMEM((L_LAT, GW), BF16), pltpu.VMEM((L_LAT, GW), BF16),
                        pltpu.VMEM((HEADS, NA_DR - 1, GRID_W, 2 * GRID_W), F32)],
        compiler_params=_cparams(("arbitrary", "arbitrary")),
        name="na_latent",
    )(parts, parts, parts, ck, cv, rpb.reshape(HEADS * NA_DR, NA_DC), prev)


def _rope(x, cos, sin_signed):
    lane = _iota(x.shape, 1)
    partner = jnp.where(lane % 32 < 16, pltpu.roll(x, GW - 16, 1), pltpu.roll(x, 16, 1))
    return x * cos + partner * sin_signed


def _ret_kernel(*refs, rope, has_s0, n_tiles, n_prev):
    it = iter(refs)
    qf_ref, kf_ref, vf_ref, qb_ref, kb_ref, vb_ref = (next(it) for _ in range(6))
    if rope:
        cf_ref, sf_ref, cb_ref, sb_ref = (next(it) for _ in range(4))
    lg_ref = next(it)
    if has_s0:
        s0_ref = next(it)
    for _ in range(n_prev):
        next(it)
    of_ref, ob_ref, st_ref, sf_scr, sb_scr = (next(it) for _ in range(5))

    t = pl.program_id(1)
    n = SEQ_TILE
    lane_h = _iota((GW, GW), 1) // HD
    row_h = _iota((GW, GW), 0) // HD
    blockdiag = row_h == lane_h

    @pl.when(t == 0)
    def _():
        if has_s0:
            for d, scr in ((0, sf_scr), (1, sb_scr)):
                s0 = jnp.concatenate([s0_ref[0, d]] * HEADS, axis=1)
                scr[...] = jnp.where(blockdiag, s0, 0.0)
        else:
            sf_scr[...] = jnp.zeros((GW, GW), F32)
            sb_scr[...] = jnp.zeros((GW, GW), F32)

    lg = _log_sigmoid(lg_ref[...])
    ti = _iota((n, GW), 0).astype(F32)
    i_ = _iota((n, n), 0).astype(F32)
    j_ = _iota((n, n), 1).astype(F32)

    def stream(d, q_ref, k_ref, v_ref, c_ref, s_ref, o_ref, scr):
        lgd = lg[d:d + 1, :]
        q = q_ref[...] * (RET_DK ** -0.5)
        k = k_ref[...]
        if rope:
            q = _rope(q, c_ref[...], s_ref[...])
            k = _rope(k, c_ref[...], s_ref[...])
        vb16 = v_ref[...].astype(BF16)
        diff = (i_ - j_) if d == 0 else (j_ - i_)
        pos = ti if d == 0 else (n - 1.0) - ti
        w = jnp.concatenate(
            [jnp.where(diff >= 0, jnp.exp(jnp.maximum(diff, 0.0) * lgd[:, h * HD:h * HD + 1]), 0.0)
             for h in range(HEADS)], axis=0)
        a = _dot_nt(_expand_heads(q.astype(BF16), HD), k.astype(BF16)) * w
        o = _extract_heads(_dot(a.astype(BF16), vb16), n)
        s_old = scr[...]
        o = o + _dot(q.astype(BF16), s_old.astype(BF16)) * jnp.exp((pos + 1.0) * lgd)
        o_ref[...] = o
        kd = (k * jnp.exp((n - 1.0 - pos) * lgd)).astype(BF16)
        upd = _dot_tn(kd, vb16)
        scr[...] = s_old * jnp.exp(float(n) * lgd) + jnp.where(blockdiag, upd, 0.0)

    stream(0, qf_ref, kf_ref, vf_ref, cf_ref if rope else None, sf_ref if rope else None, of_ref, sf_scr)
    stream(1, qb_ref, kb_ref, vb_ref, cb_ref if rope else None, sb_ref if rope else None, ob_ref, sb_scr)

    @pl.when(t == n_tiles - 1)
    def _():
        for d, scr in ((0, sf_scr), (1, sb_scr)):
            s = scr[...]
            st_ref[0, d] = s[:, 0:HD] + s[:, HD:2 * HD] + s[:, 2 * HD:3 * HD] + s[:, 3 * HD:4 * HD]


def _retention(parts, lg_lanes, *, tok0, batch, seq_len, rope_tabs=None, s0=None, prev=()):
    n = SEQ_TILE
    nt = seq_len // n
    blk0 = tok0 // n
    fwd = lambda b, t: blk0 + b * nt + t
    bwd = lambda b, t: blk0 + b * nt + (nt - 1 - t)
    in_specs = [pl.BlockSpec((n, GW), lambda b, t, c=c: (fwd(b, t), c)) for c in (CB_RQ, CB_RK, CB_RV)]
    in_specs += [pl.BlockSpec((n, GW), lambda b, t, c=c: (bwd(b, t), c)) for c in (CB_RQ, CB_RK, CB_RV)]
    args = [parts] * 6
    if rope_tabs is not None:
        in_specs += [pl.BlockSpec((n, GW), lambda b, t: (t, 0))] * 2
        in_specs += [pl.BlockSpec((n, GW), lambda b, t: (nt - 1 - t, 0))] * 2
        args += [rope_tabs[0], rope_tabs[1], rope_tabs[0], rope_tabs[1]]
    in_specs.append(pl.BlockSpec((2, GW), lambda b, t: (0, 0)))
    args.append(lg_lanes)
    if s0 is not None:
        in_specs.append(pl.BlockSpec((1, 2, GW, HD), lambda b, t: (b, 0, 0, 0)))
        args.append(s0)
    aliases = {len(args) + i: i for i in range(len(prev))}
    in_specs += [pl.BlockSpec(memory_space=pl.ANY)] * len(prev)
    args += list(prev)
    kern = functools.partial(_ret_kernel, rope=rope_tabs is not None, has_s0=s0 is not None, n_tiles=nt,
                             n_prev=len(prev))
    return pl.pallas_call(
        kern,
        grid=(batch, nt),
        in_specs=in_specs,
        out_specs=[
            pl.BlockSpec((n, GW), lambda b, t: (fwd(b, t), 0)),
            pl.BlockSpec((n, GW), lambda b, t: (bwd(b, t), 0)),
            pl.BlockSpec((1, 2, GW, HD), lambda b, t: (b, 0, 0, 0)),
        ],
        out_shape=[
            jax.ShapeDtypeStruct((T, GW), F32),
            jax.ShapeDtypeStruct((T, GW), F32),
            jax.ShapeDtypeStruct((batch, 2, GW, HD), F32),
        ],
        input_output_aliases=aliases,
        scratch_shapes=[pltpu.VMEM((GW, GW), F32), pltpu.VMEM((GW, GW), F32)],
        compiler_params=_cparams(("arbitrary", "arbitrary")),
        name="retention",
    )(*args)


GLA_QK = HEADS * GLA_DK
N_SUB = GLA_CHUNK // GLA_SUB
GLA_SAFE_DECAY = 60.0


def _gla_chunk(q, k, v, la, st, rev, exact):
    c = GLA_CHUNK
    ri = _iota((c, c), 0)
    ci = _iota((c, c), 1)
    tri = jnp.where((ci >= ri) if rev else (ci <= ri), 1.0, 0.0).astype(BF16)
    b = _dot_exact01(tri, la)
    b_end = b[0:1, :] if rev else b[c - 1:c, :]
    row = _iota((c, GLA_QK), 0)
    sub = row // GLA_SUB
    off = row % GLA_SUB

    o = _dot_nt((q * jnp.exp(b)).astype(BF16), st.astype(BF16))
    kt = (k * jnp.exp(b_end - b)).astype(BF16)
    lane_h = _iota((GW, GLA_QK), 1) // GLA_DK
    row_h = _iota((GW, GLA_QK), 0) // HD
    st_new = st * jnp.exp(b_end) + jnp.where(row_h == lane_h, _dot_tn(v.astype(BF16), kt), 0.0)

    if not exact:
        zero_row = jnp.zeros((1, GLA_QK), F32)
        if rev:
            refs = [b[(s + 1) * GLA_SUB:(s + 1) * GLA_SUB + 1, :] for s in range(N_SUB - 1)] + [zero_row]
        else:
            refs = [zero_row] + [b[s * GLA_SUB - 1:s * GLA_SUB, :] for s in range(1, N_SUB)]
        own_ref = jnp.concatenate([jnp.broadcast_to(r, (GLA_SUB, GLA_QK)) for r in refs], axis=0)
        qh = q * jnp.exp(b - own_ref)
        q_parts, k_parts = [], []
        for s in range(N_SUB):
            kvalid = (sub >= s) if rev else (sub <= s)
            q_parts.append(jnp.where(sub == s, qh, 0.0))
            k_parts.append(jnp.where(kvalid, k * jnp.exp(jnp.minimum(refs[s] - b, GLA_SAFE_DECAY)), 0.0))
        q_cat = _expand_heads(jnp.concatenate(q_parts, axis=1).astype(BF16), GLA_DK)
        k_cat = jnp.concatenate(k_parts, axis=1).astype(BF16)
        a = _dot_nt(q_cat, k_cat)
        qi = _iota((HEADS * c, c), 0) % c
        kj = _iota((HEADS * c, c), 1)
        a = jnp.where((kj >= qi) if rev else (kj <= qi), a, 0.0)
        return o + _extract_heads(_dot(a.astype(BF16), v.astype(BF16)), c), st_new

    q_parts, k_parts = [], []
    for s in range(1, N_SUB):
        if rev:
            qsub, brow = N_SUB - 1 - s, b[(N_SUB - s) * GLA_SUB:(N_SUB - s) * GLA_SUB + 1, :]
            kvalid = sub > qsub
        else:
            qsub, brow = s, b[s * GLA_SUB - 1:s * GLA_SUB, :]
            kvalid = sub < qsub
        q_parts.append(jnp.where(sub == qsub, q * jnp.exp(jnp.where(sub == qsub, b - brow, 0.0)), 0.0))
        k_parts.append(jnp.where(kvalid, k * jnp.exp(jnp.where(kvalid, brow - b, 0.0)), 0.0))
    q_cat = _expand_heads(jnp.concatenate(q_parts, axis=1).astype(BF16), GLA_DK)
    k_cat = jnp.concatenate(k_parts, axis=1).astype(BF16)
    a_off = _dot_nt(q_cat, k_cat)
    o = o + _extract_heads(_dot(a_off.astype(BF16), v.astype(BF16)), c)

    red = jnp.where(_iota((GLA_QK, GW), 0) // GLA_DK == _iota((GLA_QK, GW), 1) // HD, 1.0, 0.0).astype(BF16)
    rowv = _iota((c, GW), 0) % GLA_SUB
    for dl in range(GLA_SUB):
        if dl == 0:
            x = q * k
            vs = v
        else:
            sh = dl if not rev else c - dl
            valid = (off + dl < GLA_SUB) if rev else (off >= dl)
            ks = pltpu.roll(k, sh, 0)
            bs = pltpu.roll(b, sh, 0)
            vs = pltpu.roll(v, sh, 0)
            x = jnp.where(valid, q * ks * jnp.exp(jnp.where(valid, b - bs, 0.0)), 0.0)
            validv = (rowv + dl < GLA_SUB) if rev else (rowv >= dl)
            vs = jnp.where(validv, vs, 0.0)
        o = o + _dot(x.astype(BF16), red) * vs
    return o, st_new


def _gla_kernel(*refs, has_s0, n_tiles, n_prev):
    it = iter(refs)
    qkf_ref, vf_ref, lrf_ref, qkb_ref, vb_ref, lrb_ref, gu_ref, gb_ref = (next(it) for _ in range(8))
    if has_s0:
        s0_ref = next(it)
    for _ in range(n_prev):
        next(it)
    of_ref, ob_ref, st_ref, sf_scr, sb_scr = (next(it) for _ in range(5))

    t = pl.program_id(1)
    lane_h = _iota((GW, GLA_QK), 1) // GLA_DK
    row_h = _iota((GW, GLA_QK), 0) // HD
    blockdiag = row_h == lane_h

    @pl.when(t == 0)
    def _():
        if has_s0:
            for d, scr in ((0, sf_scr), (1, sb_scr)):
                s0t = jnp.concatenate([s0_ref[0, d].T] * HEADS, axis=0)
                scr[...] = jnp.where(blockdiag, s0t, 0.0)
        else:
            sf_scr[...] = jnp.zeros((GW, GLA_QK), F32)
            sb_scr[...] = jnp.zeros((GW, GLA_QK), F32)

    c = GLA_CHUNK
    n_chunks = SEQ_TILE // c

    def log_gate(d, lr_ref):
        z = _dot(lr_ref[...].astype(BF16), gu_ref[d].astype(BF16)) + gb_ref[d]
        return _log_sigmoid(z) / GLA_TAU

    def stream(d, qk_ref, v_ref, la, o_ref, scr, exact):
        st = scr[...]
        for cc in (range(n_chunks) if d == 0 else reversed(range(n_chunks))):
            rows = slice(cc * c, (cc + 1) * c)
            q = qk_ref[rows, 0:GLA_QK] * (GLA_DK ** -0.5)
            k = qk_ref[rows, GLA_QK:2 * GLA_QK]
            o, st = _gla_chunk(q, k, v_ref[rows, :], la[rows, :], st, rev=(d == 1), exact=exact)
            o_ref[rows, :] = o
        scr[...] = st

    la_f = log_gate(0, lrf_ref)
    la_b = log_gate(1, lrb_ref)
    decay = GLA_SUB * jnp.max(jnp.maximum(-la_f, -la_b))

    for exact in (False, True):
        @pl.when((decay > GLA_SAFE_DECAY) if exact else (decay <= GLA_SAFE_DECAY))
        def _():
            stream(0, qkf_ref, vf_ref, la_f, of_ref, sf_scr, exact)
            stream(1, qkb_ref, vb_ref, la_b, ob_ref, sb_scr, exact)

    @pl.when(t == n_tiles - 1)
    def _():
        for d, scr in ((0, sf_scr), (1, sb_scr)):
            s = scr[...].T
            st_ref[0, d] = s[:, 0:HD] + s[:, HD:2 * HD] + s[:, 2 * HD:3 * HD] + s[:, 3 * HD:4 * HD]


def _gla(parts, gate_up_pad, gate_b, *, tok0, batch, seq_len, s0=None, prev=()):
    n = SEQ_TILE
    nt = seq_len // n
    blk0 = tok0 // n
    fwd = lambda b, t: blk0 + b * nt + t
    bwd = lambda b, t: blk0 + b * nt + (nt - 1 - t)
    in_specs, args = [], []
    for m in (fwd, bwd):
        in_specs += [
            pl.BlockSpec((n, GW), lambda b, t, m=m: (m(b, t), CB_AQK)),
            pl.BlockSpec((n, GW), lambda b, t, m=m: (m(b, t), CB_AV)),
            pl.BlockSpec((n, 128), lambda b, t, m=m: (m(b, t), CB_LR128)),
        ]
        args += [parts] * 3
    in_specs += [pl.BlockSpec((2, 128, GLA_QK), lambda b, t: (0, 0, 0)),
                 pl.BlockSpec((2, 1, GLA_QK), lambda b, t: (0, 0, 0))]
    args += [gate_up_pad, gate_b]
    if s0 is not None:
        in_specs.append(pl.BlockSpec((1, 2, GLA_QK, HD), lambda b, t: (b, 0, 0, 0)))
        args.append(s0)
    aliases = {len(args) + i: i for i in range(len(prev))}
    in_specs += [pl.BlockSpec(memory_space=pl.ANY)] * len(prev)
    args += list(prev)
    kern = functools.partial(_gla_kernel, has_s0=s0 is not None, n_tiles=nt, n_prev=len(prev))
    return pl.pallas_call(
        kern,
        grid=(batch, nt),
        in_specs=in_specs,
        out_specs=[
            pl.BlockSpec((n, GW), lambda b, t: (fwd(b, t), 0)),
            pl.BlockSpec((n, GW), lambda b, t: (bwd(b, t), 0)),
            pl.BlockSpec((1, 2, GLA_QK, HD), lambda b, t: (b, 0, 0, 0)),
        ],
        out_shape=[
            jax.ShapeDtypeStruct((T, GW), F32),
            jax.ShapeDtypeStruct((T, GW), F32),
            jax.ShapeDtypeStruct((batch, 2, GLA_QK, HD), F32),
        ],
        input_output_aliases=aliases,
        scratch_shapes=[pltpu.VMEM((GW, GLA_QK), F32), pltpu.VMEM((GW, GLA_QK), F32)],
        compiler_params=_cparams(("arbitrary", "arbitrary")),
        name="gla",
    )(*args)


OUT_TM = 512


def _outproj_kernel(x_ref, mod_ref, gpost_ref, pool_ref, na_ref, rf_ref, rb_ref, rg_ref,
                    af_ref, ab_ref, ag_ref, ng_ref, w_ref, o_ref):
    avg = jnp.where(_iota((GW, GW), 0) // HD == _iota((GW, GW), 1) // HD, 1.0 / HD, 0.0).astype(BF16)
    r = rf_ref[...] + rb_ref[...]
    r = r - _head_mean(r, avg)
    r = r * lax.rsqrt(_head_mean(r * r, avg) + GN_EPS) * _silu(rg_ref[...])
    a = af_ref[...] + ab_ref[...]
    a = a * lax.rsqrt(_head_mean(a * a, avg) + RMS_EPS) * ng_ref[...] * _silu(ag_ref[...])
    y = _dot(pool_ref[...].astype(BF16), w_ref[0:GW, :])
    y = y + _dot(na_ref[...].astype(BF16), w_ref[GW:2 * GW, :])
    y = y + _dot(r.astype(BF16), w_ref[2 * GW:3 * GW, :])
    y = y + _dot(a.astype(BF16), w_ref[3 * GW:4 * GW, :])
    o_ref[...] = x_ref[...] + mod_ref[0, 2:3, :] * _rms(y, gpost_ref[...])


def _outproj(x, mod_l, g_post, o_pool, o_na, ret_f, ret_b, gla_f, gla_b, parts, ng_lanes, w_out_b):
    tm = OUT_TM
    tile = lambda i: (i, 0)
    act = pl.BlockSpec((tm, GW), tile)
    return pl.pallas_call(
        _outproj_kernel,
        grid=(T // tm,),
        in_specs=[
            pl.BlockSpec((tm, D), tile),
            pl.BlockSpec((1, 6, D), lambda i: (_mod_row(i, tm), 0, 0)),
            pl.BlockSpec((1, D), lambda i: (0, 0)),
            act, act, act, act,
            pl.BlockSpec((tm, GW), lambda i: (i, CB_RG)),
            act, act,
            pl.BlockSpec((tm, GW), lambda i: (i, CB_AG)),
            pl.BlockSpec((1, GW), lambda i: (0, 0)),
            pl.BlockSpec((D, D), lambda i: (0, 0)),
        ],
        out_specs=pl.BlockSpec((tm, D), tile),
        out_shape=jax.ShapeDtypeStruct((T, D), F32),
        compiler_params=_cparams(("parallel",)),
        name="outproj",
    )(x, mod_l, g_post, o_pool, o_na, ret_f, ret_b, parts, gla_f, gla_b, parts, ng_lanes, w_out_b)


FFN_TM = 512
FFN_TF = 1408


def _ffn_kernel(x_ref, mod_ref, gpre_ref, gpost_ref, wg_ref, wu_ref, wd_ref, o_ref, h_scr, acc_scr):
    j = pl.program_id(1)

    @pl.when(j == 0)
    def _():
        h = _rms(x_ref[...], gpre_ref[...]) * (1.0 + mod_ref[0, 4:5, :]) + mod_ref[0, 3:4, :]
        h_scr[...] = h.astype(BF16)
        acc_scr[...] = jnp.zeros_like(acc_scr)

    hb = h_scr[...]
    act = (_silu(_dot(hb, wg_ref[...])) * _dot(hb, wu_ref[...])).astype(BF16)
    acc_scr[...] += _dot(act, wd_ref[...])

    @pl.when(j == pl.num_programs(1) - 1)
    def _():
        o_ref[...] = x_ref[...] + mod_ref[0, 5:6, :] * _rms(acc_scr[...], gpost_ref[...])


def _ffn(x, mod_l, g_pre, g_post, wg, wu, wd):
    tm, tf = FFN_TM, FFN_TF
    return pl.pallas_call(
        _ffn_kernel,
        grid=(T // tm, D_FF // tf),
        in_specs=[
            pl.BlockSpec((tm, D), lambda i, j: (i, 0)),
            pl.BlockSpec((1, 6, D), lambda i, j: (_mod_row(i, tm), 0, 0)),
            pl.BlockSpec((1, D), lambda i, j: (0, 0)),
            pl.BlockSpec((1, D), lambda i, j: (0, 0)),
            pl.BlockSpec((D, tf), lambda i, j: (0, j)),
            pl.BlockSpec((D, tf), lambda i, j: (0, j)),
            pl.BlockSpec((tf, D), lambda i, j: (j, 0)),
        ],
        out_specs=pl.BlockSpec((tm, D), lambda i, j: (i, 0)),
        out_shape=jax.ShapeDtypeStruct((T, D), F32),
        scratch_shapes=[pltpu.VMEM((tm, D), BF16), pltpu.VMEM((tm, D), F32)],
        compiler_params=_cparams(("parallel", "arbitrary")),
        name="ffn",
    )(x, mod_l, g_pre, g_post, wg, wu, wd)


def _rope_tables():
    nf = 16
    inv = (ROPE_BASE ** (-np.arange(nf, dtype=np.float32) / nf)).astype(np.float32)
    tok = np.arange(L_LAT)
    cos = np.zeros((L_LAT, HD), np.float32)
    sin = np.zeros((L_LAT, HD), np.float32)
    for axis, pos in enumerate((tok // GRID_W, tok % GRID_W)):
        ang = pos.astype(np.float32)[:, None] * inv[None, :]
        c, s = np.cos(ang), np.sin(ang)
        cos[:, axis * 32:axis * 32 + 32] = np.concatenate([c, c], axis=1)
        sin[:, axis * 32:axis * 32 + 32] = np.concatenate([-s, s], axis=1)
    return jnp.asarray(np.tile(cos, (1, HEADS))), jnp.asarray(np.tile(sin, (1, HEADS)))


def _block_diag(w):
    g, c, _ = w.shape
    out = jnp.zeros((g * c, g * c), w.dtype)
    for i in range(g):
        out = out.at[i * c:(i + 1) * c, i * c:(i + 1) * c].set(w[i])
    return out


def kernel(x_prompt, x_sample, cache_na_k, cache_na_v, state_ret, state_gla, c, c_ctx, w_mod, b_mod,
           g_pre_mix, g_post_mix, g_pre_ffn, g_post_ffn, w_in, w_out, pool_w, pool_scale, na_rpb,
           ret_decay_logit, gla_gate_up, gla_gate_b, gla_norm_g, w_ffn_gate, w_ffn_up, w_ffn_down):
    x = jnp.concatenate([x_prompt.reshape(T_CTX, D), x_sample.reshape(T_LAT, D)], axis=0)
    cv8 = jnp.concatenate([c_ctx[None, :], c, jnp.zeros((8 - 1 - B_LAT, D), F32)], axis=0)
    mods = _modulation(cv8, w_mod, b_mod).reshape(DEPTH, 8, 6, D)

    w_in_b = jnp.pad(w_in, ((0, 0), (0, 0), (0, P_PAD - P_IN))).astype(BF16)
    w_out_b = w_out.astype(BF16)
    wg_b, wu_b, wd_b = w_ffn_gate.astype(BF16), w_ffn_up.astype(BF16), w_ffn_down.astype(BF16)
    gate_up_pad = jnp.pad(gla_gate_up, ((0, 0), (0, 0), (0, 128 - GLA_LOWRANK), (0, 0)))
    rope_tabs = _rope_tables()
    ck = cache_na_k.reshape(B_LAT, DEPTH, PAST, GW)
    cv = cache_na_v.reshape(B_LAT, DEPTH, PAST, GW)
    s0_ret = state_ret.reshape(B_LAT, DEPTH, 2, GW, HD)
    s0_gla = state_gla.reshape(B_LAT, DEPTH, 2, GLA_QK, HD)

    ks, vs, srs, sgs = [], [], [], []
    for l in range(DEPTH):
        mod_l = mods[l]
        parts = _inproj(x, mod_l, g_pre_mix[l][None, :], w_in_b[l])
        o_pool = _pool(parts, _block_diag(pool_w[l]).astype(BF16), pool_scale[l][None, :])
        o_na = _na_latent(parts, ck, cv, na_rpb[l], l, _ctx_attn(parts))
        lg_lanes = jnp.repeat(ret_decay_logit[l], HD, axis=1)
        rf, rb, s_ret = _retention(parts, lg_lanes, tok0=0, batch=B_CTX, seq_len=L_CTX)
        rf, rb, _ = _retention(parts, lg_lanes, tok0=T_CTX, batch=B_LAT, seq_len=L_LAT,
                               rope_tabs=rope_tabs, s0=s0_ret[:, l], prev=(rf, rb))
        gb = gla_gate_b[l][:, None, :]
        gf, gbw, s_gla = _gla(parts, gate_up_pad[l], gb, tok0=0, batch=B_CTX, seq_len=L_CTX)
        gf, gbw, _ = _gla(parts, gate_up_pad[l], gb, tok0=T_CTX, batch=B_LAT, seq_len=L_LAT,
                          s0=s0_gla[:, l], prev=(gf, gbw))
        x = _outproj(x, mod_l, g_post_mix[l][None, :], o_pool, o_na, rf, rb, gf, gbw,
                     parts, jnp.tile(gla_norm_g[l], HEADS)[None, :], w_out_b[l])
        x = _ffn(x, mod_l, g_pre_ffn[l][None, :], g_post_ffn[l][None, :], wg_b[l], wu_b[l], wd_b[l])
        ks.append(parts[:T_CTX, CB_NAK * GW:(CB_NAK + 1) * GW].reshape(B_CTX, L_CTX, HEADS, HD))
        vs.append(parts[:T_CTX, CB_NAV * GW:(CB_NAV + 1) * GW].reshape(B_CTX, L_CTX, HEADS, HD))
        srs.append(s_ret.reshape(B_CTX, 2, HEADS, RET_DK, HD))
        sgs.append(s_gla.reshape(B_CTX, 2, HEADS, GLA_DK, HD))

    return (x[:T_CTX].reshape(B_CTX, L_CTX, D), x[T_CTX:].reshape(B_LAT, L_LAT, D),
            jnp.stack(ks, axis=1), jnp.stack(vs, axis=1), jnp.stack(srs, axis=1), jnp.stack(sgs, axis=1))
```

```python
import functools

import numpy as np
import jax
import jax.numpy as jnp
from jax import lax
from jax.experimental import pallas as pl
from jax.experimental.pallas import tpu as pltpu

F32 = jnp.float32
BF16 = jnp.bfloat16

D = 1024
B_CTX, L_CTX = 32, 256
B_LAT, L_LAT = 2, 4096
DEPTH = 4
PAST = 256
GRID_W = 64
GRID_H = L_LAT // GRID_W
T_CTX = B_CTX * L_CTX
T_LAT = B_LAT * L_LAT
T = T_CTX + T_LAT
GW = 256
HEADS = 4
HD = 64
POOL_WINDOWS = (2, 4, 8, 16)
NA_ROWS, NA_COLS = 8, 16
RET_DK = 64
GLA_DK = 32
GLA_LOWRANK = 16
GLA_TAU = 16.0
D_FF = 2816
P_IN = 2832
P_PAD = 2944
ROPE_BASE = 10000.0
RMS_EPS = 1e-6
GN_EPS = 1e-5
NEG = -1e30

CB_POOL, CB_NAQ, CB_NAK, CB_NAV, CB_RQ, CB_RK, CB_RV, CB_RG, CB_AQK, CB_AV, CB_AG = range(11)
CB_LR128 = P_IN // 128

SEQ_TILE = 256
GLA_CHUNK = 64
GLA_SUB = 16
VMEM_LIMIT = 56 * 1024 * 1024


def _cparams(sem):
    return pltpu.CompilerParams(dimension_semantics=sem, vmem_limit_bytes=VMEM_LIMIT)


def _silu(x):
    return x / (1.0 + jnp.exp(-x))


def _log_sigmoid(z):
    return jnp.minimum(z, 0.0) - jnp.log1p(jnp.exp(-jnp.abs(z)))


def _rms(x, g):
    return x * lax.rsqrt(jnp.mean(x * x, axis=-1, keepdims=True) + RMS_EPS) * g


def _dot(a, b):
    return jnp.dot(a, b, preferred_element_type=F32)


def _dot_nt(a, b):
    return lax.dot_general(a, b, (((1,), (1,)), ((), ())), preferred_element_type=F32)


def _dot_tn(a, b):
    return lax.dot_general(a, b, (((0,), (0,)), ((), ())), preferred_element_type=F32)


def _split_hi_lo(x):
    hi = x.astype(BF16)
    return hi, (x - hi.astype(F32)).astype(BF16)


def _dot_exact01(a01, x):
    hi, lo = _split_hi_lo(x)
    return _dot(a01, hi) + _dot(a01, lo)


def _iota(shape, dim):
    return lax.broadcasted_iota(jnp.int32, shape, dim)


def _expand_heads(x, head_w):
    n, w = x.shape
    xe = jnp.concatenate([x] * HEADS, axis=0)
    rowh = _iota((HEADS * n, w), 0) // n
    laneh = (_iota((HEADS * n, w), 1) // head_w) % HEADS
    return jnp.where(rowh == laneh, xe, jnp.zeros_like(xe))


def _extract_heads(p, n):
    laneh = _iota((n, GW), 1) // HD
    out = p[0:n]
    for h in range(1, HEADS):
        out = jnp.where(laneh == h, p[h * n:(h + 1) * n], out)
    return out


def _head_mean(x, avg):
    hi, lo = _split_hi_lo(x)
    return _dot(hi, avg) + _dot(lo, avg)


def _mod_row(i, tm):
    return jnp.where(i < T_CTX // tm, 0, 1 + (i * tm - T_CTX) // L_LAT)


def _mod_kernel(cv_ref, w_ref, b_ref, o_ref):
    s = _silu(cv_ref[...]).astype(BF16)
    o_ref[0] = _dot(s, w_ref[0].astype(BF16)) + b_ref[0]


def _modulation(cv8, w_mod, b_mod):
    tn = 1536
    return pl.pallas_call(
        _mod_kernel,
        grid=(DEPTH, 6 * D // tn),
        in_specs=[
            pl.BlockSpec((8, D), lambda l, j: (0, 0)),
            pl.BlockSpec((1, D, tn), lambda l, j: (l, 0, j)),
            pl.BlockSpec((1, 1, tn), lambda l, j: (l, 0, j)),
        ],
        out_specs=pl.BlockSpec((1, 8, tn), lambda l, j: (l, 0, j)),
        out_shape=jax.ShapeDtypeStruct((DEPTH, 8, 6 * D), F32),
        compiler_params=_cparams(("arbitrary", "arbitrary")),
        name="modulation",
    )(cv8, w_mod, b_mod.reshape(DEPTH, 1, 6 * D))


IN_TM = 512


def _inproj_kernel(x_ref, mod_ref, g_ref, w_ref, o_ref):
    h = _rms(x_ref[...], g_ref[...]) * (1.0 + mod_ref[0, 1:2, :]) + mod_ref[0, 0:1, :]
    hb = h.astype(BF16)
    for a in range(0, P_PAD, 1024):
        b = min(a + 1024, P_PAD)
        o_ref[:, a:b] = _dot(hb, w_ref[0, :, a:b])


def _inproj(x, mod_l, g_pre, w_in_b, layer):
    tm = IN_TM
    return pl.pallas_call(
        _inproj_kernel,
        grid=(T // tm,),
        in_specs=[
            pl.BlockSpec((tm, D), lambda i: (i, 0)),
            pl.BlockSpec((1, 6, D), lambda i: (_mod_row(i, tm), 0, 0)),
            pl.BlockSpec((1, D), lambda i: (0, 0)),
            pl.BlockSpec((1, D, P_PAD), lambda i: (layer, 0, 0), pipeline_mode=pl.Buffered(1)),
        ],
        out_specs=pl.BlockSpec((tm, P_PAD), lambda i: (i, 0)),
        out_shape=jax.ShapeDtypeStruct((T, P_PAD), F32),
        compiler_params=_cparams(("parallel",)),
        name="inproj",
    )(x, mod_l, g_pre, w_in_b)


def _pool_kernel(v_ref, w_ref, scale_ref, o_ref):
    i = pl.program_id(0)
    n = SEQ_TILE
    seg_mask = jnp.where(i < T_CTX // n, ~(L_CTX - 1), ~(GRID_W - 1))
    seg_len = jnp.where(i < T_CTX // n, L_CTX, GRID_W)
    v = v_ref[...]
    vh = v.astype(BF16)
    vl = (v - vh.astype(F32)).astype(BF16)
    t = _iota((n, n), 0)
    s = _iota((n, n), 1)
    seg0 = t & seg_mask
    seg1 = seg0 + seg_len
    lane_g = _iota((n, GW), 1) // HD
    mean = jnp.zeros((n, GW), F32)
    for gi, win in enumerate(POOL_WINDOWS):
        lo = jnp.maximum(t - win // 2, seg0)
        hi = jnp.minimum(t - win // 2 + win, seg1)
        w01 = jnp.where(s >= lo, jnp.where(s < hi, 1.0, 0.0), 0.0).astype(BF16)
        cnt = (hi - lo).astype(F32)
        m = (_dot(w01, vh) + _dot(w01, vl)) / cnt
        mean = jnp.where(lane_g == gi, m, mean)
    d = (mean - v).astype(BF16)
    o_ref[...] = _dot(d, w_ref[...]) * scale_ref[...]


def _pool(parts, w_bd, scale):
    n = SEQ_TILE
    return pl.pallas_call(
        _pool_kernel,
        grid=(T // n,),
        in_specs=[
            pl.BlockSpec((n, GW), lambda i: (i, CB_POOL)),
            pl.BlockSpec((GW, GW), lambda i: (0, 0)),
            pl.BlockSpec((1, GW), lambda i: (0, 0)),
        ],
        out_specs=pl.BlockSpec((n, GW), lambda i: (i, 0)),
        out_shape=jax.ShapeDtypeStruct((T, GW), F32),
        compiler_params=_cparams(("parallel",)),
        name="pool",
    )(parts, w_bd, scale)


def _softmax_rows(s):
    m = jnp.max(s, axis=-1, keepdims=True)
    p = jnp.exp(s - m)
    return p / jnp.sum(p, axis=-1, keepdims=True)


def _ctx_attn_kernel(q_ref, k_ref, v_ref, o_ref):
    n = L_CTX
    qe = _expand_heads(q_ref[...].astype(BF16), HD)
    s = _dot_nt(qe, k_ref[...].astype(BF16)) * (HD ** -0.5)
    p = _softmax_rows(s).astype(BF16)
    o_ref[...] = _extract_heads(_dot(p, v_ref[...].astype(BF16)), n)


def _ctx_attn(parts):
    n = L_CTX
    return pl.pallas_call(
        _ctx_attn_kernel,
        grid=(B_CTX,),
        in_specs=[
            pl.BlockSpec((n, GW), lambda b: (b, CB_NAQ)),
            pl.BlockSpec((n, GW), lambda b: (b, CB_NAK)),
            pl.BlockSpec((n, GW), lambda b: (b, CB_NAV)),
        ],
        out_specs=pl.BlockSpec((n, GW), lambda b: (b, 0)),
        out_shape=jax.ShapeDtypeStruct((T, GW), F32),
        compiler_params=_cparams(("parallel",)),
        name="ctx_attn",
    )(parts, parts, parts)


NA_ROWS_PER_STEP = 8
NA_WIN = NA_ROWS * GRID_W
NA_DR = 2 * NA_ROWS - 1
NA_DC = 2 * NA_COLS - 1


def _na_bias_table(rpb_ref, e2_ref):
    shape = (GRID_W, 2 * GRID_W)
    qc = _iota(shape, 0)
    lane = _iota(shape, 1)
    kc = lane % GRID_W
    upper = lane >= GRID_W
    c0 = jnp.clip(qc - NA_COLS // 2, 0, GRID_W - NA_COLS)
    dc = jnp.where((kc >= c0) & (kc < c0 + NA_COLS), kc - qc + (NA_COLS - 1), -1)

    def one(ha, carry):
        h = ha // (NA_DR - 1)
        a = ha % (NA_DR - 1)
        acc = jnp.full(shape, NEG, F32)
        for j in range(NA_DC):
            val = jnp.where(upper, rpb_ref[h * NA_DR + a + 1, j], rpb_ref[h * NA_DR + a, j])
            acc = jnp.where(dc == j, val, acc)
        e2_ref[h, a] = acc
        return carry

    lax.fori_loop(0, HEADS * (NA_DR - 1), one, 0)


def _na_kernel(q_ref, k_ref, v_ref, ck_ref, cv_ref, rpb_ref, prev_ref, o_ref, kb_ref, vb_ref, e2_ref):
    del prev_ref
    step = pl.program_id(1)

    @pl.when(step == 0)
    def _():
        kb_ref[...] = k_ref[...].astype(BF16)
        vb_ref[...] = v_ref[...].astype(BF16)
        _na_bias_table(rpb_ref, e2_ref)

    ckb = ck_ref[0, 0].astype(BF16)
    cvb = cv_ref[0, 0].astype(BF16)
    scale = HD ** -0.5

    def one_row(rr):
        r = step * NA_ROWS_PER_STEP + rr
        r0 = jnp.clip(r - NA_ROWS // 2, 0, GRID_H - NA_ROWS)
        base = r0 - r + (NA_ROWS - 1)
        q0 = pl.multiple_of(rr * GRID_W, GRID_W)
        k0 = pl.multiple_of(r0 * GRID_W, GRID_W)
        qe = _expand_heads(q_ref[pl.ds(q0, GRID_W), :].astype(BF16), HD)
        kw = kb_ref[pl.ds(k0, NA_WIN), :]
        vw = vb_ref[pl.ds(k0, NA_WIN), :]
        bias = jnp.concatenate(
            [jnp.concatenate([e2_ref[h, base + 2 * p] for p in range(NA_ROWS // 2)], axis=1)
             for h in range(HEADS)], axis=0)
        yield
        s_loc = _dot_nt(qe, kw) * scale + bias
        s_ctx = _dot_nt(qe, ckb) * scale
        yield
        m = jnp.maximum(jnp.max(s_loc, axis=-1, keepdims=True), jnp.max(s_ctx, axis=-1, keepdims=True))
        p_loc = jnp.exp(s_loc - m)
        p_ctx = jnp.exp(s_ctx - m)
        inv = 1.0 / (jnp.sum(p_loc, axis=-1, keepdims=True) + jnp.sum(p_ctx, axis=-1, keepdims=True))
        yield
        pv = _dot((p_loc * inv).astype(BF16), vw) + _dot((p_ctx * inv).astype(BF16), cvb)
        yield
        o_ref[pl.ds(q0, GRID_W), :] = _extract_heads(pv, GRID_W)

    def row_pair(i, carry):
        _interleave(one_row(2 * i), one_row(2 * i + 1))
        return carry

    lax.fori_loop(0, NA_ROWS_PER_STEP // 2, row_pair, 0)


def _na_latent(parts, ck, cv, rpb, layer, prev):
    rows = NA_ROWS_PER_STEP * GRID_W
    steps = L_LAT // rows
    row_blk0 = T_CTX // rows
    seq_blk0 = T_CTX // L_LAT
    return pl.pallas_call(
        _na_kernel,
        grid=(B_LAT, steps),
        in_specs=[
            pl.BlockSpec((rows, GW), lambda b, s: (row_blk0 + b * steps + s, CB_NAQ)),
            pl.BlockSpec((L_LAT, GW), lambda b, s: (seq_blk0 + b, CB_NAK)),
            pl.BlockSpec((L_LAT, GW), lambda b, s: (seq_blk0 + b, CB_NAV)),
            pl.BlockSpec((1, 1, PAST, GW), lambda b, s: (b, layer, 0, 0)),
            pl.BlockSpec((1, 1, PAST, GW), lambda b, s: (b, layer, 0, 0)),
            pl.BlockSpec(memory_space=pltpu.SMEM),
            pl.BlockSpec(memory_space=pl.ANY),
        ],
        out_specs=pl.BlockSpec((rows, GW), lambda b, s: (row_blk0 + b * steps + s, 0)),
        out_shape=jax.ShapeDtypeStruct((T, GW), F32),
        input_output_aliases={6: 0},
        scratch_shapes=[pltpu.VMEM((L_LAT, GW), BF16), pltpu.VMEM((L_LAT, GW), BF16),
                        pltpu.VMEM((HEADS, NA_DR - 1, GRID_W, 2 * GRID_W), F32)],
        compiler_params=_cparams(("arbitrary", "arbitrary")),
        name="na_latent",
    )(parts, parts, parts, ck, cv, rpb.reshape(HEADS * NA_DR, NA_DC), prev)


def _rope(x, cos, sin_signed):
    lane = _iota(x.shape, 1)
    partner = jnp.where(lane % 32 < 16, pltpu.roll(x, GW - 16, 1), pltpu.roll(x, 16, 1))
    return x * cos + partner * sin_signed


def _ret_kernel(*refs, rope, has_s0, n_tiles, n_prev):
    it = iter(refs)
    qf_ref, kf_ref, vf_ref, qb_ref, kb_ref, vb_ref = (next(it) for _ in range(6))
    if rope:
        cf_ref, sf_ref, cb_ref, sb_ref = (next(it) for _ in range(4))
    lg_ref = next(it)
    if has_s0:
        s0_ref = next(it)
    for _ in range(n_prev):
        next(it)
    of_ref, ob_ref, st_ref, sf_scr, sb_scr = (next(it) for _ in range(5))

    t = pl.program_id(1)
    n = SEQ_TILE
    lane_h = _iota((GW, GW), 1) // HD
    row_h = _iota((GW, GW), 0) // HD
    blockdiag = row_h == lane_h

    @pl.when(t == 0)
    def _():
        if has_s0:
            for d, scr in ((0, sf_scr), (1, sb_scr)):
                s0 = jnp.concatenate([s0_ref[0, d]] * HEADS, axis=1)
                scr[...] = jnp.where(blockdiag, s0, 0.0)
        else:
            sf_scr[...] = jnp.zeros((GW, GW), F32)
            sb_scr[...] = jnp.zeros((GW, GW), F32)

    lg = _log_sigmoid(lg_ref[...])
    ti = _iota((n, GW), 0).astype(F32)
    i_ = _iota((n, n), 0).astype(F32)
    j_ = _iota((n, n), 1).astype(F32)

    def stream(d, q_ref, k_ref, v_ref, c_ref, s_ref, o_ref, scr):
        lgd = lg[d:d + 1, :]
        q = q_ref[...] * (RET_DK ** -0.5)
        k = k_ref[...]
        if rope:
            q = _rope(q, c_ref[...], s_ref[...])
            k = _rope(k, c_ref[...], s_ref[...])
        vb16 = v_ref[...].astype(BF16)
        yield
        diff = (i_ - j_) if d == 0 else (j_ - i_)
        pos = ti if d == 0 else (n - 1.0) - ti
        w = jnp.concatenate(
            [jnp.where(diff >= 0, jnp.exp(jnp.maximum(diff, 0.0) * lgd[:, h * HD:h * HD + 1]), 0.0)
             for h in range(HEADS)], axis=0)
        yield
        a = _dot_nt(_expand_heads(q.astype(BF16), HD), k.astype(BF16)) * w
        yield
        o = _extract_heads(_dot(a.astype(BF16), vb16), n)
        yield
        s_old = scr[...]
        o = o + _dot(q.astype(BF16), s_old.astype(BF16)) * jnp.exp((pos + 1.0) * lgd)
        o_ref[...] = o
        yield
        kd = (k * jnp.exp((n - 1.0 - pos) * lgd)).astype(BF16)
        upd = _dot_tn(kd, vb16)
        scr[...] = s_old * jnp.exp(float(n) * lgd) + jnp.where(blockdiag, upd, 0.0)

    _interleave(
        stream(0, qf_ref, kf_ref, vf_ref, cf_ref if rope else None, sf_ref if rope else None, of_ref, sf_scr),
        stream(1, qb_ref, kb_ref, vb_ref, cb_ref if rope else None, sb_ref if rope else None, ob_ref, sb_scr))

    @pl.when(t == n_tiles - 1)
    def _():
        for d, scr in ((0, sf_scr), (1, sb_scr)):
            s = scr[...]
            st_ref[0, d] = s[:, 0:HD] + s[:, HD:2 * HD] + s[:, 2 * HD:3 * HD] + s[:, 3 * HD:4 * HD]


def _retention(parts, lg_lanes, *, tok0, batch, seq_len, rope_tabs=None, s0=None, prev=()):
    n = SEQ_TILE
    nt = seq_len // n
    blk0 = tok0 // n
    fwd = lambda b, t: blk0 + b * nt + t
    bwd = lambda b, t: blk0 + b * nt + (nt - 1 - t)
    in_specs = [pl.BlockSpec((n, GW), lambda b, t, c=c: (fwd(b, t), c)) for c in (CB_RQ, CB_RK, CB_RV)]
    in_specs += [pl.BlockSpec((n, GW), lambda b, t, c=c: (bwd(b, t), c)) for c in (CB_RQ, CB_RK, CB_RV)]
    args = [parts] * 6
    if rope_tabs is not None:
        in_specs += [pl.BlockSpec((n, GW), lambda b, t: (t, 0))] * 2
        in_specs += [pl.BlockSpec((n, GW), lambda b, t: (nt - 1 - t, 0))] * 2
        args += [rope_tabs[0], rope_tabs[1], rope_tabs[0], rope_tabs[1]]
    in_specs.append(pl.BlockSpec((2, GW), lambda b, t: (0, 0)))
    args.append(lg_lanes)
    if s0 is not None:
        in_specs.append(pl.BlockSpec((1, 2, GW, HD), lambda b, t: (b, 0, 0, 0)))
        args.append(s0)
    aliases = {len(args) + i: i for i in range(len(prev))}
    in_specs += [pl.BlockSpec(memory_space=pl.ANY)] * len(prev)
    args += list(prev)
    kern = functools.partial(_ret_kernel, rope=rope_tabs is not None, has_s0=s0 is not None, n_tiles=nt,
                             n_prev=len(prev))
    return pl.pallas_call(
        kern,
        grid=(batch, nt),
        in_specs=in_specs,
        out_specs=[
            pl.BlockSpec((n, GW), lambda b, t: (fwd(b, t), 0)),
            pl.BlockSpec((n, GW), lambda b, t: (bwd(b, t), 0)),
            pl.BlockSpec((1, 2, GW, HD), lambda b, t: (b, 0, 0, 0)),
        ],
        out_shape=[
            jax.ShapeDtypeStruct((T, GW), F32),
            jax.ShapeDtypeStruct((T, GW), F32),
            jax.ShapeDtypeStruct((batch, 2, GW, HD), F32),
        ],
        input_output_aliases=aliases,
        scratch_shapes=[pltpu.VMEM((GW, GW), F32), pltpu.VMEM((GW, GW), F32)],
        compiler_params=_cparams(("arbitrary", "arbitrary")),
        name="retention",
    )(*args)


GLA_QK = HEADS * GLA_DK
N_SUB = GLA_CHUNK // GLA_SUB
GLA_SAFE_DECAY = 60.0


def _interleave(*gens):
    results = [None] * len(gens)
    live = list(range(len(gens)))
    while live:
        for i in list(live):
            try:
                next(gens[i])
            except StopIteration as stop:
                results[i] = stop.value
                live.remove(i)
    return results


def _gla_tile(q, k, v, la, st, rev):
    n, c = SEQ_TILE, GLA_CHUNK
    n_chunks = n // c
    ri = _iota((n, n), 0)
    ci = _iota((n, n), 1)
    same_chunk = ri // c == ci // c
    causal = (ci >= ri) if rev else (ci <= ri)
    tri = jnp.where(same_chunk, jnp.where(causal, 1.0, 0.0), 0.0).astype(BF16)
    b = _dot_exact01(tri, la)
    yield

    def rows_of(idx, count):
        if idx is None:
            return jnp.zeros((count, GLA_QK), F32)
        return jnp.broadcast_to(b[idx:idx + 1, :], (count, GLA_QK))

    def ref_row(cc, s):
        if rev:
            return cc * c + (s + 1) * GLA_SUB if s < N_SUB - 1 else None
        return cc * c + s * GLA_SUB - 1 if s > 0 else None

    end_rows = [cc * c if rev else cc * c + c - 1 for cc in range(n_chunks)]
    own_ref = jnp.concatenate([rows_of(ref_row(cc, s), GLA_SUB) for cc in range(n_chunks) for s in range(N_SUB)],
                              axis=0)
    b_end = jnp.concatenate([rows_of(r, c) for r in end_rows], axis=0)
    sub = (_iota((n, GLA_QK), 0) // GLA_SUB) % N_SUB
    qh = q * jnp.exp(b - own_ref)
    q_parts, k_parts = [], []
    for s in range(N_SUB):
        kvalid = (sub >= s) if rev else (sub <= s)
        ref_s = jnp.concatenate([rows_of(ref_row(cc, s), c) for cc in range(n_chunks)], axis=0)
        q_parts.append(jnp.where(sub == s, qh, 0.0))
        k_parts.append(jnp.where(kvalid, k * jnp.exp(jnp.minimum(ref_s - b, GLA_SAFE_DECAY)), 0.0))
    q_cat = _expand_heads(jnp.concatenate(q_parts, axis=1).astype(BF16), GLA_DK)
    k_cat = jnp.concatenate(k_parts, axis=1).astype(BF16)
    yield
    a = _dot_nt(q_cat, k_cat)
    yield
    qi = _iota((HEADS * n, n), 0) % n
    kj = _iota((HEADS * n, n), 1)
    keep = (qi // c == kj // c) & ((kj >= qi) if rev else (kj <= qi))
    vb16 = v.astype(BF16)
    o_intra = _extract_heads(_dot(jnp.where(keep, a, 0.0).astype(BF16), vb16), n)
    yield

    qt = (q * jnp.exp(b)).astype(BF16)
    kt = (k * jnp.exp(b_end - b)).astype(BF16)
    blockdiag = _iota((GW, GLA_QK), 0) // HD == _iota((GW, GLA_QK), 1) // GLA_DK
    upd = [jnp.where(blockdiag, _dot_tn(vb16[cc * c:(cc + 1) * c], kt[cc * c:(cc + 1) * c]), 0.0)
           for cc in range(n_chunks)]
    yield
    o_inter = [None] * n_chunks
    for cc in (reversed(range(n_chunks)) if rev else range(n_chunks)):
        o_inter[cc] = _dot_nt(qt[cc * c:(cc + 1) * c], st.astype(BF16))
        st = st * jnp.exp(b[end_rows[cc]:end_rows[cc] + 1, :]) + upd[cc]
        yield
    return o_intra + jnp.concatenate(o_inter, axis=0), st


def _gla_chunk(q, k, v, la, st, rev):
    c = GLA_CHUNK
    ri = _iota((c, c), 0)
    ci = _iota((c, c), 1)
    tri = jnp.where((ci >= ri) if rev else (ci <= ri), 1.0, 0.0).astype(BF16)
    b = _dot_exact01(tri, la)
    b_end = b[0:1, :] if rev else b[c - 1:c, :]
    row = _iota((c, GLA_QK), 0)
    sub = row // GLA_SUB
    off = row % GLA_SUB

    o = _dot_nt((q * jnp.exp(b)).astype(BF16), st.astype(BF16))
    kt = (k * jnp.exp(b_end - b)).astype(BF16)
    lane_h = _iota((GW, GLA_QK), 1) // GLA_DK
    row_h = _iota((GW, GLA_QK), 0) // HD
    st_new = st * jnp.exp(b_end) + jnp.where(row_h == lane_h, _dot_tn(v.astype(BF16), kt), 0.0)

    q_parts, k_parts = [], []
    for s in range(1, N_SUB):
        if rev:
            qsub, brow = N_SUB - 1 - s, b[(N_SUB - s) * GLA_SUB:(N_SUB - s) * GLA_SUB + 1, :]
            kvalid = sub > qsub
        else:
            qsub, brow = s, b[s * GLA_SUB - 1:s * GLA_SUB, :]
            kvalid = sub < qsub
        q_parts.append(jnp.where(sub == qsub, q * jnp.exp(jnp.where(sub == qsub, b - brow, 0.0)), 0.0))
        k_parts.append(jnp.where(kvalid, k * jnp.exp(jnp.where(kvalid, brow - b, 0.0)), 0.0))
    q_cat = _expand_heads(jnp.concatenate(q_parts, axis=1).astype(BF16), GLA_DK)
    k_cat = jnp.concatenate(k_parts, axis=1).astype(BF16)
    a_off = _dot_nt(q_cat, k_cat)
    o = o + _extract_heads(_dot(a_off.astype(BF16), v.astype(BF16)), c)

    red = jnp.where(_iota((GLA_QK, GW), 0) // GLA_DK == _iota((GLA_QK, GW), 1) // HD, 1.0, 0.0).astype(BF16)
    rowv = _iota((c, GW), 0) % GLA_SUB
    for dl in range(GLA_SUB):
        if dl == 0:
            x = q * k
            vs = v
        else:
            sh = dl if not rev else c - dl
            valid = (off + dl < GLA_SUB) if rev else (off >= dl)
            ks = pltpu.roll(k, sh, 0)
            bs = pltpu.roll(b, sh, 0)
            vs = pltpu.roll(v, sh, 0)
            x = jnp.where(valid, q * ks * jnp.exp(jnp.where(valid, b - bs, 0.0)), 0.0)
            validv = (rowv + dl < GLA_SUB) if rev else (rowv >= dl)
            vs = jnp.where(validv, vs, 0.0)
        o = o + _dot(x.astype(BF16), red) * vs
    return o, st_new


def _gla_kernel(*refs, has_s0, n_tiles, n_prev):
    it = iter(refs)
    qkf_ref, vf_ref, lrf_ref, qkb_ref, vb_ref, lrb_ref, gu_ref, gb_ref = (next(it) for _ in range(8))
    if has_s0:
        s0_ref = next(it)
    for _ in range(n_prev):
        next(it)
    of_ref, ob_ref, st_ref, sf_scr, sb_scr = (next(it) for _ in range(5))

    t = pl.program_id(1)
    lane_h = _iota((GW, GLA_QK), 1) // GLA_DK
    row_h = _iota((GW, GLA_QK), 0) // HD
    blockdiag = row_h == lane_h

    @pl.when(t == 0)
    def _():
        if has_s0:
            for d, scr in ((0, sf_scr), (1, sb_scr)):
                s0t = jnp.concatenate([s0_ref[0, d].T] * HEADS, axis=0)
                scr[...] = jnp.where(blockdiag, s0t, 0.0)
        else:
            sf_scr[...] = jnp.zeros((GW, GLA_QK), F32)
            sb_scr[...] = jnp.zeros((GW, GLA_QK), F32)

    c = GLA_CHUNK
    n_chunks = SEQ_TILE // c

    def log_gate(d, lr_ref):
        z = _dot(lr_ref[...].astype(BF16), gu_ref[d].astype(BF16)) + gb_ref[d]
        return _log_sigmoid(z) / GLA_TAU

    def tile_stream(d, qk_ref, v_ref, la, scr):
        return _gla_tile(qk_ref[:, 0:GLA_QK] * (GLA_DK ** -0.5), qk_ref[:, GLA_QK:2 * GLA_QK],
                         v_ref[...], la, scr[...], rev=(d == 1))

    def chunk_stream(d, qk_ref, v_ref, la, o_ref, scr):
        st = scr[...]
        for cc in (range(n_chunks) if d == 0 else reversed(range(n_chunks))):
            rows = slice(cc * c, (cc + 1) * c)
            q = qk_ref[rows, 0:GLA_QK] * (GLA_DK ** -0.5)
            k = qk_ref[rows, GLA_QK:2 * GLA_QK]
            o, st = _gla_chunk(q, k, v_ref[rows, :], la[rows, :], st, rev=(d == 1))
            o_ref[rows, :] = o
        scr[...] = st

    la_f = log_gate(0, lrf_ref)
    la_b = log_gate(1, lrb_ref)
    decay = GLA_SUB * jnp.max(jnp.maximum(-la_f, -la_b))

    @pl.when(decay <= GLA_SAFE_DECAY)
    def _():
        (o_f, st_f), (o_b, st_b) = _interleave(tile_stream(0, qkf_ref, vf_ref, la_f, sf_scr),
                                               tile_stream(1, qkb_ref, vb_ref, la_b, sb_scr))
        of_ref[...] = o_f
        ob_ref[...] = o_b
        sf_scr[...] = st_f
        sb_scr[...] = st_b

    @pl.when(decay > GLA_SAFE_DECAY)
    def _():
        chunk_stream(0, qkf_ref, vf_ref, la_f, of_ref, sf_scr)
        chunk_stream(1, qkb_ref, vb_ref, la_b, ob_ref, sb_scr)

    @pl.when(t == n_tiles - 1)
    def _():
        for d, scr in ((0, sf_scr), (1, sb_scr)):
            s = scr[...].T
            st_ref[0, d] = s[:, 0:HD] + s[:, HD:2 * HD] + s[:, 2 * HD:3 * HD] + s[:, 3 * HD:4 * HD]


def _gla(parts, gate_up_pad, gate_b, *, tok0, batch, seq_len, s0=None, prev=()):
    n = SEQ_TILE
    nt = seq_len // n
    blk0 = tok0 // n
    fwd = lambda b, t: blk0 + b * nt + t
    bwd = lambda b, t: blk0 + b * nt + (nt - 1 - t)
    in_specs, args = [], []
    for m in (fwd, bwd):
        in_specs += [
            pl.BlockSpec((n, GW), lambda b, t, m=m: (m(b, t), CB_AQK)),
            pl.BlockSpec((n, GW), lambda b, t, m=m: (m(b, t), CB_AV)),
            pl.BlockSpec((n, 128), lambda b, t, m=m: (m(b, t), CB_LR128)),
        ]
        args += [parts] * 3
    in_specs += [pl.BlockSpec((2, 128, GLA_QK), lambda b, t: (0, 0, 0)),
                 pl.BlockSpec((2, 1, GLA_QK), lambda b, t: (0, 0, 0))]
    args += [gate_up_pad, gate_b]
    if s0 is not None:
        in_specs.append(pl.BlockSpec((1, 2, GLA_QK, HD), lambda b, t: (b, 0, 0, 0)))
        args.append(s0)
    aliases = {len(args) + i: i for i in range(len(prev))}
    in_specs += [pl.BlockSpec(memory_space=pl.ANY)] * len(prev)
    args += list(prev)
    kern = functools.partial(_gla_kernel, has_s0=s0 is not None, n_tiles=nt, n_prev=len(prev))
    return pl.pallas_call(
        kern,
        grid=(batch, nt),
        in_specs=in_specs,
        out_specs=[
            pl.BlockSpec((n, GW), lambda b, t: (fwd(b, t), 0)),
            pl.BlockSpec((n, GW), lambda b, t: (bwd(b, t), 0)),
            pl.BlockSpec((1, 2, GLA_QK, HD), lambda b, t: (b, 0, 0, 0)),
        ],
        out_shape=[
            jax.ShapeDtypeStruct((T, GW), F32),
            jax.ShapeDtypeStruct((T, GW), F32),
            jax.ShapeDtypeStruct((batch, 2, GLA_QK, HD), F32),
        ],
        input_output_aliases=aliases,
        scratch_shapes=[pltpu.VMEM((GW, GLA_QK), F32), pltpu.VMEM((GW, GLA_QK), F32)],
        compiler_params=_cparams(("arbitrary", "arbitrary")),
        name="gla",
    )(*args)


OUT_TM = 512
FFN_TF = 1408
FFN_PASSES = D_FF // FFN_TF


def _outproj_ffn_kernel(x_ref, mod_ref, gpost_ref, gpre2_ref, gpost2_ref, pool_ref, na_ref, rf_ref, rb_ref,
                        rg_ref, af_ref, ab_ref, ag_ref, ng_ref, w_ref, wg_ref, wu_ref, wd_ref, o_ref):
    avg = jnp.where(_iota((GW, GW), 0) // HD == _iota((GW, GW), 1) // HD, 1.0 / HD, 0.0).astype(BF16)
    r = rf_ref[...] + rb_ref[...]
    r = r - _head_mean(r, avg)
    r = r * lax.rsqrt(_head_mean(r * r, avg) + GN_EPS) * _silu(rg_ref[...])
    a = af_ref[...] + ab_ref[...]
    a = a * lax.rsqrt(_head_mean(a * a, avg) + RMS_EPS) * ng_ref[...] * _silu(ag_ref[...])
    y = _dot(pool_ref[...].astype(BF16), w_ref[0, 0:GW, :])
    y = y + _dot(na_ref[...].astype(BF16), w_ref[0, GW:2 * GW, :])
    y = y + _dot(r.astype(BF16), w_ref[0, 2 * GW:3 * GW, :])
    y = y + _dot(a.astype(BF16), w_ref[0, 3 * GW:4 * GW, :])
    x1 = x_ref[...] + mod_ref[0, 2:3, :] * _rms(y, gpost_ref[...])

    hb = (_rms(x1, gpre2_ref[...]) * (1.0 + mod_ref[0, 4:5, :]) + mod_ref[0, 3:4, :]).astype(BF16)
    y = None
    for j in range(FFN_PASSES):
        act = (_silu(_dot(hb, wg_ref[0, j])) * _dot(hb, wu_ref[0, j])).astype(BF16)
        part = _dot(act, wd_ref[0, j])
        y = part if y is None else y + part
    o_ref[...] = x1 + mod_ref[0, 5:6, :] * _rms(y, gpost2_ref[...])


def _outproj_ffn(x, mod_l, g_post, g_pre2, g_post2, o_pool, o_na, ret_f, ret_b, gla_f, gla_b, parts, ng_lanes,
                 w_out_b, wg, wu, wd, layer):
    tm = OUT_TM
    tile = lambda i: (i, 0)
    act = pl.BlockSpec((tm, GW), tile)
    vec = pl.BlockSpec((1, D), lambda i: (0, 0))
    once = pl.Buffered(1)
    return pl.pallas_call(
        _outproj_ffn_kernel,
        grid=(T // tm,),
        in_specs=[
            pl.BlockSpec((tm, D), tile),
            pl.BlockSpec((1, 6, D), lambda i: (_mod_row(i, tm), 0, 0)),
            vec, vec, vec,
            act, act, act, act,
            pl.BlockSpec((tm, GW), lambda i: (i, CB_RG)),
            act, act,
            pl.BlockSpec((tm, GW), lambda i: (i, CB_AG)),
            pl.BlockSpec((1, GW), lambda i: (0, 0)),
            pl.BlockSpec((1, D, D), lambda i: (layer, 0, 0), pipeline_mode=once),
            pl.BlockSpec((1, FFN_PASSES, D, FFN_TF), lambda i: (layer, 0, 0, 0), pipeline_mode=once),
            pl.BlockSpec((1, FFN_PASSES, D, FFN_TF), lambda i: (layer, 0, 0, 0), pipeline_mode=once),
            pl.BlockSpec((1, FFN_PASSES, FFN_TF, D), lambda i: (layer, 0, 0, 0), pipeline_mode=once),
        ],
        out_specs=pl.BlockSpec((tm, D), tile),
        out_shape=jax.ShapeDtypeStruct((T, D), F32),
        compiler_params=_cparams(("arbitrary",)),
        name="outproj_ffn",
    )(x, mod_l, g_post, g_pre2, g_post2, o_pool, o_na, ret_f, ret_b, parts, gla_f, gla_b, parts, ng_lanes,
      w_out_b, wg, wu, wd)


def _rope_tables():
    nf = 16
    inv = (ROPE_BASE ** (-np.arange(nf, dtype=np.float32) / nf)).astype(np.float32)
    tok = np.arange(L_LAT)
    cos = np.zeros((L_LAT, HD), np.float32)
    sin = np.zeros((L_LAT, HD), np.float32)
    for axis, pos in enumerate((tok // GRID_W, tok % GRID_W)):
        ang = pos.astype(np.float32)[:, None] * inv[None, :]
        c, s = np.cos(ang), np.sin(ang)
        cos[:, axis * 32:axis * 32 + 32] = np.concatenate([c, c], axis=1)
        sin[:, axis * 32:axis * 32 + 32] = np.concatenate([-s, s], axis=1)
    return jnp.asarray(np.tile(cos, (1, HEADS))), jnp.asarray(np.tile(sin, (1, HEADS)))


def _block_diag(w):
    g, c, _ = w.shape
    out = jnp.zeros((g * c, g * c), w.dtype)
    for i in range(g):
        out = out.at[i * c:(i + 1) * c, i * c:(i + 1) * c].set(w[i])
    return out


def kernel(x_prompt, x_sample, cache_na_k, cache_na_v, state_ret, state_gla, c, c_ctx, w_mod, b_mod,
           g_pre_mix, g_post_mix, g_pre_ffn, g_post_ffn, w_in, w_out, pool_w, pool_scale, na_rpb,
           ret_decay_logit, gla_gate_up, gla_gate_b, gla_norm_g, w_ffn_gate, w_ffn_up, w_ffn_down):
    x = jnp.concatenate([x_prompt.reshape(T_CTX, D), x_sample.reshape(T_LAT, D)], axis=0)
    cv8 = jnp.concatenate([c_ctx[None, :], c, jnp.zeros((8 - 1 - B_LAT, D), F32)], axis=0)
    mods = _modulation(cv8, w_mod, b_mod).reshape(DEPTH, 8, 6, D)

    w_in_b = jnp.pad(w_in, ((0, 0), (0, 0), (0, P_PAD - P_IN))).astype(BF16)
    w_out_b = w_out.astype(BF16)
    split_cols = lambda w: w.astype(BF16).reshape(DEPTH, D, FFN_PASSES, FFN_TF).transpose(0, 2, 1, 3)
    wg_b, wu_b = split_cols(w_ffn_gate), split_cols(w_ffn_up)
    wd_b = w_ffn_down.astype(BF16).reshape(DEPTH, FFN_PASSES, FFN_TF, D)
    gate_up_pad = jnp.pad(gla_gate_up, ((0, 0), (0, 0), (0, 128 - GLA_LOWRANK), (0, 0)))
    rope_tabs = _rope_tables()
    ck = cache_na_k.reshape(B_LAT, DEPTH, PAST, GW)
    cv = cache_na_v.reshape(B_LAT, DEPTH, PAST, GW)
    s0_ret = state_ret.reshape(B_LAT, DEPTH, 2, GW, HD)
    s0_gla = state_gla.reshape(B_LAT, DEPTH, 2, GLA_QK, HD)

    ks, vs, srs, sgs = [], [], [], []
    for l in range(DEPTH):
        mod_l = mods[l]
        parts = _inproj(x, mod_l, g_pre_mix[l][None, :], w_in_b, l)
        o_pool = _pool(parts, _block_diag(pool_w[l]).astype(BF16), pool_scale[l][None, :])
        o_na = _na_latent(parts, ck, cv, na_rpb[l], l, _ctx_attn(parts))
        lg_lanes = jnp.repeat(ret_decay_logit[l], HD, axis=1)
        rf, rb, s_ret = _retention(parts, lg_lanes, tok0=0, batch=B_CTX, seq_len=L_CTX)
        rf, rb, _ = _retention(parts, lg_lanes, tok0=T_CTX, batch=B_LAT, seq_len=L_LAT,
                               rope_tabs=rope_tabs, s0=s0_ret[:, l], prev=(rf, rb))
        gb = gla_gate_b[l][:, None, :]
        gf, gbw, s_gla = _gla(parts, gate_up_pad[l], gb, tok0=0, batch=B_CTX, seq_len=L_CTX)
        gf, gbw, _ = _gla(parts, gate_up_pad[l], gb, tok0=T_CTX, batch=B_LAT, seq_len=L_LAT,
                          s0=s0_gla[:, l], prev=(gf, gbw))
        x = _outproj_ffn(x, mod_l, g_post_mix[l][None, :], g_pre_ffn[l][None, :], g_post_ffn[l][None, :],
                         o_pool, o_na, rf, rb, gf, gbw, parts, jnp.tile(gla_norm_g[l], HEADS)[None, :],
                         w_out_b, wg_b, wu_b, wd_b, l)
        ks.append(parts[:T_CTX, CB_NAK * GW:(CB_NAK + 1) * GW].reshape(B_CTX, L_CTX, HEADS, HD))
        vs.append(parts[:T_CTX, CB_NAV * GW:(CB_NAV + 1) * GW].reshape(B_CTX, L_CTX, HEADS, HD))
        srs.append(s_ret.reshape(B_CTX, 2, HEADS, RET_DK, HD))
        sgs.append(s_gla.reshape(B_CTX, 2, HEADS, GLA_DK, HD))

    return (x[:T_CTX].reshape(B_CTX, L_CTX, D), x[T_CTX:].reshape(B_LAT, L_LAT, D),
            jnp.stack(ks, axis=1), jnp.stack(vs, axis=1), jnp.stack(srs, axis=1), jnp.stack(sgs, axis=1))
```

```python
import functools

import numpy as np
import jax
import jax.numpy as jnp
from jax import lax
from jax.experimental import pallas as pl
from jax.experimental.pallas import tpu as pltpu

F32 = jnp.float32
BF16 = jnp.bfloat16

D = 1024
B_CTX, L_CTX = 32, 256
B_LAT, L_LAT = 2, 4096
DEPTH = 4
PAST = 256
GRID_W = 64
GRID_H = L_LAT // GRID_W
T_CTX = B_CTX * L_CTX
T_LAT = B_LAT * L_LAT
T = T_CTX + T_LAT
GW = 256
HEADS = 4
HD = 64
POOL_WINDOWS = (2, 4, 8, 16)
NA_ROWS, NA_COLS = 8, 16
RET_DK = 64
GLA_DK = 32
GLA_LOWRANK = 16
GLA_TAU = 16.0
D_FF = 2816
P_IN = 2832
P_PAD = 2944
ROPE_BASE = 10000.0
RMS_EPS = 1e-6
GN_EPS = 1e-5
NEG = -1e30

CB_POOL, CB_NAQ, CB_NAK, CB_NAV, CB_RQ, CB_RK, CB_RV, CB_RG, CB_AQK, CB_AV, CB_AG = range(11)
CB_LR128 = P_IN // 128

SEQ_TILE = L_CTX
SEQ_STEPS = T // SEQ_TILE
LAT_STEP0 = T_CTX // SEQ_TILE
LAT_TILES = L_LAT // SEQ_TILE
GLA_CHUNK = 64
GLA_SUB = 16
VMEM_LIMIT = 56 * 1024 * 1024


def _cparams(sem):
    return pltpu.CompilerParams(dimension_semantics=sem, vmem_limit_bytes=VMEM_LIMIT)


def _silu(x):
    return x / (1.0 + jnp.exp(-x))


def _log_sigmoid(z):
    return jnp.minimum(z, 0.0) - jnp.log1p(jnp.exp(-jnp.abs(z)))


def _rms(x, g):
    return x * lax.rsqrt(jnp.mean(x * x, axis=-1, keepdims=True) + RMS_EPS) * g


def _dot(a, b):
    return jnp.dot(a, b, preferred_element_type=F32)


def _dot_nt(a, b):
    return lax.dot_general(a, b, (((1,), (1,)), ((), ())), preferred_element_type=F32)


def _dot_tn(a, b):
    return lax.dot_general(a, b, (((0,), (0,)), ((), ())), preferred_element_type=F32)


def _split_hi_lo(x):
    hi = x.astype(BF16)
    return hi, (x - hi.astype(F32)).astype(BF16)


def _dot_exact01(a01, x):
    hi, lo = _split_hi_lo(x)
    return _dot(a01, hi) + _dot(a01, lo)


def _iota(shape, dim):
    return lax.broadcasted_iota(jnp.int32, shape, dim)


def _expand_heads(x, head_w):
    n, w = x.shape
    xe = jnp.concatenate([x] * HEADS, axis=0)
    rowh = _iota((HEADS * n, w), 0) // n
    laneh = (_iota((HEADS * n, w), 1) // head_w) % HEADS
    return jnp.where(rowh == laneh, xe, jnp.zeros_like(xe))


def _extract_heads(p, n):
    laneh = _iota((n, GW), 1) // HD
    out = p[0:n]
    for h in range(1, HEADS):
        out = jnp.where(laneh == h, p[h * n:(h + 1) * n], out)
    return out


def _head_mean(x, avg):
    hi, lo = _split_hi_lo(x)
    return _dot(hi, avg) + _dot(lo, avg)


def _interleave(*gens):
    results = [None] * len(gens)
    live = list(range(len(gens)))
    while live:
        for i in list(live):
            try:
                next(gens[i])
            except StopIteration as stop:
                results[i] = stop.value
                live.remove(i)
    return results


def _mod_row(i, tm):
    return jnp.where(i < T_CTX // tm, 0, 1 + (i * tm - T_CTX) // L_LAT)


def _mod_kernel(cv_ref, w_ref, b_ref, o_ref):
    s = _silu(cv_ref[...]).astype(BF16)
    o_ref[0] = _dot(s, w_ref[0].astype(BF16)) + b_ref[0]


def _modulation(cv8, w_mod, b_mod):
    tn = 1536
    return pl.pallas_call(
        _mod_kernel,
        grid=(DEPTH, 6 * D // tn),
        in_specs=[
            pl.BlockSpec((8, D), lambda l, j: (0, 0)),
            pl.BlockSpec((1, D, tn), lambda l, j: (l, 0, j)),
            pl.BlockSpec((1, 1, tn), lambda l, j: (l, 0, j)),
        ],
        out_specs=pl.BlockSpec((1, 8, tn), lambda l, j: (l, 0, j)),
        out_shape=jax.ShapeDtypeStruct((DEPTH, 8, 6 * D), F32),
        compiler_params=_cparams(("arbitrary", "arbitrary")),
        name="modulation",
    )(cv8, w_mod, b_mod.reshape(DEPTH, 1, 6 * D))


IN_TM = 512


def _x_specs(x, tm):
    if not isinstance(x, tuple):
        return [pl.BlockSpec((tm, D), lambda i: (i, 0))], [x]
    nc = T_CTX // tm
    return ([pl.BlockSpec((tm, D), lambda i: (jnp.minimum(i, nc - 1), 0)),
             pl.BlockSpec((tm, D), lambda i: (jnp.maximum(i - nc, 0), 0))], list(x))


def _x_tile(x_refs, tm):
    if len(x_refs) == 1:
        return x_refs[0][...]
    return jnp.where(pl.program_id(0) < T_CTX // tm, x_refs[0][...], x_refs[1][...])


def _inproj_kernel(*refs, n_x):
    x_refs, (mod_ref, g_ref, w_ref, o_ref) = refs[:n_x], refs[n_x:]
    h = _rms(_x_tile(x_refs, IN_TM), g_ref[...]) * (1.0 + mod_ref[0, 1:2, :]) + mod_ref[0, 0:1, :]
    hb = h.astype(BF16)
    for a in range(0, P_PAD, 1024):
        b = min(a + 1024, P_PAD)
        o_ref[:, a:b] = _dot(hb, w_ref[0, :, a:b])


def _inproj(x, mod_l, g_pre, w_in_b, layer):
    tm = IN_TM
    x_specs, x_args = _x_specs(x, tm)
    return pl.pallas_call(
        functools.partial(_inproj_kernel, n_x=len(x_args)),
        grid=(T // tm,),
        in_specs=x_specs + [
            pl.BlockSpec((1, 6, D), lambda i: (_mod_row(i, tm), 0, 0)),
            pl.BlockSpec((1, D), lambda i: (0, 0)),
            pl.BlockSpec((1, D, P_PAD), lambda i: (layer, 0, 0), pipeline_mode=pl.Buffered(1)),
        ],
        out_specs=pl.BlockSpec((tm, P_PAD), lambda i: (i, 0)),
        out_shape=jax.ShapeDtypeStruct((T, P_PAD), F32),
        compiler_params=_cparams(("arbitrary",)),
        name="inproj",
    )(*x_args, mod_l, g_pre, w_in_b)


def _pool_kernel(v_ref, w_ref, scale_ref, o_ref):
    i = pl.program_id(0)
    n = SEQ_TILE
    seg_mask = jnp.where(i < T_CTX // n, ~(L_CTX - 1), ~(GRID_W - 1))
    seg_len = jnp.where(i < T_CTX // n, L_CTX, GRID_W)
    v = v_ref[...]
    vh = v.astype(BF16)
    vl = (v - vh.astype(F32)).astype(BF16)
    t = _iota((n, n), 0)
    s = _iota((n, n), 1)
    seg0 = t & seg_mask
    seg1 = seg0 + seg_len
    lane_g = _iota((n, GW), 1) // HD
    mean = jnp.zeros((n, GW), F32)
    for gi, win in enumerate(POOL_WINDOWS):
        lo = jnp.maximum(t - win // 2, seg0)
        hi = jnp.minimum(t - win // 2 + win, seg1)
        w01 = jnp.where(s >= lo, jnp.where(s < hi, 1.0, 0.0), 0.0).astype(BF16)
        cnt = (hi - lo).astype(F32)
        m = (_dot(w01, vh) + _dot(w01, vl)) / cnt
        mean = jnp.where(lane_g == gi, m, mean)
    d = (mean - v).astype(BF16)
    o_ref[...] = _dot(d, w_ref[...]) * scale_ref[...]


def _pool(parts, w_bd, scale):
    n = SEQ_TILE
    return pl.pallas_call(
        _pool_kernel,
        grid=(T // n,),
        in_specs=[
            pl.BlockSpec((n, GW), lambda i: (i, CB_POOL)),
            pl.BlockSpec((GW, GW), lambda i: (0, 0)),
            pl.BlockSpec((1, GW), lambda i: (0, 0)),
        ],
        out_specs=pl.BlockSpec((n, GW), lambda i: (i, 0)),
        out_shape=jax.ShapeDtypeStruct((T, GW), F32),
        compiler_params=_cparams(("parallel",)),
        name="pool",
    )(parts, w_bd, scale)


def _softmax_rows(s):
    m = jnp.max(s, axis=-1, keepdims=True)
    p = jnp.exp(s - m)
    return p / jnp.sum(p, axis=-1, keepdims=True)


def _ctx_attn_rows(q_ref, k_ref, v_ref, o_ref, rows):
    qe = _expand_heads(q_ref[rows, :].astype(BF16), HD)
    yield
    s = _dot_nt(qe, k_ref[rows, :].astype(BF16)) * (HD ** -0.5)
    yield
    p = _softmax_rows(s).astype(BF16)
    yield
    o_ref[rows, :] = _extract_heads(_dot(p, v_ref[rows, :].astype(BF16)), L_CTX)


NA_ROWS_PER_STEP = 8
ATT_ROWS = NA_ROWS_PER_STEP * GRID_W
ATT_CTX_STEPS = T_CTX // ATT_ROWS
ATT_LAT_STEPS = L_LAT // ATT_ROWS
NA_WIN = NA_ROWS * GRID_W
NA_DR = 2 * NA_ROWS - 1
NA_DC = 2 * NA_COLS - 1


def _na_bias_table(rpb_ref, e2_ref):
    shape = (GRID_W, 2 * GRID_W)
    qc = _iota(shape, 0)
    lane = _iota(shape, 1)
    kc = lane % GRID_W
    upper = lane >= GRID_W
    c0 = jnp.clip(qc - NA_COLS // 2, 0, GRID_W - NA_COLS)
    dc = jnp.where((kc >= c0) & (kc < c0 + NA_COLS), kc - qc + (NA_COLS - 1), -1)

    def one(ha, carry):
        h = ha // (NA_DR - 1)
        a = ha % (NA_DR - 1)
        acc = jnp.full(shape, NEG, F32)
        for j in range(NA_DC):
            val = jnp.where(upper, rpb_ref[h * NA_DR + a + 1, j], rpb_ref[h * NA_DR + a, j])
            acc = jnp.where(dc == j, val, acc)
        e2_ref[h, a] = acc
        return carry

    lax.fori_loop(0, HEADS * (NA_DR - 1), one, 0)


def _attn_kernel(q_ref, k_ref, v_ref, kseq_ref, vseq_ref, ck_ref, cv_ref, rpb_ref, o_ref, kb_ref, vb_ref, e2_ref):
    s = pl.program_id(0)

    @pl.when(s < ATT_CTX_STEPS)
    def _():
        _interleave(*[_ctx_attn_rows(q_ref, k_ref, v_ref, o_ref, slice(i * L_CTX, (i + 1) * L_CTX))
                      for i in range(ATT_ROWS // L_CTX)])

    @pl.when(s >= ATT_CTX_STEPS)
    def _():
        step = (s - ATT_CTX_STEPS) % ATT_LAT_STEPS

        @pl.when(s == ATT_CTX_STEPS)
        def _():
            _na_bias_table(rpb_ref, e2_ref)

        @pl.when(step == 0)
        def _():
            kb_ref[...] = kseq_ref[...].astype(BF16)
            vb_ref[...] = vseq_ref[...].astype(BF16)

        _na_rows(step, q_ref, ck_ref, cv_ref, o_ref, kb_ref, vb_ref, e2_ref)


def _na_rows(step, q_ref, ck_ref, cv_ref, o_ref, kb_ref, vb_ref, e2_ref):
    ckb = ck_ref[0, 0].astype(BF16)
    cvb = cv_ref[0, 0].astype(BF16)
    scale = HD ** -0.5

    def one_row(rr):
        r = step * NA_ROWS_PER_STEP + rr
        r0 = jnp.clip(r - NA_ROWS // 2, 0, GRID_H - NA_ROWS)
        base = r0 - r + (NA_ROWS - 1)
        q0 = pl.multiple_of(rr * GRID_W, GRID_W)
        k0 = pl.multiple_of(r0 * GRID_W, GRID_W)
        qe = _expand_heads(q_ref[pl.ds(q0, GRID_W), :].astype(BF16), HD)
        kw = kb_ref[pl.ds(k0, NA_WIN), :]
        vw = vb_ref[pl.ds(k0, NA_WIN), :]
        bias = jnp.concatenate(
            [jnp.concatenate([e2_ref[h, base + 2 * p] for p in range(NA_ROWS // 2)], axis=1)
             for h in range(HEADS)], axis=0)
        yield
        s_loc = _dot_nt(qe, kw) * scale + bias
        s_ctx = _dot_nt(qe, ckb) * scale
        yield
        m = jnp.maximum(jnp.max(s_loc, axis=-1, keepdims=True), jnp.max(s_ctx, axis=-1, keepdims=True))
        p_loc = jnp.exp(s_loc - m)
        p_ctx = jnp.exp(s_ctx - m)
        inv = 1.0 / (jnp.sum(p_loc, axis=-1, keepdims=True) + jnp.sum(p_ctx, axis=-1, keepdims=True))
        yield
        pv = _dot((p_loc * inv).astype(BF16), vw) + _dot((p_ctx * inv).astype(BF16), cvb)
        yield
        o_ref[pl.ds(q0, GRID_W), :] = _extract_heads(pv, GRID_W)

    def row_pair(i, carry):
        _interleave(one_row(2 * i), one_row(2 * i + 1))
        return carry

    lax.fori_loop(0, NA_ROWS_PER_STEP // 2, row_pair, 0)


def _attention(parts, ck, cv, rpb, layer):
    lat_req = lambda s: jnp.maximum(s - ATT_CTX_STEPS, 0) // ATT_LAT_STEPS
    seq_blk0 = T_CTX // L_LAT
    return pl.pallas_call(
        _attn_kernel,
        grid=(T // ATT_ROWS,),
        in_specs=[
            pl.BlockSpec((ATT_ROWS, GW), lambda s: (s, CB_NAQ)),
            pl.BlockSpec((ATT_ROWS, GW), lambda s: (s, CB_NAK)),
            pl.BlockSpec((ATT_ROWS, GW), lambda s: (s, CB_NAV)),
            pl.BlockSpec((L_LAT, GW), lambda s: (seq_blk0 + lat_req(s), CB_NAK)),
            pl.BlockSpec((L_LAT, GW), lambda s: (seq_blk0 + lat_req(s), CB_NAV)),
            pl.BlockSpec((1, 1, PAST, GW), lambda s: (lat_req(s), layer, 0, 0)),
            pl.BlockSpec((1, 1, PAST, GW), lambda s: (lat_req(s), layer, 0, 0)),
            pl.BlockSpec(memory_space=pltpu.SMEM),
        ],
        out_specs=pl.BlockSpec((ATT_ROWS, GW), lambda s: (s, 0)),
        out_shape=jax.ShapeDtypeStruct((T, GW), F32),
        scratch_shapes=[pltpu.VMEM((L_LAT, GW), BF16), pltpu.VMEM((L_LAT, GW), BF16),
                        pltpu.VMEM((HEADS, NA_DR - 1, GRID_W, 2 * GRID_W), F32)],
        compiler_params=_cparams(("arbitrary",)),
        name="attention",
    )(parts, parts, parts, parts, parts, ck, cv, rpb.reshape(HEADS * NA_DR, NA_DC))


def _rope(x, cos, sin_signed):
    lane = _iota(x.shape, 1)
    partner = jnp.where(lane % 32 < 16, pltpu.roll(x, GW - 16, 1), pltpu.roll(x, 16, 1))
    return x * cos + partner * sin_signed


def _seq_step(s):
    is_lat = s >= LAT_STEP0
    u = jnp.maximum(s - LAT_STEP0, 0)
    b = u // LAT_TILES
    t = u % LAT_TILES
    bwd = jnp.where(is_lat, LAT_STEP0 + b * LAT_TILES + (LAT_TILES - 1 - t), s)
    return is_lat, b, t, s, bwd


def _ret_kernel(qf_ref, kf_ref, vf_ref, qb_ref, kb_ref, vb_ref, cf_ref, sf_ref, cb_ref, sb_ref, lg_ref, s0_ref,
                of_ref, ob_ref, st_ref, sf_scr, sb_scr, w_scr, dq_scr, dk_scr):
    s = pl.program_id(0)
    is_lat, _, t, _, _ = _seq_step(s)
    n = SEQ_TILE
    blockdiag = _iota((GW, GW), 0) // HD == _iota((GW, GW), 1) // HD
    lg = _log_sigmoid(lg_ref[...])

    @pl.when(s == 0)
    def _():
        ti = _iota((n, GW), 0).astype(F32)
        i_ = _iota((n, n), 0).astype(F32)
        j_ = _iota((n, n), 1).astype(F32)
        for d in range(2):
            lgd = lg[d:d + 1, :]
            diff = (i_ - j_) if d == 0 else (j_ - i_)
            pos = ti if d == 0 else (n - 1.0) - ti
            w_scr[d] = jnp.concatenate(
                [jnp.where(diff >= 0, jnp.exp(jnp.maximum(diff, 0.0) * lgd[:, h * HD:h * HD + 1]), 0.0)
                 for h in range(HEADS)], axis=0)
            dq_scr[d] = jnp.exp((pos + 1.0) * lgd)
            dk_scr[d] = jnp.exp((n - 1.0 - pos) * lgd)

    @pl.when(jnp.logical_or(jnp.logical_not(is_lat), t == 0))
    def _():
        for d, scr in ((0, sf_scr), (1, sb_scr)):
            s0 = jnp.concatenate([s0_ref[0, d]] * HEADS, axis=1)
            scr[...] = jnp.where(blockdiag, s0, 0.0) * jnp.where(is_lat, 1.0, 0.0)

    def stream(d, q_ref, k_ref, v_ref, c_ref, s_ref, o_ref, scr):
        cos = jnp.where(is_lat, c_ref[...], 1.0)
        sin = jnp.where(is_lat, s_ref[...], 0.0)
        q = _rope(q_ref[...] * (RET_DK ** -0.5), cos, sin)
        k = _rope(k_ref[...], cos, sin)
        vb16 = v_ref[...].astype(BF16)
        yield
        a = _dot_nt(_expand_heads(q.astype(BF16), HD), k.astype(BF16)) * w_scr[d]
        yield
        o = _extract_heads(_dot(a.astype(BF16), vb16), n)
        yield
        s_old = scr[...]
        o_ref[...] = o + _dot(q.astype(BF16), s_old.astype(BF16)) * dq_scr[d]
        yield
        upd = _dot_tn((k * dk_scr[d]).astype(BF16), vb16)
        scr[...] = s_old * jnp.exp(float(n) * lg[d:d + 1, :]) + jnp.where(blockdiag, upd, 0.0)

    _interleave(stream(0, qf_ref, kf_ref, vf_ref, cf_ref, sf_ref, of_ref, sf_scr),
                stream(1, qb_ref, kb_ref, vb_ref, cb_ref, sb_ref, ob_ref, sb_scr))

    @pl.when(jnp.logical_or(jnp.logical_not(is_lat), t == LAT_TILES - 1))
    def _():
        for d, scr in ((0, sf_scr), (1, sb_scr)):
            st = scr[...]
            st_ref[0, d] = st[:, 0:HD] + st[:, HD:2 * HD] + st[:, 2 * HD:3 * HD] + st[:, 3 * HD:4 * HD]


def _retention(parts, lg_lanes, rope_tabs, s0):
    n = SEQ_TILE
    fwd = lambda s: _seq_step(s)[3]
    bwd = lambda s: _seq_step(s)[4]
    in_specs = [pl.BlockSpec((n, GW), lambda s, c=c: (fwd(s), c)) for c in (CB_RQ, CB_RK, CB_RV)]
    in_specs += [pl.BlockSpec((n, GW), lambda s, c=c: (bwd(s), c)) for c in (CB_RQ, CB_RK, CB_RV)]
    in_specs += [pl.BlockSpec((n, GW), lambda s: (_seq_step(s)[2], 0))] * 2
    in_specs += [pl.BlockSpec((n, GW), lambda s: (LAT_TILES - 1 - _seq_step(s)[2], 0))] * 2
    in_specs += [pl.BlockSpec((2, GW), lambda s: (0, 0)),
                 pl.BlockSpec((1, 2, GW, HD), lambda s: (_seq_step(s)[1], 0, 0, 0))]
    return pl.pallas_call(
        _ret_kernel,
        grid=(SEQ_STEPS,),
        in_specs=in_specs,
        out_specs=[
            pl.BlockSpec((n, GW), lambda s: (fwd(s), 0)),
            pl.BlockSpec((n, GW), lambda s: (bwd(s), 0)),
            pl.BlockSpec((1, 2, GW, HD), lambda s: (jnp.minimum(s, B_CTX), 0, 0, 0)),
        ],
        out_shape=[
            jax.ShapeDtypeStruct((T, GW), F32),
            jax.ShapeDtypeStruct((T, GW), F32),
            jax.ShapeDtypeStruct((B_CTX + 1, 2, GW, HD), F32),
        ],
        scratch_shapes=[pltpu.VMEM((GW, GW), F32), pltpu.VMEM((GW, GW), F32),
                        pltpu.VMEM((2, HEADS * n, n), F32), pltpu.VMEM((2, n, GW), F32),
                        pltpu.VMEM((2, n, GW), F32)],
        compiler_params=_cparams(("arbitrary",)),
        name="retention",
    )(parts, parts, parts, parts, parts, parts, rope_tabs[0], rope_tabs[1], rope_tabs[0], rope_tabs[1],
      lg_lanes, s0)


GLA_QK = HEADS * GLA_DK
N_SUB = GLA_CHUNK // GLA_SUB
GLA_SAFE_DECAY = 60.0


def _gla_tile(q, k, v, la, st, rev):
    n, c = SEQ_TILE, GLA_CHUNK
    n_chunks = n // c
    ri = _iota((n, n), 0)
    ci = _iota((n, n), 1)
    same_chunk = ri // c == ci // c
    causal = (ci >= ri) if rev else (ci <= ri)
    tri = jnp.where(same_chunk, jnp.where(causal, 1.0, 0.0), 0.0).astype(BF16)
    b = _dot_exact01(tri, la)
    yield

    def rows_of(idx, count):
        if idx is None:
            return jnp.zeros((count, GLA_QK), F32)
        return jnp.broadcast_to(b[idx:idx + 1, :], (count, GLA_QK))

    def ref_row(cc, s):
        if rev:
            return cc * c + (s + 1) * GLA_SUB if s < N_SUB - 1 else None
        return cc * c + s * GLA_SUB - 1 if s > 0 else None

    end_rows = [cc * c if rev else cc * c + c - 1 for cc in range(n_chunks)]
    own_ref = jnp.concatenate([rows_of(ref_row(cc, s), GLA_SUB) for cc in range(n_chunks) for s in range(N_SUB)],
                              axis=0)
    b_end = jnp.concatenate([rows_of(r, c) for r in end_rows], axis=0)
    sub = (_iota((n, GLA_QK), 0) // GLA_SUB) % N_SUB
    qh = q * jnp.exp(b - own_ref)
    q_parts, k_parts = [], []
    for s in range(N_SUB):
        kvalid = (sub >= s) if rev else (sub <= s)
        ref_s = jnp.concatenate([rows_of(ref_row(cc, s), c) for cc in range(n_chunks)], axis=0)
        q_parts.append(jnp.where(sub == s, qh, 0.0))
        k_parts.append(jnp.where(kvalid, k * jnp.exp(jnp.minimum(ref_s - b, GLA_SAFE_DECAY)), 0.0))
    q_cat = _expand_heads(jnp.concatenate(q_parts, axis=1).astype(BF16), GLA_DK)
    k_cat = jnp.concatenate(k_parts, axis=1).astype(BF16)
    yield
    a = _dot_nt(q_cat, k_cat)
    yield
    qi = _iota((HEADS * n, n), 0) % n
    kj = _iota((HEADS * n, n), 1)
    keep = (qi // c == kj // c) & ((kj >= qi) if rev else (kj <= qi))
    vb16 = v.astype(BF16)
    o_intra = _extract_heads(_dot(jnp.where(keep, a, 0.0).astype(BF16), vb16), n)
    yield

    qt = (q * jnp.exp(b)).astype(BF16)
    kt = (k * jnp.exp(b_end - b)).astype(BF16)
    blockdiag = _iota((GW, GLA_QK), 0) // HD == _iota((GW, GLA_QK), 1) // GLA_DK
    upd = [jnp.where(blockdiag, _dot_tn(vb16[cc * c:(cc + 1) * c], kt[cc * c:(cc + 1) * c]), 0.0)
           for cc in range(n_chunks)]
    yield
    o_inter = [None] * n_chunks
    for cc in (reversed(range(n_chunks)) if rev else range(n_chunks)):
        o_inter[cc] = _dot_nt(qt[cc * c:(cc + 1) * c], st.astype(BF16))
        st = st * jnp.exp(b[end_rows[cc]:end_rows[cc] + 1, :]) + upd[cc]
        yield
    return o_intra + jnp.concatenate(o_inter, axis=0), st


def _gla_chunk(q, k, v, la, st, rev):
    c = GLA_CHUNK
    ri = _iota((c, c), 0)
    ci = _iota((c, c), 1)
    tri = jnp.where((ci >= ri) if rev else (ci <= ri), 1.0, 0.0).astype(BF16)
    b = _dot_exact01(tri, la)
    b_end = b[0:1, :] if rev else b[c - 1:c, :]
    row = _iota((c, GLA_QK), 0)
    sub = row // GLA_SUB
    off = row % GLA_SUB

    o = _dot_nt((q * jnp.exp(b)).astype(BF16), st.astype(BF16))
    kt = (k * jnp.exp(b_end - b)).astype(BF16)
    lane_h = _iota((GW, GLA_QK), 1) // GLA_DK
    row_h = _iota((GW, GLA_QK), 0) // HD
    st_new = st * jnp.exp(b_end) + jnp.where(row_h == lane_h, _dot_tn(v.astype(BF16), kt), 0.0)

    q_parts, k_parts = [], []
    for s in range(1, N_SUB):
        if rev:
            qsub, brow = N_SUB - 1 - s, b[(N_SUB - s) * GLA_SUB:(N_SUB - s) * GLA_SUB + 1, :]
            kvalid = sub > qsub
        else:
            qsub, brow = s, b[s * GLA_SUB - 1:s * GLA_SUB, :]
            kvalid = sub < qsub
        q_parts.append(jnp.where(sub == qsub, q * jnp.exp(jnp.where(sub == qsub, b - brow, 0.0)), 0.0))
        k_parts.append(jnp.where(kvalid, k * jnp.exp(jnp.where(kvalid, brow - b, 0.0)), 0.0))
    q_cat = _expand_heads(jnp.concatenate(q_parts, axis=1).astype(BF16), GLA_DK)
    k_cat = jnp.concatenate(k_parts, axis=1).astype(BF16)
    a_off = _dot_nt(q_cat, k_cat)
    o = o + _extract_heads(_dot(a_off.astype(BF16), v.astype(BF16)), c)

    red = jnp.where(_iota((GLA_QK, GW), 0) // GLA_DK == _iota((GLA_QK, GW), 1) // HD, 1.0, 0.0).astype(BF16)
    rowv = _iota((c, GW), 0) % GLA_SUB
    for dl in range(GLA_SUB):
        if dl == 0:
            x = q * k
            vs = v
        else:
            sh = dl if not rev else c - dl
            valid = (off + dl < GLA_SUB) if rev else (off >= dl)
            ks = pltpu.roll(k, sh, 0)
            bs = pltpu.roll(b, sh, 0)
            vs = pltpu.roll(v, sh, 0)
            x = jnp.where(valid, q * ks * jnp.exp(jnp.where(valid, b - bs, 0.0)), 0.0)
            validv = (rowv + dl < GLA_SUB) if rev else (rowv >= dl)
            vs = jnp.where(validv, vs, 0.0)
        o = o + _dot(x.astype(BF16), red) * vs
    return o, st_new


def _gla_kernel(qkf_ref, vf_ref, lrf_ref, qkb_ref, vb_ref, lrb_ref, gu_ref, gb_ref, s0_ref,
                of_ref, ob_ref, st_ref, sf_scr, sb_scr):
    is_lat, _, t, _, _ = _seq_step(pl.program_id(0))
    lane_h = _iota((GW, GLA_QK), 1) // GLA_DK
    row_h = _iota((GW, GLA_QK), 0) // HD
    blockdiag = row_h == lane_h

    @pl.when(jnp.logical_or(jnp.logical_not(is_lat), t == 0))
    def _():
        for d, scr in ((0, sf_scr), (1, sb_scr)):
            s0t = jnp.concatenate([s0_ref[0, d].T] * HEADS, axis=0)
            scr[...] = jnp.where(blockdiag, s0t, 0.0) * jnp.where(is_lat, 1.0, 0.0)

    c = GLA_CHUNK
    n_chunks = SEQ_TILE // c

    def log_gate(d, lr_ref):
        z = _dot(lr_ref[...].astype(BF16), gu_ref[d].astype(BF16)) + gb_ref[d]
        return _log_sigmoid(z) / GLA_TAU

    def tile_stream(d, qk_ref, v_ref, la, scr):
        return _gla_tile(qk_ref[:, 0:GLA_QK] * (GLA_DK ** -0.5), qk_ref[:, GLA_QK:2 * GLA_QK],
                         v_ref[...], la, scr[...], rev=(d == 1))

    def chunk_stream(d, qk_ref, v_ref, la, o_ref, scr):
        st = scr[...]
        for cc in (range(n_chunks) if d == 0 else reversed(range(n_chunks))):
            rows = slice(cc * c, (cc + 1) * c)
            q = qk_ref[rows, 0:GLA_QK] * (GLA_DK ** -0.5)
            k = qk_ref[rows, GLA_QK:2 * GLA_QK]
            o, st = _gla_chunk(q, k, v_ref[rows, :], la[rows, :], st, rev=(d == 1))
            o_ref[rows, :] = o
        scr[...] = st

    la_f = log_gate(0, lrf_ref)
    la_b = log_gate(1, lrb_ref)
    decay = GLA_SUB * jnp.max(jnp.maximum(-la_f, -la_b))

    @pl.when(decay <= GLA_SAFE_DECAY)
    def _():
        (o_f, st_f), (o_b, st_b) = _interleave(tile_stream(0, qkf_ref, vf_ref, la_f, sf_scr),
                                               tile_stream(1, qkb_ref, vb_ref, la_b, sb_scr))
        of_ref[...] = o_f
        ob_ref[...] = o_b
        sf_scr[...] = st_f
        sb_scr[...] = st_b

    @pl.when(decay > GLA_SAFE_DECAY)
    def _():
        chunk_stream(0, qkf_ref, vf_ref, la_f, of_ref, sf_scr)
        chunk_stream(1, qkb_ref, vb_ref, la_b, ob_ref, sb_scr)

    @pl.when(jnp.logical_or(jnp.logical_not(is_lat), t == LAT_TILES - 1))
    def _():
        for d, scr in ((0, sf_scr), (1, sb_scr)):
            s = scr[...].T
            st_ref[0, d] = s[:, 0:HD] + s[:, HD:2 * HD] + s[:, 2 * HD:3 * HD] + s[:, 3 * HD:4 * HD]


def _gla(parts, gate_up_pad, gate_b, s0):
    n = SEQ_TILE
    in_specs = []
    for blk in (3, 4):
        in_specs += [
            pl.BlockSpec((n, GW), lambda s, blk=blk: (_seq_step(s)[blk], CB_AQK)),
            pl.BlockSpec((n, GW), lambda s, blk=blk: (_seq_step(s)[blk], CB_AV)),
            pl.BlockSpec((n, 128), lambda s, blk=blk: (_seq_step(s)[blk], CB_LR128)),
        ]
    in_specs += [pl.BlockSpec((2, 128, GLA_QK), lambda s: (0, 0, 0)),
                 pl.BlockSpec((2, 1, GLA_QK), lambda s: (0, 0, 0)),
                 pl.BlockSpec((1, 2, GLA_QK, HD), lambda s: (_seq_step(s)[1], 0, 0, 0))]
    return pl.pallas_call(
        _gla_kernel,
        grid=(SEQ_STEPS,),
        in_specs=in_specs,
        out_specs=[
            pl.BlockSpec((n, GW), lambda s: (_seq_step(s)[3], 0)),
            pl.BlockSpec((n, GW), lambda s: (_seq_step(s)[4], 0)),
            pl.BlockSpec((1, 2, GLA_QK, HD), lambda s: (jnp.minimum(s, B_CTX), 0, 0, 0)),
        ],
        out_shape=[
            jax.ShapeDtypeStruct((T, GW), F32),
            jax.ShapeDtypeStruct((T, GW), F32),
            jax.ShapeDtypeStruct((B_CTX + 1, 2, GLA_QK, HD), F32),
        ],
        scratch_shapes=[pltpu.VMEM((GW, GLA_QK), F32), pltpu.VMEM((GW, GLA_QK), F32)],
        compiler_params=_cparams(("arbitrary",)),
        name="gla",
    )(parts, parts, parts, parts, parts, parts, gate_up_pad, gate_b, s0)


OUT_TM = 512
FFN_TF = 1408
FFN_PASSES = D_FF // FFN_TF


def _outproj_ffn_kernel(*refs, n_x, n_out):
    x_refs, o_refs = refs[:n_x], refs[len(refs) - n_out:]
    (mod_ref, gpost_ref, gpre2_ref, gpost2_ref, pool_ref, na_ref, rf_ref, rb_ref, rg_ref, af_ref, ab_ref, ag_ref,
     ng_ref, w_ref, wg_ref, wu_ref, wd_ref) = refs[n_x:len(refs) - n_out]
    avg = jnp.where(_iota((GW, GW), 0) // HD == _iota((GW, GW), 1) // HD, 1.0 / HD, 0.0).astype(BF16)
    r = rf_ref[...] + rb_ref[...]
    r = r - _head_mean(r, avg)
    r = r * lax.rsqrt(_head_mean(r * r, avg) + GN_EPS) * _silu(rg_ref[...])
    a = af_ref[...] + ab_ref[...]
    a = a * lax.rsqrt(_head_mean(a * a, avg) + RMS_EPS) * ng_ref[...] * _silu(ag_ref[...])
    y = _dot(pool_ref[...].astype(BF16), w_ref[0, 0:GW, :])
    y = y + _dot(na_ref[...].astype(BF16), w_ref[0, GW:2 * GW, :])
    y = y + _dot(r.astype(BF16), w_ref[0, 2 * GW:3 * GW, :])
    y = y + _dot(a.astype(BF16), w_ref[0, 3 * GW:4 * GW, :])
    x1 = _x_tile(x_refs, OUT_TM) + mod_ref[0, 2:3, :] * _rms(y, gpost_ref[...])

    hb = (_rms(x1, gpre2_ref[...]) * (1.0 + mod_ref[0, 4:5, :]) + mod_ref[0, 3:4, :]).astype(BF16)
    y = None
    for j in range(FFN_PASSES):
        cols = slice(j * FFN_TF, (j + 1) * FFN_TF)
        act = (_silu(_dot(hb, wg_ref[0, :, cols])) * _dot(hb, wu_ref[0, :, cols])).astype(BF16)
        part = _dot(act, wd_ref[0, cols, :])
        y = part if y is None else y + part
    x2 = x1 + mod_ref[0, 5:6, :] * _rms(y, gpost2_ref[...])
    if len(o_refs) == 1:
        o_refs[0][...] = x2
    else:
        is_ctx = pl.program_id(0) < T_CTX // OUT_TM

        @pl.when(is_ctx)
        def _():
            o_refs[0][...] = x2

        @pl.when(jnp.logical_not(is_ctx))
        def _():
            o_refs[1][...] = x2


def _outproj_ffn(x, mod_l, g_post, g_pre2, g_post2, o_pool, o_na, ret_f, ret_b, gla_f, gla_b, parts, ng_lanes,
                 w_out_b, wg, wu, wd, layer, split_out):
    tm = OUT_TM
    tile = lambda i: (i, 0)
    act = pl.BlockSpec((tm, GW), tile)
    vec = pl.BlockSpec((1, D), lambda i: (0, 0))
    once = pl.Buffered(1)
    x_specs, x_args = _x_specs(x, tm)
    if split_out:
        out_specs, _ = _x_specs((None, None), tm)
        out_shape = [jax.ShapeDtypeStruct((T_CTX, D), F32), jax.ShapeDtypeStruct((T_LAT, D), F32)]
    else:
        out_specs = [pl.BlockSpec((tm, D), tile)]
        out_shape = [jax.ShapeDtypeStruct((T, D), F32)]
    out = pl.pallas_call(
        functools.partial(_outproj_ffn_kernel, n_x=len(x_args), n_out=len(out_shape)),
        grid=(T // tm,),
        in_specs=x_specs + [
            pl.BlockSpec((1, 6, D), lambda i: (_mod_row(i, tm), 0, 0)),
            vec, vec, vec,
            act, act, act, act,
            pl.BlockSpec((tm, GW), lambda i: (i, CB_RG)),
            act, act,
            pl.BlockSpec((tm, GW), lambda i: (i, CB_AG)),
            pl.BlockSpec((1, GW), lambda i: (0, 0)),
            pl.BlockSpec((1, D, D), lambda i: (layer, 0, 0), pipeline_mode=once),
            pl.BlockSpec((1, D, D_FF), lambda i: (layer, 0, 0), pipeline_mode=once),
            pl.BlockSpec((1, D, D_FF), lambda i: (layer, 0, 0), pipeline_mode=once),
            pl.BlockSpec((1, D_FF, D), lambda i: (layer, 0, 0), pipeline_mode=once),
        ],
        out_specs=out_specs,
        out_shape=out_shape,
        compiler_params=_cparams(("arbitrary",)),
        name="outproj_ffn",
    )(*x_args, mod_l, g_post, g_pre2, g_post2, o_pool, o_na, ret_f, ret_b, parts, gla_f, gla_b, parts, ng_lanes,
      w_out_b, wg, wu, wd)
    return tuple(out) if split_out else out[0]


def _rope_tables():
    nf = 16
    inv = (ROPE_BASE ** (-np.arange(nf, dtype=np.float32) / nf)).astype(np.float32)
    tok = np.arange(L_LAT)
    cos = np.zeros((L_LAT, HD), np.float32)
    sin = np.zeros((L_LAT, HD), np.float32)
    for axis, pos in enumerate((tok // GRID_W, tok % GRID_W)):
        ang = pos.astype(np.float32)[:, None] * inv[None, :]
        c, s = np.cos(ang), np.sin(ang)
        cos[:, axis * 32:axis * 32 + 32] = np.concatenate([c, c], axis=1)
        sin[:, axis * 32:axis * 32 + 32] = np.concatenate([-s, s], axis=1)
    return jnp.asarray(np.tile(cos, (1, HEADS))), jnp.asarray(np.tile(sin, (1, HEADS)))


def _block_diag(w):
    g, c, _ = w.shape
    out = jnp.zeros((g * c, g * c), w.dtype)
    for i in range(g):
        out = out.at[i * c:(i + 1) * c, i * c:(i + 1) * c].set(w[i])
    return out


def kernel(x_prompt, x_sample, cache_na_k, cache_na_v, state_ret, state_gla, c, c_ctx, w_mod, b_mod,
           g_pre_mix, g_post_mix, g_pre_ffn, g_post_ffn, w_in, w_out, pool_w, pool_scale, na_rpb,
           ret_decay_logit, gla_gate_up, gla_gate_b, gla_norm_g, w_ffn_gate, w_ffn_up, w_ffn_down):
    x = (x_prompt.reshape(T_CTX, D), x_sample.reshape(T_LAT, D))
    cv8 = jnp.concatenate([c_ctx[None, :], c, jnp.zeros((8 - 1 - B_LAT, D), F32)], axis=0)
    mods = _modulation(cv8, w_mod, b_mod).reshape(DEPTH, 8, 6, D)

    w_in_b = jnp.pad(w_in, ((0, 0), (0, 0), (0, P_PAD - P_IN))).astype(BF16)
    w_out_b = w_out.astype(BF16)
    wg_b, wu_b, wd_b = w_ffn_gate.astype(BF16), w_ffn_up.astype(BF16), w_ffn_down.astype(BF16)
    gate_up_pad = jnp.pad(gla_gate_up, ((0, 0), (0, 0), (0, 128 - GLA_LOWRANK), (0, 0)))
    rope_tabs = _rope_tables()
    ck = cache_na_k.reshape(B_LAT, DEPTH, PAST, GW)
    cv = cache_na_v.reshape(B_LAT, DEPTH, PAST, GW)
    s0_ret = state_ret.reshape(B_LAT, DEPTH, 2, GW, HD)
    s0_gla = state_gla.reshape(B_LAT, DEPTH, 2, GLA_QK, HD)

    ks, vs, srs, sgs = [], [], [], []
    for l in range(DEPTH):
        mod_l = mods[l]
        parts = _inproj(x, mod_l, g_pre_mix[l][None, :], w_in_b, l)
        o_pool = _pool(parts, _block_diag(pool_w[l]).astype(BF16), pool_scale[l][None, :])
        o_na = _attention(parts, ck, cv, na_rpb[l], l)
        lg_lanes = jnp.repeat(ret_decay_logit[l], HD, axis=1)
        rf, rb, s_ret = _retention(parts, lg_lanes, rope_tabs, s0_ret[:, l])
        gf, gbw, s_gla = _gla(parts, gate_up_pad[l], gla_gate_b[l][:, None, :], s0_gla[:, l])
        x = _outproj_ffn(x, mod_l, g_post_mix[l][None, :], g_pre_ffn[l][None, :], g_post_ffn[l][None, :],
                         o_pool, o_na, rf, rb, gf, gbw, parts, jnp.tile(gla_norm_g[l], HEADS)[None, :],
                         w_out_b, wg_b, wu_b, wd_b, l, split_out=(l == DEPTH - 1))
        ks.append(parts[:T_CTX, CB_NAK * GW:(CB_NAK + 1) * GW].reshape(B_CTX, L_CTX, HEADS, HD))
        vs.append(parts[:T_CTX, CB_NAV * GW:(CB_NAV + 1) * GW].reshape(B_CTX, L_CTX, HEADS, HD))
        srs.append(s_ret[:B_CTX].reshape(B_CTX, 2, HEADS, RET_DK, HD))
        sgs.append(s_gla[:B_CTX].reshape(B_CTX, 2, HEADS, GLA_DK, HD))

    return (x[0].reshape(B_CTX, L_CTX, D), x[1].reshape(B_LAT, L_LAT, D),
            jnp.stack(ks, axis=1), jnp.stack(vs, axis=1), jnp.stack(srs, axis=1), jnp.stack(sgs, axis=1))
```

```python
import functools

import numpy as np
import jax
import jax.numpy as jnp
from jax import lax
from jax.experimental import pallas as pl
from jax.experimental.pallas import tpu as pltpu

F32 = jnp.float32
BF16 = jnp.bfloat16

D = 1024
B_CTX, L_CTX = 32, 256
B_LAT, L_LAT = 2, 4096
DEPTH = 4
PAST = 256
GRID_W = 64
GRID_H = L_LAT // GRID_W
T_CTX = B_CTX * L_CTX
T_LAT = B_LAT * L_LAT
T = T_CTX + T_LAT
GW = 256
HEADS = 4
HD = 64
POOL_WINDOWS = (2, 4, 8, 16)
NA_ROWS, NA_COLS = 8, 16
RET_DK = 64
GLA_DK = 32
GLA_LOWRANK = 16
GLA_TAU = 16.0
D_FF = 2816
P_IN = 2832
P_PAD = 2944
ROPE_BASE = 10000.0
RMS_EPS = 1e-6
GN_EPS = 1e-5
NEG = -1e30

CB_POOL, CB_NAQ, CB_NAK, CB_NAV, CB_RQ, CB_RK, CB_RV, CB_RG, CB_AQK, CB_AV, CB_AG = range(11)
CB_LR128 = P_IN // 128

SEQ_TILE = L_CTX
SEQ_STEPS = T // SEQ_TILE
LAT_STEP0 = T_CTX // SEQ_TILE
LAT_TILES = L_LAT // SEQ_TILE
GLA_CHUNK = 64
GLA_SUB = 16
VMEM_LIMIT = 56 * 1024 * 1024


def _cparams(sem):
    return pltpu.CompilerParams(dimension_semantics=sem, vmem_limit_bytes=VMEM_LIMIT)


def _silu(x):
    return x / (1.0 + jnp.exp(-x))


def _log_sigmoid(z):
    return jnp.minimum(z, 0.0) - jnp.log1p(jnp.exp(-jnp.abs(z)))


def _rms(x, g):
    return x * lax.rsqrt(jnp.mean(x * x, axis=-1, keepdims=True) + RMS_EPS) * g


def _dot(a, b):
    return jnp.dot(a, b, preferred_element_type=F32)


def _dot_nt(a, b):
    return lax.dot_general(a, b, (((1,), (1,)), ((), ())), preferred_element_type=F32)


def _dot_tn(a, b):
    return lax.dot_general(a, b, (((0,), (0,)), ((), ())), preferred_element_type=F32)


def _split_hi_lo(x):
    hi = x.astype(BF16)
    return hi, (x - hi.astype(F32)).astype(BF16)


def _dot_exact01(a01, x):
    hi, lo = _split_hi_lo(x)
    return _dot(a01, hi) + _dot(a01, lo)


def _iota(shape, dim):
    return lax.broadcasted_iota(jnp.int32, shape, dim)


def _expand_heads(x, head_w):
    n, w = x.shape
    xe = jnp.concatenate([x] * HEADS, axis=0)
    rowh = _iota((HEADS * n, w), 0) // n
    laneh = (_iota((HEADS * n, w), 1) // head_w) % HEADS
    return jnp.where(rowh == laneh, xe, jnp.zeros_like(xe))


def _extract_heads(p, n):
    laneh = _iota((n, GW), 1) // HD
    out = p[0:n]
    for h in range(1, HEADS):
        out = jnp.where(laneh == h, p[h * n:(h + 1) * n], out)
    return out


def _head_mean(x, avg):
    hi, lo = _split_hi_lo(x)
    return _dot(hi, avg) + _dot(lo, avg)


def _interleave(*gens):
    results = [None] * len(gens)
    live = list(range(len(gens)))
    while live:
        for i in list(live):
            try:
                next(gens[i])
            except StopIteration as stop:
                results[i] = stop.value
                live.remove(i)
    return results


def _mod_row(i, tm):
    return jnp.where(i < T_CTX // tm, 0, 1 + (i * tm - T_CTX) // L_LAT)


def _mod_kernel(cv_ref, w_ref, b_ref, o_ref):
    s = _silu(cv_ref[...]).astype(BF16)
    o_ref[0] = _dot(s, w_ref[0].astype(BF16)) + b_ref[0]


def _modulation(cv8, w_mod, b_mod):
    tn = 1536
    return pl.pallas_call(
        _mod_kernel,
        grid=(DEPTH, 6 * D // tn),
        in_specs=[
            pl.BlockSpec((8, D), lambda l, j: (0, 0)),
            pl.BlockSpec((1, D, tn), lambda l, j: (l, 0, j)),
            pl.BlockSpec((1, 1, tn), lambda l, j: (l, 0, j)),
        ],
        out_specs=pl.BlockSpec((1, 8, tn), lambda l, j: (l, 0, j)),
        out_shape=jax.ShapeDtypeStruct((DEPTH, 8, 6 * D), F32),
        compiler_params=_cparams(("arbitrary", "arbitrary")),
        name="modulation",
    )(cv8, w_mod, b_mod.reshape(DEPTH, 1, 6 * D))


IN_TM = 512


def _x_specs(x, tm):
    if not isinstance(x, tuple):
        return [pl.BlockSpec((tm, D), lambda i: (i, 0))], [x]
    nc = T_CTX // tm
    return ([pl.BlockSpec((tm, D), lambda i: (jnp.minimum(i, nc - 1), 0)),
             pl.BlockSpec((tm, D), lambda i: (jnp.maximum(i - nc, 0), 0))], list(x))


def _x_tile(x_refs, tm):
    if len(x_refs) == 1:
        return x_refs[0][...]
    return jnp.where(pl.program_id(0) < T_CTX // tm, x_refs[0][...], x_refs[1][...])


def _inproj_kernel(*refs, n_x):
    x_refs, (mod_ref, g_ref, w_ref, o_ref) = refs[:n_x], refs[n_x:]
    h = _rms(_x_tile(x_refs, IN_TM), g_ref[...]) * (1.0 + mod_ref[0, 1:2, :]) + mod_ref[0, 0:1, :]
    hb = h.astype(BF16)
    for a in range(0, P_PAD, 1024):
        b = min(a + 1024, P_PAD)
        o_ref[:, a:b] = _dot(hb, w_ref[0, :, a:b])


def _inproj(x, mod_l, g_pre, w_in_b, layer):
    tm = IN_TM
    x_specs, x_args = _x_specs(x, tm)
    return pl.pallas_call(
        functools.partial(_inproj_kernel, n_x=len(x_args)),
        grid=(T // tm,),
        in_specs=x_specs + [
            pl.BlockSpec((1, 6, D), lambda i: (_mod_row(i, tm), 0, 0)),
            pl.BlockSpec((1, D), lambda i: (0, 0)),
            pl.BlockSpec((1, D, P_PAD), lambda i: (layer, 0, 0), pipeline_mode=pl.Buffered(1)),
        ],
        out_specs=pl.BlockSpec((tm, P_PAD), lambda i: (i, 0)),
        out_shape=jax.ShapeDtypeStruct((T, P_PAD), F32),
        compiler_params=_cparams(("arbitrary",)),
        name="inproj",
    )(*x_args, mod_l, g_pre, w_in_b)


POOL_TM = 512
POOL_SUB = 256


def _pool_kernel(v_ref, w_ref, scale_ref, o_ref, band_scr, cnt_scr):
    i = pl.program_id(0)
    n = POOL_SUB

    @pl.when(i == 0)
    def _():
        t = _iota((n, n), 0)
        s = _iota((n, n), 1)
        lane_g = _iota((n, GW), 1) // HD
        for kind, seg_len in enumerate((L_CTX, GRID_W)):
            seg0 = t & ~(seg_len - 1)
            seg1 = seg0 + seg_len
            cnt = jnp.zeros((n, GW), F32)
            for gi, win in enumerate(POOL_WINDOWS):
                lo = jnp.maximum(t - win // 2, seg0)
                hi = jnp.minimum(t - win // 2 + win, seg1)
                band_scr[kind, gi] = jnp.where(s >= lo, jnp.where(s < hi, 1.0, 0.0), 0.0).astype(BF16)
                cnt = jnp.where(lane_g == gi, (hi - lo).astype(F32), cnt)
            cnt_scr[kind] = cnt

    kind = jnp.where(i < T_CTX // POOL_TM, 0, 1)

    def piece(rows):
        v = v_ref[rows, :]
        vh, vl = _split_hi_lo(v)
        lane_g = _iota((n, GW), 1) // HD
        yield
        mean = None
        for gi in range(len(POOL_WINDOWS)):
            band = band_scr[kind, gi]
            m = _dot(band, vh) + _dot(band, vl)
            mean = m if mean is None else jnp.where(lane_g == gi, m, mean)
        yield
        d = (mean / cnt_scr[kind] - v).astype(BF16)
        o_ref[rows, :] = (_dot(d, w_ref[...]) * scale_ref[...]).astype(o_ref.dtype)

    _interleave(*[piece(slice(j * n, (j + 1) * n)) for j in range(POOL_TM // n)])


def _pool(parts, w_bd, scale):
    tm = POOL_TM
    n_win = len(POOL_WINDOWS)
    return pl.pallas_call(
        _pool_kernel,
        grid=(T // tm,),
        in_specs=[
            pl.BlockSpec((tm, GW), lambda i: (i, CB_POOL)),
            pl.BlockSpec((GW, GW), lambda i: (0, 0)),
            pl.BlockSpec((1, GW), lambda i: (0, 0)),
        ],
        out_specs=pl.BlockSpec((tm, GW), lambda i: (i, 0)),
        out_shape=jax.ShapeDtypeStruct((T, GW), BF16),
        scratch_shapes=[pltpu.VMEM((2, n_win, POOL_SUB, POOL_SUB), BF16), pltpu.VMEM((2, POOL_SUB, GW), F32)],
        compiler_params=_cparams(("arbitrary",)),
        name="pool",
    )(parts, w_bd, scale)


def _softmax_rows(s):
    m = jnp.max(s, axis=-1, keepdims=True)
    p = jnp.exp(s - m)
    return p / jnp.sum(p, axis=-1, keepdims=True)


def _ctx_attn_rows(q_ref, k_ref, v_ref, o_ref, rows):
    qe = _expand_heads(q_ref[rows, :].astype(BF16), HD)
    yield
    s = _dot_nt(qe, k_ref[rows, :].astype(BF16)) * (HD ** -0.5)
    yield
    p = _softmax_rows(s).astype(BF16)
    yield
    o_ref[rows, :] = _extract_heads(_dot(p, v_ref[rows, :].astype(BF16)), L_CTX).astype(o_ref.dtype)


NA_ROWS_PER_STEP = 8
ATT_ROWS = NA_ROWS_PER_STEP * GRID_W
ATT_CTX_STEPS = T_CTX // ATT_ROWS
ATT_LAT_STEPS = L_LAT // ATT_ROWS
NA_WIN = NA_ROWS * GRID_W
NA_DR = 2 * NA_ROWS - 1
NA_DC = 2 * NA_COLS - 1


def _na_bias_table(rpb_ref, e2_ref):
    shape = (GRID_W, 2 * GRID_W)
    qc = _iota(shape, 0)
    lane = _iota(shape, 1)
    kc = lane % GRID_W
    upper = lane >= GRID_W
    c0 = jnp.clip(qc - NA_COLS // 2, 0, GRID_W - NA_COLS)
    dc = jnp.where((kc >= c0) & (kc < c0 + NA_COLS), kc - qc + (NA_COLS - 1), -1)

    def one(ha, carry):
        h = ha // (NA_DR - 1)
        a = ha % (NA_DR - 1)
        acc = jnp.full(shape, NEG, F32)
        for j in range(NA_DC):
            val = jnp.where(upper, rpb_ref[h * NA_DR + a + 1, j], rpb_ref[h * NA_DR + a, j])
            acc = jnp.where(dc == j, val, acc)
        e2_ref[h, a] = acc
        return carry

    lax.fori_loop(0, HEADS * (NA_DR - 1), one, 0)


def _attn_kernel(q_ref, k_ref, v_ref, kseq_ref, vseq_ref, ck_ref, cv_ref, rpb_ref, o_ref, kb_ref, vb_ref, e2_ref):
    s = pl.program_id(0)

    @pl.when(s < ATT_CTX_STEPS)
    def _():
        _interleave(*[_ctx_attn_rows(q_ref, k_ref, v_ref, o_ref, slice(i * L_CTX, (i + 1) * L_CTX))
                      for i in range(ATT_ROWS // L_CTX)])

    @pl.when(s >= ATT_CTX_STEPS)
    def _():
        step = (s - ATT_CTX_STEPS) % ATT_LAT_STEPS

        @pl.when(s == ATT_CTX_STEPS)
        def _():
            _na_bias_table(rpb_ref, e2_ref)

        @pl.when(step == 0)
        def _():
            kb_ref[...] = kseq_ref[...].astype(BF16)
            vb_ref[...] = vseq_ref[...].astype(BF16)

        _na_rows(step, q_ref, ck_ref, cv_ref, o_ref, kb_ref, vb_ref, e2_ref)


def _na_rows(step, q_ref, ck_ref, cv_ref, o_ref, kb_ref, vb_ref, e2_ref):
    ckb = ck_ref[0, 0].astype(BF16)
    cvb = cv_ref[0, 0].astype(BF16)
    scale = HD ** -0.5

    def one_row(rr):
        r = step * NA_ROWS_PER_STEP + rr
        r0 = jnp.clip(r - NA_ROWS // 2, 0, GRID_H - NA_ROWS)
        base = r0 - r + (NA_ROWS - 1)
        q0 = pl.multiple_of(rr * GRID_W, GRID_W)
        k0 = pl.multiple_of(r0 * GRID_W, GRID_W)
        qe = _expand_heads(q_ref[pl.ds(q0, GRID_W), :].astype(BF16), HD)
        kw = kb_ref[pl.ds(k0, NA_WIN), :]
        vw = vb_ref[pl.ds(k0, NA_WIN), :]
        bias = jnp.concatenate(
            [jnp.concatenate([e2_ref[h, base + 2 * p] for p in range(NA_ROWS // 2)], axis=1)
             for h in range(HEADS)], axis=0)
        yield
        s_loc = _dot_nt(qe, kw) * scale + bias
        s_ctx = _dot_nt(qe, ckb) * scale
        yield
        m = jnp.maximum(jnp.max(s_loc, axis=-1, keepdims=True), jnp.max(s_ctx, axis=-1, keepdims=True))
        p_loc = jnp.exp(s_loc - m)
        p_ctx = jnp.exp(s_ctx - m)
        inv = 1.0 / (jnp.sum(p_loc, axis=-1, keepdims=True) + jnp.sum(p_ctx, axis=-1, keepdims=True))
        yield
        pv = _dot((p_loc * inv).astype(BF16), vw) + _dot((p_ctx * inv).astype(BF16), cvb)
        yield
        o_ref[pl.ds(q0, GRID_W), :] = _extract_heads(pv, GRID_W).astype(o_ref.dtype)

    def row_pair(i, carry):
        _interleave(one_row(2 * i), one_row(2 * i + 1))
        return carry

    lax.fori_loop(0, NA_ROWS_PER_STEP // 2, row_pair, 0)


def _attention(parts, ck, cv, rpb, layer):
    lat_req = lambda s: jnp.maximum(s - ATT_CTX_STEPS, 0) // ATT_LAT_STEPS
    seq_blk0 = T_CTX // L_LAT
    return pl.pallas_call(
        _attn_kernel,
        grid=(T // ATT_ROWS,),
        in_specs=[
            pl.BlockSpec((ATT_ROWS, GW), lambda s: (s, CB_NAQ)),
            pl.BlockSpec((ATT_ROWS, GW), lambda s: (s, CB_NAK)),
            pl.BlockSpec((ATT_ROWS, GW), lambda s: (s, CB_NAV)),
            pl.BlockSpec((L_LAT, GW), lambda s: (seq_blk0 + lat_req(s), CB_NAK)),
            pl.BlockSpec((L_LAT, GW), lambda s: (seq_blk0 + lat_req(s), CB_NAV)),
            pl.BlockSpec((1, 1, PAST, GW), lambda s: (lat_req(s), layer, 0, 0)),
            pl.BlockSpec((1, 1, PAST, GW), lambda s: (lat_req(s), layer, 0, 0)),
            pl.BlockSpec(memory_space=pltpu.SMEM),
        ],
        out_specs=pl.BlockSpec((ATT_ROWS, GW), lambda s: (s, 0)),
        out_shape=jax.ShapeDtypeStruct((T, GW), BF16),
        scratch_shapes=[pltpu.VMEM((L_LAT, GW), BF16), pltpu.VMEM((L_LAT, GW), BF16),
                        pltpu.VMEM((HEADS, NA_DR - 1, GRID_W, 2 * GRID_W), F32)],
        compiler_params=_cparams(("arbitrary",)),
        name="attention",
    )(parts, parts, parts, parts, parts, ck, cv, rpb.reshape(HEADS * NA_DR, NA_DC))


def _rope(x, cos, sin_signed):
    lane = _iota(x.shape, 1)
    partner = jnp.where(lane % 32 < 16, pltpu.roll(x, GW - 16, 1), pltpu.roll(x, 16, 1))
    return x * cos + partner * sin_signed


def _seq_step(s):
    is_lat = s >= LAT_STEP0
    u = jnp.maximum(s - LAT_STEP0, 0)
    b = u // LAT_TILES
    t = u % LAT_TILES
    bwd = jnp.where(is_lat, LAT_STEP0 + b * LAT_TILES + (LAT_TILES - 1 - t), s)
    return is_lat, b, t, s, bwd


def _ret_kernel(qf_ref, kf_ref, vf_ref, qb_ref, kb_ref, vb_ref, cf_ref, sf_ref, cb_ref, sb_ref, lg_ref, s0_ref,
                of_ref, ob_ref, st_ref, sf_scr, sb_scr, w_scr, dq_scr, dk_scr):
    s = pl.program_id(0)
    is_lat, _, t, _, _ = _seq_step(s)
    n = SEQ_TILE
    blockdiag = _iota((GW, GW), 0) // HD == _iota((GW, GW), 1) // HD
    lg = _log_sigmoid(lg_ref[...])

    @pl.when(s == 0)
    def _():
        ti = _iota((n, GW), 0).astype(F32)
        i_ = _iota((n, n), 0).astype(F32)
        j_ = _iota((n, n), 1).astype(F32)
        for d in range(2):
            lgd = lg[d:d + 1, :]
            diff = (i_ - j_) if d == 0 else (j_ - i_)
            pos = ti if d == 0 else (n - 1.0) - ti
            w_scr[d] = jnp.concatenate(
                [jnp.where(diff >= 0, jnp.exp(jnp.maximum(diff, 0.0) * lgd[:, h * HD:h * HD + 1]), 0.0)
                 for h in range(HEADS)], axis=0)
            dq_scr[d] = jnp.exp((pos + 1.0) * lgd)
            dk_scr[d] = jnp.exp((n - 1.0 - pos) * lgd)

    @pl.when(jnp.logical_or(jnp.logical_not(is_lat), t == 0))
    def _():
        for d, scr in ((0, sf_scr), (1, sb_scr)):
            s0 = jnp.concatenate([s0_ref[0, d]] * HEADS, axis=1)
            scr[...] = jnp.where(blockdiag, s0, 0.0) * jnp.where(is_lat, 1.0, 0.0)

    def stream(d, q_ref, k_ref, v_ref, c_ref, s_ref, o_ref, scr, rope):
        q = q_ref[...] * (RET_DK ** -0.5)
        k = k_ref[...]
        if rope:
            q = _rope(q, c_ref[...], s_ref[...])
            k = _rope(k, c_ref[...], s_ref[...])
        vb16 = v_ref[...].astype(BF16)
        yield
        a = _dot_nt(_expand_heads(q.astype(BF16), HD), k.astype(BF16)) * w_scr[d]
        yield
        o = _extract_heads(_dot(a.astype(BF16), vb16), n)
        yield
        s_old = scr[...]
        o_ref[...] = o + _dot(q.astype(BF16), s_old.astype(BF16)) * dq_scr[d]
        yield
        upd = _dot_tn((k * dk_scr[d]).astype(BF16), vb16)
        scr[...] = s_old * jnp.exp(float(n) * lg[d:d + 1, :]) + jnp.where(blockdiag, upd, 0.0)

    for rope in (False, True):
        @pl.when(is_lat if rope else jnp.logical_not(is_lat))
        def _():
            _interleave(stream(0, qf_ref, kf_ref, vf_ref, cf_ref, sf_ref, of_ref, sf_scr, rope),
                        stream(1, qb_ref, kb_ref, vb_ref, cb_ref, sb_ref, ob_ref, sb_scr, rope))

    @pl.when(jnp.logical_or(jnp.logical_not(is_lat), t == LAT_TILES - 1))
    def _():
        for d, scr in ((0, sf_scr), (1, sb_scr)):
            st = scr[...]
            st_ref[0, d] = st[:, 0:HD] + st[:, HD:2 * HD] + st[:, 2 * HD:3 * HD] + st[:, 3 * HD:4 * HD]


def _retention(parts, lg_lanes, rope_tabs, s0):
    n = SEQ_TILE
    fwd = lambda s: _seq_step(s)[3]
    bwd = lambda s: _seq_step(s)[4]
    in_specs = [pl.BlockSpec((n, GW), lambda s, c=c: (fwd(s), c)) for c in (CB_RQ, CB_RK, CB_RV)]
    in_specs += [pl.BlockSpec((n, GW), lambda s, c=c: (bwd(s), c)) for c in (CB_RQ, CB_RK, CB_RV)]
    in_specs += [pl.BlockSpec((n, GW), lambda s: (_seq_step(s)[2], 0))] * 2
    in_specs += [pl.BlockSpec((n, GW), lambda s: (LAT_TILES - 1 - _seq_step(s)[2], 0))] * 2
    in_specs += [pl.BlockSpec((2, GW), lambda s: (0, 0)),
                 pl.BlockSpec((1, 2, GW, HD), lambda s: (_seq_step(s)[1], 0, 0, 0))]
    return pl.pallas_call(
        _ret_kernel,
        grid=(SEQ_STEPS,),
        in_specs=in_specs,
        out_specs=[
            pl.BlockSpec((n, GW), lambda s: (fwd(s), 0)),
            pl.BlockSpec((n, GW), lambda s: (bwd(s), 0)),
            pl.BlockSpec((1, 2, GW, HD), lambda s: (jnp.minimum(s, B_CTX), 0, 0, 0)),
        ],
        out_shape=[
            jax.ShapeDtypeStruct((T, GW), F32),
            jax.ShapeDtypeStruct((T, GW), F32),
            jax.ShapeDtypeStruct((B_CTX + 1, 2, GW, HD), F32),
        ],
        scratch_shapes=[pltpu.VMEM((GW, GW), F32), pltpu.VMEM((GW, GW), F32),
                        pltpu.VMEM((2, HEADS * n, n), F32), pltpu.VMEM((2, n, GW), F32),
                        pltpu.VMEM((2, n, GW), F32)],
        compiler_params=_cparams(("arbitrary",)),
        name="retention",
    )(parts, parts, parts, parts, parts, parts, rope_tabs[0], rope_tabs[1], rope_tabs[0], rope_tabs[1],
      lg_lanes, s0)


GLA_QK = HEADS * GLA_DK
N_SUB = GLA_CHUNK // GLA_SUB
GLA_SAFE_DECAY = 60.0


def _gla_tile(q, k, v, la, st, rev):
    n, c = SEQ_TILE, GLA_CHUNK
    n_chunks = n // c
    ri = _iota((n, n), 0)
    ci = _iota((n, n), 1)
    same_chunk = ri // c == ci // c
    causal = (ci >= ri) if rev else (ci <= ri)
    tri = jnp.where(same_chunk, jnp.where(causal, 1.0, 0.0), 0.0).astype(BF16)
    b = _dot_exact01(tri, la)
    yield

    def rows_of(idx, count):
        if idx is None:
            return jnp.zeros((count, GLA_QK), F32)
        return jnp.broadcast_to(b[idx:idx + 1, :], (count, GLA_QK))

    def ref_row(cc, s):
        if rev:
            return cc * c + (s + 1) * GLA_SUB if s < N_SUB - 1 else None
        return cc * c + s * GLA_SUB - 1 if s > 0 else None

    end_rows = [cc * c if rev else cc * c + c - 1 for cc in range(n_chunks)]
    own_ref = jnp.concatenate([rows_of(ref_row(cc, s), GLA_SUB) for cc in range(n_chunks) for s in range(N_SUB)],
                              axis=0)
    b_end = jnp.concatenate([rows_of(r, c) for r in end_rows], axis=0)
    sub = (_iota((n, GLA_QK), 0) // GLA_SUB) % N_SUB
    qh = q * jnp.exp(b - own_ref)
    q_parts, k_parts = [], []
    for s in range(N_SUB):
        kvalid = (sub >= s) if rev else (sub <= s)
        ref_s = jnp.concatenate([rows_of(ref_row(cc, s), c) for cc in range(n_chunks)], axis=0)
        q_parts.append(jnp.where(sub == s, qh, 0.0))
        k_parts.append(jnp.where(kvalid, k * jnp.exp(jnp.minimum(ref_s - b, GLA_SAFE_DECAY)), 0.0))
    q_cat = _expand_heads(jnp.concatenate(q_parts, axis=1).astype(BF16), GLA_DK)
    k_cat = jnp.concatenate(k_parts, axis=1).astype(BF16)
    yield
    a = _dot_nt(q_cat, k_cat)
    yield
    qi = _iota((HEADS * n, n), 0) % n
    kj = _iota((HEADS * n, n), 1)
    keep = (qi // c == kj // c) & ((kj >= qi) if rev else (kj <= qi))
    vb16 = v.astype(BF16)
    o_intra = _extract_heads(_dot(jnp.where(keep, a, 0.0).astype(BF16), vb16), n)
    yield

    qt = (q * jnp.exp(b)).astype(BF16)
    kt = (k * jnp.exp(b_end - b)).astype(BF16)
    blockdiag = _iota((GW, GLA_QK), 0) // HD == _iota((GW, GLA_QK), 1) // GLA_DK
    upd = [jnp.where(blockdiag, _dot_tn(vb16[cc * c:(cc + 1) * c], kt[cc * c:(cc + 1) * c]), 0.0)
           for cc in range(n_chunks)]
    yield
    o_inter = [None] * n_chunks
    for cc in (reversed(range(n_chunks)) if rev else range(n_chunks)):
        o_inter[cc] = _dot_nt(qt[cc * c:(cc + 1) * c], st.astype(BF16))
        st = st * jnp.exp(b[end_rows[cc]:end_rows[cc] + 1, :]) + upd[cc]
        yield
    return o_intra + jnp.concatenate(o_inter, axis=0), st


def _gla_chunk(q, k, v, la, st, rev):
    c = GLA_CHUNK
    ri = _iota((c, c), 0)
    ci = _iota((c, c), 1)
    tri = jnp.where((ci >= ri) if rev else (ci <= ri), 1.0, 0.0).astype(BF16)
    b = _dot_exact01(tri, la)
    b_end = b[0:1, :] if rev else b[c - 1:c, :]
    row = _iota((c, GLA_QK), 0)
    sub = row // GLA_SUB
    off = row % GLA_SUB

    o = _dot_nt((q * jnp.exp(b)).astype(BF16), st.astype(BF16))
    kt = (k * jnp.exp(b_end - b)).astype(BF16)
    lane_h = _iota((GW, GLA_QK), 1) // GLA_DK
    row_h = _iota((GW, GLA_QK), 0) // HD
    st_new = st * jnp.exp(b_end) + jnp.where(row_h == lane_h, _dot_tn(v.astype(BF16), kt), 0.0)

    q_parts, k_parts = [], []
    for s in range(1, N_SUB):
        if rev:
            qsub, brow = N_SUB - 1 - s, b[(N_SUB - s) * GLA_SUB:(N_SUB - s) * GLA_SUB + 1, :]
            kvalid = sub > qsub
        else:
            qsub, brow = s, b[s * GLA_SUB - 1:s * GLA_SUB, :]
            kvalid = sub < qsub
        q_parts.append(jnp.where(sub == qsub, q * jnp.exp(jnp.where(sub == qsub, b - brow, 0.0)), 0.0))
        k_parts.append(jnp.where(kvalid, k * jnp.exp(jnp.where(kvalid, brow - b, 0.0)), 0.0))
    q_cat = _expand_heads(jnp.concatenate(q_parts, axis=1).astype(BF16), GLA_DK)
    k_cat = jnp.concatenate(k_parts, axis=1).astype(BF16)
    a_off = _dot_nt(q_cat, k_cat)
    o = o + _extract_heads(_dot(a_off.astype(BF16), v.astype(BF16)), c)

    red = jnp.where(_iota((GLA_QK, GW), 0) // GLA_DK == _iota((GLA_QK, GW), 1) // HD, 1.0, 0.0).astype(BF16)
    rowv = _iota((c, GW), 0) % GLA_SUB
    for dl in range(GLA_SUB):
        if dl == 0:
            x = q * k
            vs = v
        else:
            sh = dl if not rev else c - dl
            valid = (off + dl < GLA_SUB) if rev else (off >= dl)
            ks = pltpu.roll(k, sh, 0)
            bs = pltpu.roll(b, sh, 0)
            vs = pltpu.roll(v, sh, 0)
            x = jnp.where(valid, q * ks * jnp.exp(jnp.where(valid, b - bs, 0.0)), 0.0)
            validv = (rowv + dl < GLA_SUB) if rev else (rowv >= dl)
            vs = jnp.where(validv, vs, 0.0)
        o = o + _dot(x.astype(BF16), red) * vs
    return o, st_new


def _gla_kernel(qkf_ref, vf_ref, lrf_ref, qkb_ref, vb_ref, lrb_ref, gu_ref, gb_ref, s0_ref,
                of_ref, ob_ref, st_ref, sf_scr, sb_scr):
    is_lat, _, t, _, _ = _seq_step(pl.program_id(0))
    lane_h = _iota((GW, GLA_QK), 1) // GLA_DK
    row_h = _iota((GW, GLA_QK), 0) // HD
    blockdiag = row_h == lane_h

    @pl.when(jnp.logical_or(jnp.logical_not(is_lat), t == 0))
    def _():
        for d, scr in ((0, sf_scr), (1, sb_scr)):
            s0t = jnp.concatenate([s0_ref[0, d].T] * HEADS, axis=0)
            scr[...] = jnp.where(blockdiag, s0t, 0.0) * jnp.where(is_lat, 1.0, 0.0)

    c = GLA_CHUNK
    n_chunks = SEQ_TILE // c

    def log_gate(d, lr_ref):
        z = _dot(lr_ref[...].astype(BF16), gu_ref[d].astype(BF16)) + gb_ref[d]
        return _log_sigmoid(z) / GLA_TAU

    def tile_stream(d, qk_ref, v_ref, la, scr):
        return _gla_tile(qk_ref[:, 0:GLA_QK] * (GLA_DK ** -0.5), qk_ref[:, GLA_QK:2 * GLA_QK],
                         v_ref[...], la, scr[...], rev=(d == 1))

    def chunk_stream(d, qk_ref, v_ref, la, o_ref, scr):
        st = scr[...]
        for cc in (range(n_chunks) if d == 0 else reversed(range(n_chunks))):
            rows = slice(cc * c, (cc + 1) * c)
            q = qk_ref[rows, 0:GLA_QK] * (GLA_DK ** -0.5)
            k = qk_ref[rows, GLA_QK:2 * GLA_QK]
            o, st = _gla_chunk(q, k, v_ref[rows, :], la[rows, :], st, rev=(d == 1))
            o_ref[rows, :] = o
        scr[...] = st

    la_f = log_gate(0, lrf_ref)
    la_b = log_gate(1, lrb_ref)
    decay = GLA_SUB * jnp.max(jnp.maximum(-la_f, -la_b))

    @pl.when(decay <= GLA_SAFE_DECAY)
    def _():
        (o_f, st_f), (o_b, st_b) = _interleave(tile_stream(0, qkf_ref, vf_ref, la_f, sf_scr),
                                               tile_stream(1, qkb_ref, vb_ref, la_b, sb_scr))
        of_ref[...] = o_f
        ob_ref[...] = o_b
        sf_scr[...] = st_f
        sb_scr[...] = st_b

    @pl.when(decay > GLA_SAFE_DECAY)
    def _():
        chunk_stream(0, qkf_ref, vf_ref, la_f, of_ref, sf_scr)
        chunk_stream(1, qkb_ref, vb_ref, la_b, ob_ref, sb_scr)

    @pl.when(jnp.logical_or(jnp.logical_not(is_lat), t == LAT_TILES - 1))
    def _():
        for d, scr in ((0, sf_scr), (1, sb_scr)):
            s = scr[...].T
            st_ref[0, d] = s[:, 0:HD] + s[:, HD:2 * HD] + s[:, 2 * HD:3 * HD] + s[:, 3 * HD:4 * HD]


def _gla(parts, gate_up_pad, gate_b, s0):
    n = SEQ_TILE
    in_specs = []
    for blk in (3, 4):
        in_specs += [
            pl.BlockSpec((n, GW), lambda s, blk=blk: (_seq_step(s)[blk], CB_AQK)),
            pl.BlockSpec((n, GW), lambda s, blk=blk: (_seq_step(s)[blk], CB_AV)),
            pl.BlockSpec((n, 128), lambda s, blk=blk: (_seq_step(s)[blk], CB_LR128)),
        ]
    in_specs += [pl.BlockSpec((2, 128, GLA_QK), lambda s: (0, 0, 0)),
                 pl.BlockSpec((2, 1, GLA_QK), lambda s: (0, 0, 0)),
                 pl.BlockSpec((1, 2, GLA_QK, HD), lambda s: (_seq_step(s)[1], 0, 0, 0))]
    return pl.pallas_call(
        _gla_kernel,
        grid=(SEQ_STEPS,),
        in_specs=in_specs,
        out_specs=[
            pl.BlockSpec((n, GW), lambda s: (_seq_step(s)[3], 0)),
            pl.BlockSpec((n, GW), lambda s: (_seq_step(s)[4], 0)),
            pl.BlockSpec((1, 2, GLA_QK, HD), lambda s: (jnp.minimum(s, B_CTX), 0, 0, 0)),
        ],
        out_shape=[
            jax.ShapeDtypeStruct((T, GW), F32),
            jax.ShapeDtypeStruct((T, GW), F32),
            jax.ShapeDtypeStruct((B_CTX + 1, 2, GLA_QK, HD), F32),
        ],
        scratch_shapes=[pltpu.VMEM((GW, GLA_QK), F32), pltpu.VMEM((GW, GLA_QK), F32)],
        compiler_params=_cparams(("arbitrary",)),
        name="gla",
    )(parts, parts, parts, parts, parts, parts, gate_up_pad, gate_b, s0)


OUT_TM = 512
FFN_TF = 1408
FFN_PASSES = D_FF // FFN_TF
OUT_SKEW = 3


def _outproj_ffn_kernel(*refs, n_x, n_out):
    x_refs, o_refs = refs[:n_x], refs[len(refs) - n_out:]
    (mod_ref, gpost_ref, gpre2_ref, gpost2_ref, pool_ref, na_ref, rf_ref, rb_ref, rg_ref, af_ref, ab_ref, ag_ref,
     ng_ref, w_ref, wg_ref, wu_ref, wd_ref) = refs[n_x:len(refs) - n_out]
    avg = jnp.where(_iota((GW, GW), 0) // HD == _iota((GW, GW), 1) // HD, 1.0 / HD, 0.0).astype(BF16)
    x_tile = _x_tile(x_refs, OUT_TM)

    def rows_block(rows):
        r = rf_ref[rows, :] + rb_ref[rows, :]
        r = r - _head_mean(r, avg)
        r = r * lax.rsqrt(_head_mean(r * r, avg) + GN_EPS) * _silu(rg_ref[rows, :])
        a = af_ref[rows, :] + ab_ref[rows, :]
        a = a * lax.rsqrt(_head_mean(a * a, avg) + RMS_EPS) * ng_ref[...] * _silu(ag_ref[rows, :])
        yield
        y = _dot(pool_ref[rows, :].astype(BF16), w_ref[0, 0:GW, :])
        y = y + _dot(na_ref[rows, :].astype(BF16), w_ref[0, GW:2 * GW, :])
        y = y + _dot(r.astype(BF16), w_ref[0, 2 * GW:3 * GW, :])
        y = y + _dot(a.astype(BF16), w_ref[0, 3 * GW:4 * GW, :])
        yield
        x1 = x_tile[rows] + mod_ref[0, 2:3, :] * _rms(y, gpost_ref[...])
        hb = (_rms(x1, gpre2_ref[...]) * (1.0 + mod_ref[0, 4:5, :]) + mod_ref[0, 3:4, :]).astype(BF16)
        yield
        y = None
        for j in range(FFN_PASSES):
            cols = slice(j * FFN_TF, (j + 1) * FFN_TF)
            act = (_silu(_dot(hb, wg_ref[0, :, cols])) * _dot(hb, wu_ref[0, :, cols])).astype(BF16)
            yield
            part = _dot(act, wd_ref[0, cols, :])
            y = part if y is None else y + part
            yield
        return x1 + mod_ref[0, 5:6, :] * _rms(y, gpost2_ref[...])

    half = OUT_TM // 2
    first, second = rows_block(slice(0, half)), rows_block(slice(half, OUT_TM))
    for _ in range(OUT_SKEW):
        next(first)
    x2 = jnp.concatenate(_interleave(first, second), axis=0)
    if len(o_refs) == 1:
        o_refs[0][...] = x2
    else:
        is_ctx = pl.program_id(0) < T_CTX // OUT_TM

        @pl.when(is_ctx)
        def _():
            o_refs[0][...] = x2

        @pl.when(jnp.logical_not(is_ctx))
        def _():
            o_refs[1][...] = x2


def _outproj_ffn(x, mod_l, g_post, g_pre2, g_post2, o_pool, o_na, ret_f, ret_b, gla_f, gla_b, parts, ng_lanes,
                 w_out_b, wg, wu, wd, layer, split_out):
    tm = OUT_TM
    tile = lambda i: (i, 0)
    act = pl.BlockSpec((tm, GW), tile)
    vec = pl.BlockSpec((1, D), lambda i: (0, 0))
    once = pl.Buffered(1)
    x_specs, x_args = _x_specs(x, tm)
    if split_out:
        out_specs, _ = _x_specs((None, None), tm)
        out_shape = [jax.ShapeDtypeStruct((T_CTX, D), F32), jax.ShapeDtypeStruct((T_LAT, D), F32)]
    else:
        out_specs = [pl.BlockSpec((tm, D), tile)]
        out_shape = [jax.ShapeDtypeStruct((T, D), F32)]
    out = pl.pallas_call(
        functools.partial(_outproj_ffn_kernel, n_x=len(x_args), n_out=len(out_shape)),
        grid=(T // tm,),
        in_specs=x_specs + [
            pl.BlockSpec((1, 6, D), lambda i: (_mod_row(i, tm), 0, 0)),
            vec, vec, vec,
            act, act, act, act,
            pl.BlockSpec((tm, GW), lambda i: (i, CB_RG)),
            act, act,
            pl.BlockSpec((tm, GW), lambda i: (i, CB_AG)),
            pl.BlockSpec((1, GW), lambda i: (0, 0)),
            pl.BlockSpec((1, D, D), lambda i: (layer, 0, 0), pipeline_mode=once),
            pl.BlockSpec((1, D, D_FF), lambda i: (layer, 0, 0), pipeline_mode=once),
            pl.BlockSpec((1, D, D_FF), lambda i: (layer, 0, 0), pipeline_mode=once),
            pl.BlockSpec((1, D_FF, D), lambda i: (layer, 0, 0), pipeline_mode=once),
        ],
        out_specs=out_specs,
        out_shape=out_shape,
        compiler_params=_cparams(("arbitrary",)),
        name="outproj_ffn",
    )(*x_args, mod_l, g_post, g_pre2, g_post2, o_pool, o_na, ret_f, ret_b, parts, gla_f, gla_b, parts, ng_lanes,
      w_out_b, wg, wu, wd)
    return tuple(out) if split_out else out[0]


def _rope_tables():
    nf = 16
    inv = (ROPE_BASE ** (-np.arange(nf, dtype=np.float32) / nf)).astype(np.float32)
    tok = np.arange(L_LAT)
    cos = np.zeros((L_LAT, HD), np.float32)
    sin = np.zeros((L_LAT, HD), np.float32)
    for axis, pos in enumerate((tok // GRID_W, tok % GRID_W)):
        ang = pos.astype(np.float32)[:, None] * inv[None, :]
        c, s = np.cos(ang), np.sin(ang)
        cos[:, axis * 32:axis * 32 + 32] = np.concatenate([c, c], axis=1)
        sin[:, axis * 32:axis * 32 + 32] = np.concatenate([-s, s], axis=1)
    return jnp.asarray(np.tile(cos, (1, HEADS))), jnp.asarray(np.tile(sin, (1, HEADS)))


def _block_diag(w):
    g, c, _ = w.shape
    out = jnp.zeros((g * c, g * c), w.dtype)
    for i in range(g):
        out = out.at[i * c:(i + 1) * c, i * c:(i + 1) * c].set(w[i])
    return out


def kernel(x_prompt, x_sample, cache_na_k, cache_na_v, state_ret, state_gla, c, c_ctx, w_mod, b_mod,
           g_pre_mix, g_post_mix, g_pre_ffn, g_post_ffn, w_in, w_out, pool_w, pool_scale, na_rpb,
           ret_decay_logit, gla_gate_up, gla_gate_b, gla_norm_g, w_ffn_gate, w_ffn_up, w_ffn_down):
    x = (x_prompt.reshape(T_CTX, D), x_sample.reshape(T_LAT, D))
    cv8 = jnp.concatenate([c_ctx[None, :], c, jnp.zeros((8 - 1 - B_LAT, D), F32)], axis=0)
    mods = _modulation(cv8, w_mod, b_mod).reshape(DEPTH, 8, 6, D)

    w_in_b = jnp.pad(w_in, ((0, 0), (0, 0), (0, P_PAD - P_IN))).astype(BF16)
    w_out_b = w_out.astype(BF16)
    wg_b, wu_b, wd_b = w_ffn_gate.astype(BF16), w_ffn_up.astype(BF16), w_ffn_down.astype(BF16)
    gate_up_pad = jnp.pad(gla_gate_up, ((0, 0), (0, 0), (0, 128 - GLA_LOWRANK), (0, 0)))
    rope_tabs = _rope_tables()
    ck = cache_na_k.reshape(B_LAT, DEPTH, PAST, GW)
    cv = cache_na_v.reshape(B_LAT, DEPTH, PAST, GW)
    s0_ret = state_ret.reshape(B_LAT, DEPTH, 2, GW, HD)
    s0_gla = state_gla.reshape(B_LAT, DEPTH, 2, GLA_QK, HD)

    ks, vs, srs, sgs = [], [], [], []
    for l in range(DEPTH):
        mod_l = mods[l]
        parts = _inproj(x, mod_l, g_pre_mix[l][None, :], w_in_b, l)
        o_pool = _pool(parts, _block_diag(pool_w[l]).astype(BF16), pool_scale[l][None, :])
        o_na = _attention(parts, ck, cv, na_rpb[l], l)
        lg_lanes = jnp.repeat(ret_decay_logit[l], HD, axis=1)
        rf, rb, s_ret = _retention(parts, lg_lanes, rope_tabs, s0_ret[:, l])
        gf, gbw, s_gla = _gla(parts, gate_up_pad[l], gla_gate_b[l][:, None, :], s0_gla[:, l])
        x = _outproj_ffn(x, mod_l, g_post_mix[l][None, :], g_pre_ffn[l][None, :], g_post_ffn[l][None, :],
                         o_pool, o_na, rf, rb, gf, gbw, parts, jnp.tile(gla_norm_g[l], HEADS)[None, :],
                         w_out_b, wg_b, wu_b, wd_b, l, split_out=(l == DEPTH - 1))
        ks.append(parts[:T_CTX, CB_NAK * GW:(CB_NAK + 1) * GW].reshape(B_CTX, L_CTX, HEADS, HD))
        vs.append(parts[:T_CTX, CB_NAV * GW:(CB_NAV + 1) * GW].reshape(B_CTX, L_CTX, HEADS, HD))
        srs.append(s_ret[:B_CTX].reshape(B_CTX, 2, HEADS, RET_DK, HD))
        sgs.append(s_gla[:B_CTX].reshape(B_CTX, 2, HEADS, GLA_DK, HD))

    return (x[0].reshape(B_CTX, L_CTX, D), x[1].reshape(B_LAT, L_LAT, D),
            jnp.stack(ks, axis=1), jnp.stack(vs, axis=1), jnp.stack(srs, axis=1), jnp.stack(sgs, axis=1))
```

```python
import functools

import numpy as np
import jax
import jax.numpy as jnp
from jax import lax
from jax.experimental import pallas as pl
from jax.experimental.pallas import tpu as pltpu

F32 = jnp.float32
BF16 = jnp.bfloat16

D = 1024
B_CTX, L_CTX = 32, 256
B_LAT, L_LAT = 2, 4096
DEPTH = 4
PAST = 256
GRID_W = 64
GRID_H = L_LAT // GRID_W
T_CTX = B_CTX * L_CTX
T_LAT = B_LAT * L_LAT
T = T_CTX + T_LAT
GW = 256
HEADS = 4
HD = 64
POOL_WINDOWS = (2, 4, 8, 16)
NA_ROWS, NA_COLS = 8, 16
RET_DK = 64
GLA_DK = 32
GLA_LOWRANK = 16
GLA_TAU = 16.0
D_FF = 2816
P_IN = 2832
P_PAD = 2944
ROPE_BASE = 10000.0
RMS_EPS = 1e-6
GN_EPS = 1e-5
NEG = -1e30

CB_POOL, CB_NAQ, CB_NAK, CB_NAV, CB_RQ, CB_RK, CB_RV, CB_RG, CB_AQK, CB_AV, CB_AG = range(11)
CB_LR128 = P_IN // 128

SEQ_TILE = L_CTX
SEQ_STEPS = T // SEQ_TILE
LAT_STEP0 = T_CTX // SEQ_TILE
LAT_TILES = L_LAT // SEQ_TILE
GLA_CHUNK = 64
GLA_SUB = 16
VMEM_LIMIT = 56 * 1024 * 1024


def _cparams(sem):
    return pltpu.CompilerParams(dimension_semantics=sem, vmem_limit_bytes=VMEM_LIMIT)


def _silu(x):
    return x / (1.0 + jnp.exp(-x))


def _log_sigmoid(z):
    return jnp.minimum(z, 0.0) - jnp.log1p(jnp.exp(-jnp.abs(z)))


def _rms(x, g):
    return x * lax.rsqrt(jnp.mean(x * x, axis=-1, keepdims=True) + RMS_EPS) * g


def _dot(a, b):
    return jnp.dot(a, b, preferred_element_type=F32)


def _dot_nt(a, b):
    return lax.dot_general(a, b, (((1,), (1,)), ((), ())), preferred_element_type=F32)


def _dot_tn(a, b):
    return lax.dot_general(a, b, (((0,), (0,)), ((), ())), preferred_element_type=F32)


def _split_hi_lo(x):
    hi = x.astype(BF16)
    return hi, (x - hi.astype(F32)).astype(BF16)


def _dot_exact01(a01, x):
    hi, lo = _split_hi_lo(x)
    return _dot(a01, hi) + _dot(a01, lo)


def _iota(shape, dim):
    return lax.broadcasted_iota(jnp.int32, shape, dim)


def _expand_heads(x, head_w):
    n, w = x.shape
    xe = jnp.concatenate([x] * HEADS, axis=0)
    rowh = _iota((HEADS * n, w), 0) // n
    laneh = (_iota((HEADS * n, w), 1) // head_w) % HEADS
    return jnp.where(rowh == laneh, xe, jnp.zeros_like(xe))


def _extract_heads(p, n):
    laneh = _iota((n, GW), 1) // HD
    out = p[0:n]
    for h in range(1, HEADS):
        out = jnp.where(laneh == h, p[h * n:(h + 1) * n], out)
    return out


def _head_mean(x, avg):
    hi, lo = _split_hi_lo(x)
    return _dot(hi, avg) + _dot(lo, avg)


def _interleave(*gens):
    results = [None] * len(gens)
    live = list(range(len(gens)))
    while live:
        for i in list(live):
            try:
                next(gens[i])
            except StopIteration as stop:
                results[i] = stop.value
                live.remove(i)
    return results


def _mod_row(i, tm):
    return jnp.where(i < T_CTX // tm, 0, 1 + (i * tm - T_CTX) // L_LAT)


def _mod_kernel(cv_ref, w_ref, b_ref, o_ref):
    s = _silu(cv_ref[...]).astype(BF16)
    o_ref[0] = _dot(s, w_ref[0].astype(BF16)) + b_ref[0]


def _modulation(cv8, w_mod, b_mod):
    tn = 1536
    return pl.pallas_call(
        _mod_kernel,
        grid=(DEPTH, 6 * D // tn),
        in_specs=[
            pl.BlockSpec((8, D), lambda l, j: (0, 0)),
            pl.BlockSpec((1, D, tn), lambda l, j: (l, 0, j)),
            pl.BlockSpec((1, 1, tn), lambda l, j: (l, 0, j)),
        ],
        out_specs=pl.BlockSpec((1, 8, tn), lambda l, j: (l, 0, j)),
        out_shape=jax.ShapeDtypeStruct((DEPTH, 8, 6 * D), F32),
        compiler_params=_cparams(("arbitrary", "arbitrary")),
        name="modulation",
    )(cv8, w_mod, b_mod.reshape(DEPTH, 1, 6 * D))


IN_TM = 512


def _x_specs(x, tm, tile_of=lambda i: i):
    if not isinstance(x, tuple):
        return [pl.BlockSpec((tm, D), lambda i: (tile_of(i), 0))], [x]
    nc = T_CTX // tm
    return ([pl.BlockSpec((tm, D), lambda i: (jnp.minimum(tile_of(i), nc - 1), 0)),
             pl.BlockSpec((tm, D), lambda i: (jnp.maximum(tile_of(i) - nc, 0), 0))], list(x))


def _x_tile(x_refs, tm, tile):
    if len(x_refs) == 1:
        return x_refs[0][...]
    return jnp.where(tile < T_CTX // tm, x_refs[0][...], x_refs[1][...])


def _inproj_kernel(*refs, n_x):
    x_refs, (mod_ref, g_ref, w_ref, o_ref, wb_scr) = refs[:n_x], refs[n_x:]

    @pl.when(pl.program_id(0) == 0)
    def _():
        wb_scr[...] = jnp.zeros((D, P_PAD), BF16)
        wb_scr[:, 0:P_IN] = w_ref[0].astype(BF16)

    x = _x_tile(x_refs, IN_TM, pl.program_id(0))

    def half_tile(rows):
        h = _rms(x[rows], g_ref[...]) * (1.0 + mod_ref[0, 1:2, :]) + mod_ref[0, 0:1, :]
        hb = h.astype(BF16)
        yield
        for a in range(0, P_PAD, 1024):
            b = min(a + 1024, P_PAD)
            o_ref[rows, a:b] = _dot(hb, wb_scr[:, a:b])
            yield

    half = IN_TM // 2
    first, second = half_tile(slice(0, half)), half_tile(slice(half, IN_TM))
    next(first)
    _interleave(first, second)


def _inproj(x, mod_l, g_pre, w_in, layer):
    tm = IN_TM
    x_specs, x_args = _x_specs(x, tm)
    return pl.pallas_call(
        functools.partial(_inproj_kernel, n_x=len(x_args)),
        grid=(T // tm,),
        in_specs=x_specs + [
            pl.BlockSpec((1, 6, D), lambda i: (_mod_row(i, tm), 0, 0)),
            pl.BlockSpec((1, D), lambda i: (0, 0)),
            pl.BlockSpec((1, D, P_IN), lambda i: (layer, 0, 0), pipeline_mode=pl.Buffered(1)),
        ],
        out_specs=pl.BlockSpec((tm, P_PAD), lambda i: (i, 0)),
        out_shape=jax.ShapeDtypeStruct((T, P_PAD), F32),
        scratch_shapes=[pltpu.VMEM((D, P_PAD), BF16)],
        compiler_params=_cparams(("arbitrary",)),
        name="inproj",
    )(*x_args, mod_l, g_pre, w_in)


POOL_TM = 512
POOL_SUB = 256


def _pool_kernel(v_ref, w_ref, scale_ref, o_ref, band_scr, cnt_scr):
    i = pl.program_id(0)
    n = POOL_SUB

    @pl.when(i == 0)
    def _():
        t = _iota((n, n), 0)
        s = _iota((n, n), 1)
        lane_g = _iota((n, GW), 1) // HD
        for kind, seg_len in enumerate((L_CTX, GRID_W)):
            seg0 = t & ~(seg_len - 1)
            seg1 = seg0 + seg_len
            cnt = jnp.zeros((n, GW), F32)
            for gi, win in enumerate(POOL_WINDOWS):
                lo = jnp.maximum(t - win // 2, seg0)
                hi = jnp.minimum(t - win // 2 + win, seg1)
                band_scr[kind, gi] = jnp.where(s >= lo, jnp.where(s < hi, 1.0, 0.0), 0.0).astype(BF16)
                cnt = jnp.where(lane_g == gi, (hi - lo).astype(F32), cnt)
            cnt_scr[kind] = cnt

    kind = jnp.where(i < T_CTX // POOL_TM, 0, 1)

    def piece(rows):
        v = v_ref[rows, :]
        vh, vl = _split_hi_lo(v)
        lane_g = _iota((n, GW), 1) // HD
        yield
        mean = None
        for gi in range(len(POOL_WINDOWS)):
            band = band_scr[kind, gi]
            m = _dot(band, vh) + _dot(band, vl)
            mean = m if mean is None else jnp.where(lane_g == gi, m, mean)
        yield
        d = (mean / cnt_scr[kind] - v).astype(BF16)
        o_ref[rows, :] = (_dot(d, w_ref[...]) * scale_ref[...]).astype(o_ref.dtype)

    _interleave(*[piece(slice(j * n, (j + 1) * n)) for j in range(POOL_TM // n)])


def _pool(parts, w_bd, scale):
    tm = POOL_TM
    n_win = len(POOL_WINDOWS)
    return pl.pallas_call(
        _pool_kernel,
        grid=(T // tm,),
        in_specs=[
            pl.BlockSpec((tm, GW), lambda i: (i, CB_POOL)),
            pl.BlockSpec((GW, GW), lambda i: (0, 0)),
            pl.BlockSpec((1, GW), lambda i: (0, 0)),
        ],
        out_specs=pl.BlockSpec((tm, GW), lambda i: (i, 0)),
        out_shape=jax.ShapeDtypeStruct((T, GW), BF16),
        scratch_shapes=[pltpu.VMEM((2, n_win, POOL_SUB, POOL_SUB), BF16), pltpu.VMEM((2, POOL_SUB, GW), F32)],
        compiler_params=_cparams(("arbitrary",)),
        name="pool",
    )(parts, w_bd, scale)


def _softmax_rows(s):
    m = jnp.max(s, axis=-1, keepdims=True)
    p = jnp.exp(s - m)
    return p / jnp.sum(p, axis=-1, keepdims=True)


def _ctx_attn_rows(q_ref, k_ref, v_ref, o_ref, rows):
    qe = _expand_heads(q_ref[rows, :].astype(BF16), HD)
    yield
    s = _dot_nt(qe, k_ref[rows, :].astype(BF16)) * (HD ** -0.5)
    yield
    p = _softmax_rows(s).astype(BF16)
    yield
    o_ref[rows, :] = _extract_heads(_dot(p, v_ref[rows, :].astype(BF16)), L_CTX).astype(o_ref.dtype)


NA_ROWS_PER_STEP = 8
NA_INTERLEAVE = 8
ATT_ROWS = NA_ROWS_PER_STEP * GRID_W
ATT_CTX_STEPS = T_CTX // ATT_ROWS
ATT_LAT_STEPS = L_LAT // ATT_ROWS
NA_WIN = NA_ROWS * GRID_W
NA_DR = 2 * NA_ROWS - 1
NA_DC = 2 * NA_COLS - 1


def _na_bias_table(rpb_ref, e2_ref):
    shape = (GRID_W, 2 * GRID_W)
    qc = _iota(shape, 0)
    lane = _iota(shape, 1)
    kc = lane % GRID_W
    upper = lane >= GRID_W
    c0 = jnp.clip(qc - NA_COLS // 2, 0, GRID_W - NA_COLS)
    dc = jnp.where((kc >= c0) & (kc < c0 + NA_COLS), kc - qc + (NA_COLS - 1), -1)

    def one(ha, carry):
        h = ha // (NA_DR - 1)
        a = ha % (NA_DR - 1)
        acc = jnp.full(shape, NEG, F32)
        for j in range(NA_DC):
            val = jnp.where(upper, rpb_ref[h * NA_DR + a + 1, j], rpb_ref[h * NA_DR + a, j])
            acc = jnp.where(dc == j, val, acc)
        e2_ref[h, a] = acc
        return carry

    lax.fori_loop(0, HEADS * (NA_DR - 1), one, 0)


def _attn_kernel(q_ref, k_ref, v_ref, kseq_ref, vseq_ref, ck_ref, cv_ref, rpb_ref, o_ref, kb_ref, vb_ref, e2_ref):
    s = pl.program_id(0)

    @pl.when(s < ATT_CTX_STEPS)
    def _():
        _interleave(*[_ctx_attn_rows(q_ref, k_ref, v_ref, o_ref, slice(i * L_CTX, (i + 1) * L_CTX))
                      for i in range(ATT_ROWS // L_CTX)])

    @pl.when(s >= ATT_CTX_STEPS)
    def _():
        step = (s - ATT_CTX_STEPS) % ATT_LAT_STEPS

        @pl.when(s == ATT_CTX_STEPS)
        def _():
            _na_bias_table(rpb_ref, e2_ref)

        @pl.when(step == 0)
        def _():
            kb_ref[...] = kseq_ref[...].astype(BF16)
            vb_ref[...] = vseq_ref[...].astype(BF16)

        _na_rows(step, q_ref, ck_ref, cv_ref, o_ref, kb_ref, vb_ref, e2_ref)


def _na_rows(step, q_ref, ck_ref, cv_ref, o_ref, kb_ref, vb_ref, e2_ref):
    ckb = ck_ref[0, 0].astype(BF16)
    cvb = cv_ref[0, 0].astype(BF16)
    scale = HD ** -0.5

    def one_row(rr):
        r = step * NA_ROWS_PER_STEP + rr
        r0 = jnp.clip(r - NA_ROWS // 2, 0, GRID_H - NA_ROWS)
        base = r0 - r + (NA_ROWS - 1)
        q0 = pl.multiple_of(rr * GRID_W, GRID_W)
        k0 = pl.multiple_of(r0 * GRID_W, GRID_W)
        qe = _expand_heads(q_ref[pl.ds(q0, GRID_W), :].astype(BF16), HD)
        kw = kb_ref[pl.ds(k0, NA_WIN), :]
        vw = vb_ref[pl.ds(k0, NA_WIN), :]
        bias = jnp.concatenate(
            [jnp.concatenate([e2_ref[h, base + 2 * p] for p in range(NA_ROWS // 2)], axis=1)
             for h in range(HEADS)], axis=0)
        yield
        s_loc = _dot_nt(qe, kw) * scale + bias
        s_ctx = _dot_nt(qe, ckb) * scale
        yield
        m = jnp.maximum(jnp.max(s_loc, axis=-1, keepdims=True), jnp.max(s_ctx, axis=-1, keepdims=True))
        p_loc = jnp.exp(s_loc - m)
        p_ctx = jnp.exp(s_ctx - m)
        inv = 1.0 / (jnp.sum(p_loc, axis=-1, keepdims=True) + jnp.sum(p_ctx, axis=-1, keepdims=True))
        yield
        pv = _dot((p_loc * inv).astype(BF16), vw) + _dot((p_ctx * inv).astype(BF16), cvb)
        yield
        o_ref[pl.ds(q0, GRID_W), :] = _extract_heads(pv, GRID_W).astype(o_ref.dtype)

    def row_group(i, carry):
        _interleave(*[one_row(NA_INTERLEAVE * i + j) for j in range(NA_INTERLEAVE)])
        return carry

    lax.fori_loop(0, NA_ROWS_PER_STEP // NA_INTERLEAVE, row_group, 0)


def _attention(parts, ck, cv, rpb, layer):
    lat_req = lambda s: jnp.maximum(s - ATT_CTX_STEPS, 0) // ATT_LAT_STEPS
    seq_blk0 = T_CTX // L_LAT
    return pl.pallas_call(
        _attn_kernel,
        grid=(T // ATT_ROWS,),
        in_specs=[
            pl.BlockSpec((ATT_ROWS, GW), lambda s: (s, CB_NAQ)),
            pl.BlockSpec((ATT_ROWS, GW), lambda s: (s, CB_NAK)),
            pl.BlockSpec((ATT_ROWS, GW), lambda s: (s, CB_NAV)),
            pl.BlockSpec((L_LAT, GW), lambda s: (seq_blk0 + lat_req(s), CB_NAK)),
            pl.BlockSpec((L_LAT, GW), lambda s: (seq_blk0 + lat_req(s), CB_NAV)),
            pl.BlockSpec((1, 1, PAST, GW), lambda s: (lat_req(s), layer, 0, 0)),
            pl.BlockSpec((1, 1, PAST, GW), lambda s: (lat_req(s), layer, 0, 0)),
            pl.BlockSpec(memory_space=pltpu.SMEM),
        ],
        out_specs=pl.BlockSpec((ATT_ROWS, GW), lambda s: (s, 0)),
        out_shape=jax.ShapeDtypeStruct((T, GW), BF16),
        scratch_shapes=[pltpu.VMEM((L_LAT, GW), BF16), pltpu.VMEM((L_LAT, GW), BF16),
                        pltpu.VMEM((HEADS, NA_DR - 1, GRID_W, 2 * GRID_W), F32)],
        compiler_params=_cparams(("arbitrary",)),
        name="attention",
    )(parts, parts, parts, parts, parts, ck, cv, rpb.reshape(HEADS * NA_DR, NA_DC))


def _rope(x, cos, sin_signed):
    lane = _iota(x.shape, 1)
    partner = jnp.where(lane % 32 < 16, pltpu.roll(x, GW - 16, 1), pltpu.roll(x, 16, 1))
    return x * cos + partner * sin_signed


def _seq_step(s):
    is_lat = s >= LAT_STEP0
    u = jnp.maximum(s - LAT_STEP0, 0)
    b = u // LAT_TILES
    t = u % LAT_TILES
    bwd = jnp.where(is_lat, LAT_STEP0 + b * LAT_TILES + (LAT_TILES - 1 - t), s)
    return is_lat, b, t, s, bwd


def _ret_kernel(qf_ref, kf_ref, vf_ref, qb_ref, kb_ref, vb_ref, cf_ref, sf_ref, cb_ref, sb_ref, lg_ref, s0_ref,
                of_ref, ob_ref, st_ref, sf_scr, sb_scr, w_scr, dq_scr, dk_scr):
    s = pl.program_id(0)
    is_lat, _, t, _, _ = _seq_step(s)
    n = SEQ_TILE
    blockdiag = _iota((GW, GW), 0) // HD == _iota((GW, GW), 1) // HD
    lg = _log_sigmoid(lg_ref[...])

    @pl.when(s == 0)
    def _():
        ti = _iota((n, GW), 0).astype(F32)
        i_ = _iota((n, n), 0).astype(F32)
        j_ = _iota((n, n), 1).astype(F32)
        for d in range(2):
            lgd = lg[d:d + 1, :]
            diff = (i_ - j_) if d == 0 else (j_ - i_)
            pos = ti if d == 0 else (n - 1.0) - ti
            w_scr[d] = jnp.concatenate(
                [jnp.where(diff >= 0, jnp.exp(jnp.maximum(diff, 0.0) * lgd[:, h * HD:h * HD + 1]), 0.0)
                 for h in range(HEADS)], axis=0)
            dq_scr[d] = jnp.exp((pos + 1.0) * lgd)
            dk_scr[d] = jnp.exp((n - 1.0 - pos) * lgd)

    @pl.when(jnp.logical_or(jnp.logical_not(is_lat), t == 0))
    def _():
        for d, scr in ((0, sf_scr), (1, sb_scr)):
            s0 = jnp.concatenate([s0_ref[0, d]] * HEADS, axis=1)
            scr[...] = jnp.where(blockdiag, s0, 0.0) * jnp.where(is_lat, 1.0, 0.0)

    def stream(d, q_ref, k_ref, v_ref, c_ref, s_ref, o_ref, scr, rope):
        q = q_ref[...] * (RET_DK ** -0.5)
        k = k_ref[...]
        if rope:
            q = _rope(q, c_ref[...], s_ref[...])
            k = _rope(k, c_ref[...], s_ref[...])
        vb16 = v_ref[...].astype(BF16)
        yield
        a = _dot_nt(_expand_heads(q.astype(BF16), HD), k.astype(BF16)) * w_scr[d]
        yield
        o = _extract_heads(_dot(a.astype(BF16), vb16), n)
        yield
        s_old = scr[...]
        o_ref[...] = o + _dot(q.astype(BF16), s_old.astype(BF16)) * dq_scr[d]
        yield
        upd = _dot_tn((k * dk_scr[d]).astype(BF16), vb16)
        scr[...] = s_old * jnp.exp(float(n) * lg[d:d + 1, :]) + jnp.where(blockdiag, upd, 0.0)

    for rope in (False, True):
        @pl.when(is_lat if rope else jnp.logical_not(is_lat))
        def _():
            _interleave(stream(0, qf_ref, kf_ref, vf_ref, cf_ref, sf_ref, of_ref, sf_scr, rope),
                        stream(1, qb_ref, kb_ref, vb_ref, cb_ref, sb_ref, ob_ref, sb_scr, rope))

    @pl.when(jnp.logical_or(jnp.logical_not(is_lat), t == LAT_TILES - 1))
    def _():
        for d, scr in ((0, sf_scr), (1, sb_scr)):
            st = scr[...]
            st_ref[0, d] = st[:, 0:HD] + st[:, HD:2 * HD] + st[:, 2 * HD:3 * HD] + st[:, 3 * HD:4 * HD]


def _retention(parts, lg_lanes, rope_tabs, s0):
    n = SEQ_TILE
    fwd = lambda s: _seq_step(s)[3]
    bwd = lambda s: _seq_step(s)[4]
    in_specs = [pl.BlockSpec((n, GW), lambda s, c=c: (fwd(s), c)) for c in (CB_RQ, CB_RK, CB_RV)]
    in_specs += [pl.BlockSpec((n, GW), lambda s, c=c: (bwd(s), c)) for c in (CB_RQ, CB_RK, CB_RV)]
    in_specs += [pl.BlockSpec((n, GW), lambda s: (_seq_step(s)[2], 0))] * 2
    in_specs += [pl.BlockSpec((n, GW), lambda s: (LAT_TILES - 1 - _seq_step(s)[2], 0))] * 2
    in_specs += [pl.BlockSpec((2, GW), lambda s: (0, 0)),
                 pl.BlockSpec((1, 2, GW, HD), lambda s: (_seq_step(s)[1], 0, 0, 0))]
    return pl.pallas_call(
        _ret_kernel,
        grid=(SEQ_STEPS,),
        in_specs=in_specs,
        out_specs=[
            pl.BlockSpec((n, GW), lambda s: (fwd(s), 0)),
            pl.BlockSpec((n, GW), lambda s: (bwd(s), 0)),
            pl.BlockSpec((1, 2, GW, HD), lambda s: (jnp.minimum(s, B_CTX), 0, 0, 0)),
        ],
        out_shape=[
            jax.ShapeDtypeStruct((T, GW), F32),
            jax.ShapeDtypeStruct((T, GW), F32),
            jax.ShapeDtypeStruct((B_CTX + 1, 2, GW, HD), F32),
        ],
        scratch_shapes=[pltpu.VMEM((GW, GW), F32), pltpu.VMEM((GW, GW), F32),
                        pltpu.VMEM((2, HEADS * n, n), F32), pltpu.VMEM((2, n, GW), F32),
                        pltpu.VMEM((2, n, GW), F32)],
        compiler_params=_cparams(("arbitrary",)),
        name="retention",
    )(parts, parts, parts, parts, parts, parts, rope_tabs[0], rope_tabs[1], rope_tabs[0], rope_tabs[1],
      lg_lanes, s0)


GLA_QK = HEADS * GLA_DK
N_SUB = GLA_CHUNK // GLA_SUB
GLA_SAFE_DECAY = 60.0


def _gla_tile(q, k, v, la, st, rev):
    n, c = SEQ_TILE, GLA_CHUNK
    n_chunks = n // c
    ri = _iota((n, n), 0)
    ci = _iota((n, n), 1)
    same_chunk = ri // c == ci // c
    causal = (ci >= ri) if rev else (ci <= ri)
    tri = jnp.where(same_chunk, jnp.where(causal, 1.0, 0.0), 0.0).astype(BF16)
    b = _dot_exact01(tri, la)
    yield

    def rows_of(idx, count):
        if idx is None:
            return jnp.zeros((count, GLA_QK), F32)
        return jnp.broadcast_to(b[idx:idx + 1, :], (count, GLA_QK))

    def ref_row(cc, s):
        if rev:
            return cc * c + (s + 1) * GLA_SUB if s < N_SUB - 1 else None
        return cc * c + s * GLA_SUB - 1 if s > 0 else None

    end_rows = [cc * c if rev else cc * c + c - 1 for cc in range(n_chunks)]
    own_ref = jnp.concatenate([rows_of(ref_row(cc, s), GLA_SUB) for cc in range(n_chunks) for s in range(N_SUB)],
                              axis=0)
    b_end = jnp.concatenate([rows_of(r, c) for r in end_rows], axis=0)
    sub = (_iota((n, GLA_QK), 0) // GLA_SUB) % N_SUB
    qh = q * jnp.exp(b - own_ref)
    q_parts, k_parts = [], []
    for s in range(N_SUB):
        kvalid = (sub >= s) if rev else (sub <= s)
        ref_s = jnp.concatenate([rows_of(ref_row(cc, s), c) for cc in range(n_chunks)], axis=0)
        q_parts.append(jnp.where(sub == s, qh, 0.0))
        k_parts.append(jnp.where(kvalid, k * jnp.exp(jnp.minimum(ref_s - b, GLA_SAFE_DECAY)), 0.0))
    q_cat = _expand_heads(jnp.concatenate(q_parts, axis=1).astype(BF16), GLA_DK)
    k_cat = jnp.concatenate(k_parts, axis=1).astype(BF16)
    yield
    a = _dot_nt(q_cat, k_cat)
    yield
    qi = _iota((HEADS * n, n), 0) % n
    kj = _iota((HEADS * n, n), 1)
    keep = (qi // c == kj // c) & ((kj >= qi) if rev else (kj <= qi))
    vb16 = v.astype(BF16)
    o_intra = _extract_heads(_dot(jnp.where(keep, a, 0.0).astype(BF16), vb16), n)
    yield

    qt = (q * jnp.exp(b)).astype(BF16)
    kt = (k * jnp.exp(b_end - b)).astype(BF16)
    blockdiag = _iota((GW, GLA_QK), 0) // HD == _iota((GW, GLA_QK), 1) // GLA_DK
    upd = [jnp.where(blockdiag, _dot_tn(vb16[cc * c:(cc + 1) * c], kt[cc * c:(cc + 1) * c]), 0.0)
           for cc in range(n_chunks)]
    yield
    o_inter = [None] * n_chunks
    for cc in (reversed(range(n_chunks)) if rev else range(n_chunks)):
        o_inter[cc] = _dot_nt(qt[cc * c:(cc + 1) * c], st.astype(BF16))
        st = st * jnp.exp(b[end_rows[cc]:end_rows[cc] + 1, :]) + upd[cc]
        yield
    return o_intra + jnp.concatenate(o_inter, axis=0), st


def _gla_chunk(q, k, v, la, st, rev):
    c = GLA_CHUNK
    ri = _iota((c, c), 0)
    ci = _iota((c, c), 1)
    tri = jnp.where((ci >= ri) if rev else (ci <= ri), 1.0, 0.0).astype(BF16)
    b = _dot_exact01(tri, la)
    b_end = b[0:1, :] if rev else b[c - 1:c, :]
    row = _iota((c, GLA_QK), 0)
    sub = row // GLA_SUB
    off = row % GLA_SUB

    o = _dot_nt((q * jnp.exp(b)).astype(BF16), st.astype(BF16))
    kt = (k * jnp.exp(b_end - b)).astype(BF16)
    lane_h = _iota((GW, GLA_QK), 1) // GLA_DK
    row_h = _iota((GW, GLA_QK), 0) // HD
    st_new = st * jnp.exp(b_end) + jnp.where(row_h == lane_h, _dot_tn(v.astype(BF16), kt), 0.0)

    q_parts, k_parts = [], []
    for s in range(1, N_SUB):
        if rev:
            qsub, brow = N_SUB - 1 - s, b[(N_SUB - s) * GLA_SUB:(N_SUB - s) * GLA_SUB + 1, :]
            kvalid = sub > qsub
        else:
            qsub, brow = s, b[s * GLA_SUB - 1:s * GLA_SUB, :]
            kvalid = sub < qsub
        q_parts.append(jnp.where(sub == qsub, q * jnp.exp(jnp.where(sub == qsub, b - brow, 0.0)), 0.0))
        k_parts.append(jnp.where(kvalid, k * jnp.exp(jnp.where(kvalid, brow - b, 0.0)), 0.0))
    q_cat = _expand_heads(jnp.concatenate(q_parts, axis=1).astype(BF16), GLA_DK)
    k_cat = jnp.concatenate(k_parts, axis=1).astype(BF16)
    a_off = _dot_nt(q_cat, k_cat)
    o = o + _extract_heads(_dot(a_off.astype(BF16), v.astype(BF16)), c)

    red = jnp.where(_iota((GLA_QK, GW), 0) // GLA_DK == _iota((GLA_QK, GW), 1) // HD, 1.0, 0.0).astype(BF16)
    rowv = _iota((c, GW), 0) % GLA_SUB
    for dl in range(GLA_SUB):
        if dl == 0:
            x = q * k
            vs = v
        else:
            sh = dl if not rev else c - dl
            valid = (off + dl < GLA_SUB) if rev else (off >= dl)
            ks = pltpu.roll(k, sh, 0)
            bs = pltpu.roll(b, sh, 0)
            vs = pltpu.roll(v, sh, 0)
            x = jnp.where(valid, q * ks * jnp.exp(jnp.where(valid, b - bs, 0.0)), 0.0)
            validv = (rowv + dl < GLA_SUB) if rev else (rowv >= dl)
            vs = jnp.where(validv, vs, 0.0)
        o = o + _dot(x.astype(BF16), red) * vs
    return o, st_new


def _gla_kernel(qkf_ref, vf_ref, lrf_ref, qkb_ref, vb_ref, lrb_ref, gu_ref, gb_ref, s0_ref,
                of_ref, ob_ref, st_ref, sf_scr, sb_scr):
    is_lat, _, t, _, _ = _seq_step(pl.program_id(0))
    lane_h = _iota((GW, GLA_QK), 1) // GLA_DK
    row_h = _iota((GW, GLA_QK), 0) // HD
    blockdiag = row_h == lane_h

    @pl.when(jnp.logical_or(jnp.logical_not(is_lat), t == 0))
    def _():
        for d, scr in ((0, sf_scr), (1, sb_scr)):
            s0t = jnp.concatenate([s0_ref[0, d].T] * HEADS, axis=0)
            scr[...] = jnp.where(blockdiag, s0t, 0.0) * jnp.where(is_lat, 1.0, 0.0)

    c = GLA_CHUNK
    n_chunks = SEQ_TILE // c

    def log_gate(d, lr_ref):
        z = _dot(lr_ref[...].astype(BF16), gu_ref[d].astype(BF16)) + gb_ref[d]
        return _log_sigmoid(z) / GLA_TAU

    def tile_stream(d, qk_ref, v_ref, la, scr):
        return _gla_tile(qk_ref[:, 0:GLA_QK] * (GLA_DK ** -0.5), qk_ref[:, GLA_QK:2 * GLA_QK],
                         v_ref[...], la, scr[...], rev=(d == 1))

    def chunk_stream(d, qk_ref, v_ref, la, o_ref, scr):
        st = scr[...]
        for cc in (range(n_chunks) if d == 0 else reversed(range(n_chunks))):
            rows = slice(cc * c, (cc + 1) * c)
            q = qk_ref[rows, 0:GLA_QK] * (GLA_DK ** -0.5)
            k = qk_ref[rows, GLA_QK:2 * GLA_QK]
            o, st = _gla_chunk(q, k, v_ref[rows, :], la[rows, :], st, rev=(d == 1))
            o_ref[rows, :] = o
        scr[...] = st

    la_f = log_gate(0, lrf_ref)
    la_b = log_gate(1, lrb_ref)
    decay = GLA_SUB * jnp.max(jnp.maximum(-la_f, -la_b))

    @pl.when(decay <= GLA_SAFE_DECAY)
    def _():
        (o_f, st_f), (o_b, st_b) = _interleave(tile_stream(0, qkf_ref, vf_ref, la_f, sf_scr),
                                               tile_stream(1, qkb_ref, vb_ref, la_b, sb_scr))
        of_ref[...] = o_f
        ob_ref[...] = o_b
        sf_scr[...] = st_f
        sb_scr[...] = st_b

    @pl.when(decay > GLA_SAFE_DECAY)
    def _():
        chunk_stream(0, qkf_ref, vf_ref, la_f, of_ref, sf_scr)
        chunk_stream(1, qkb_ref, vb_ref, la_b, ob_ref, sb_scr)

    @pl.when(jnp.logical_or(jnp.logical_not(is_lat), t == LAT_TILES - 1))
    def _():
        for d, scr in ((0, sf_scr), (1, sb_scr)):
            s = scr[...].T
            st_ref[0, d] = s[:, 0:HD] + s[:, HD:2 * HD] + s[:, 2 * HD:3 * HD] + s[:, 3 * HD:4 * HD]


def _gla(parts, gate_up_pad, gate_b, s0):
    n = SEQ_TILE
    in_specs = []
    for blk in (3, 4):
        in_specs += [
            pl.BlockSpec((n, GW), lambda s, blk=blk: (_seq_step(s)[blk], CB_AQK)),
            pl.BlockSpec((n, GW), lambda s, blk=blk: (_seq_step(s)[blk], CB_AV)),
            pl.BlockSpec((n, 128), lambda s, blk=blk: (_seq_step(s)[blk], CB_LR128)),
        ]
    in_specs += [pl.BlockSpec((2, 128, GLA_QK), lambda s: (0, 0, 0)),
                 pl.BlockSpec((2, 1, GLA_QK), lambda s: (0, 0, 0)),
                 pl.BlockSpec((1, 2, GLA_QK, HD), lambda s: (_seq_step(s)[1], 0, 0, 0))]
    return pl.pallas_call(
        _gla_kernel,
        grid=(SEQ_STEPS,),
        in_specs=in_specs,
        out_specs=[
            pl.BlockSpec((n, GW), lambda s: (_seq_step(s)[3], 0)),
            pl.BlockSpec((n, GW), lambda s: (_seq_step(s)[4], 0)),
            pl.BlockSpec((1, 2, GLA_QK, HD), lambda s: (jnp.minimum(s, B_CTX), 0, 0, 0)),
        ],
        out_shape=[
            jax.ShapeDtypeStruct((T, GW), F32),
            jax.ShapeDtypeStruct((T, GW), F32),
            jax.ShapeDtypeStruct((B_CTX + 1, 2, GLA_QK, HD), F32),
        ],
        scratch_shapes=[pltpu.VMEM((GW, GLA_QK), F32), pltpu.VMEM((GW, GLA_QK), F32)],
        compiler_params=_cparams(("arbitrary",)),
        name="gla",
    )(parts, parts, parts, parts, parts, parts, gate_up_pad, gate_b, s0)


OUT_TM = 512
FFN_TF = 1408
FFN_PASSES = D_FF // FFN_TF
OUT_TILES = T // OUT_TM


def _run_in_order(order):
    results = {}
    for gen in order:
        try:
            next(gen)
        except StopIteration as stop:
            results[gen] = stop.value
    return results


def _outproj_ffn_kernel(*refs, n_x, n_out):
    x_refs = refs[:n_x]
    (mod_ref, modp_ref, gpost_ref, gpre2_ref, gpost2_ref, pool_ref, na_ref, rf_ref, rb_ref, rg_ref, af_ref, ab_ref,
     ag_ref, ng_ref, w_ref, wg_ref, wu_ref, wd_ref) = refs[n_x:n_x + 18]
    o_refs = refs[n_x + 18:n_x + 18 + n_out]
    x1_scr, hb_scr = refs[n_x + 18 + n_out:]
    i = pl.program_id(0)
    wslot = i % 2
    rslot = 1 - wslot

    @pl.when(i == 0)
    def _():
        x1_scr[1] = jnp.zeros((OUT_TM, D), F32)
        hb_scr[1] = jnp.zeros((OUT_TM, D), BF16)

    avg = jnp.where(_iota((GW, GW), 0) // HD == _iota((GW, GW), 1) // HD, 1.0 / HD, 0.0).astype(BF16)
    x_tile = _x_tile(x_refs, OUT_TM, jnp.minimum(i, OUT_TILES - 1))

    def mixer_tail(rows):
        r = rf_ref[rows, :] + rb_ref[rows, :]
        r = r - _head_mean(r, avg)
        r = r * lax.rsqrt(_head_mean(r * r, avg) + GN_EPS) * _silu(rg_ref[rows, :])
        a = af_ref[rows, :] + ab_ref[rows, :]
        a = a * lax.rsqrt(_head_mean(a * a, avg) + RMS_EPS) * ng_ref[...] * _silu(ag_ref[rows, :])
        yield
        y = _dot(pool_ref[rows, :].astype(BF16), w_ref[0, 0:GW, :])
        y = y + _dot(na_ref[rows, :].astype(BF16), w_ref[0, GW:2 * GW, :])
        y = y + _dot(r.astype(BF16), w_ref[0, 2 * GW:3 * GW, :])
        y = y + _dot(a.astype(BF16), w_ref[0, 3 * GW:4 * GW, :])
        yield
        x1 = x_tile[rows] + mod_ref[0, 2:3, :] * _rms(y, gpost_ref[...])
        x1_scr[wslot, rows, :] = x1
        hb_scr[wslot, rows, :] = (_rms(x1, gpre2_ref[...]) * (1.0 + mod_ref[0, 4:5, :])
                                  + mod_ref[0, 3:4, :]).astype(BF16)

    def ffn(rows):
        hb = hb_scr[rslot, rows, :]
        y = None
        for j in range(FFN_PASSES):
            cols = slice(j * FFN_TF, (j + 1) * FFN_TF)
            act = (_silu(_dot(hb, wg_ref[0, :, cols])) * _dot(hb, wu_ref[0, :, cols])).astype(BF16)
            yield
            part = _dot(act, wd_ref[0, cols, :])
            y = part if y is None else y + part
            yield
        return x1_scr[rslot, rows, :] + modp_ref[0, 5:6, :] * _rms(y, gpost2_ref[...])

    half = OUT_TM // 2
    fa, fb = ffn(slice(0, half)), ffn(slice(half, OUT_TM))
    ta, tb = mixer_tail(slice(0, half)), mixer_tail(slice(half, OUT_TM))
    done = _run_in_order([fa, ta, fa, ta, fa, ta, fa, fb, tb, fa, fb, tb, fb, tb, fb, fb])
    x2 = jnp.concatenate([done[fa], done[fb]], axis=0)
    if len(o_refs) == 1:
        o_refs[0][...] = x2
    else:
        is_ctx = i - 1 < T_CTX // OUT_TM

        @pl.when(is_ctx)
        def _():
            o_refs[0][...] = x2

        @pl.when(jnp.logical_not(is_ctx))
        def _():
            o_refs[1][...] = x2


def _outproj_ffn(x, mod_l, g_post, g_pre2, g_post2, o_pool, o_na, ret_f, ret_b, gla_f, gla_b, parts, ng_lanes,
                 w_out_b, wg, wu, wd, layer, split_out):
    tm = OUT_TM
    cur = lambda i: jnp.minimum(i, OUT_TILES - 1)
    prev = lambda i: jnp.maximum(i - 1, 0)
    act = pl.BlockSpec((tm, GW), lambda i: (cur(i), 0))
    vec = pl.BlockSpec((1, D), lambda i: (0, 0))
    once = pl.Buffered(1)
    x_specs, x_args = _x_specs(x, tm, cur)
    if split_out:
        out_specs, _ = _x_specs((None, None), tm, prev)
        out_shape = [jax.ShapeDtypeStruct((T_CTX, D), F32), jax.ShapeDtypeStruct((T_LAT, D), F32)]
    else:
        out_specs = [pl.BlockSpec((tm, D), lambda i: (prev(i), 0))]
        out_shape = [jax.ShapeDtypeStruct((T, D), F32)]
    out = pl.pallas_call(
        functools.partial(_outproj_ffn_kernel, n_x=len(x_args), n_out=len(out_shape)),
        grid=(OUT_TILES + 1,),
        in_specs=x_specs + [
            pl.BlockSpec((1, 6, D), lambda i: (_mod_row(cur(i), tm), 0, 0)),
            pl.BlockSpec((1, 6, D), lambda i: (_mod_row(prev(i), tm), 0, 0)),
            vec, vec, vec,
            act, act, act, act,
            pl.BlockSpec((tm, GW), lambda i: (cur(i), CB_RG)),
            act, act,
            pl.BlockSpec((tm, GW), lambda i: (cur(i), CB_AG)),
            pl.BlockSpec((1, GW), lambda i: (0, 0)),
            pl.BlockSpec((1, D, D), lambda i: (layer, 0, 0), pipeline_mode=once),
            pl.BlockSpec((1, D, D_FF), lambda i: (layer, 0, 0), pipeline_mode=once),
            pl.BlockSpec((1, D, D_FF), lambda i: (layer, 0, 0), pipeline_mode=once),
            pl.BlockSpec((1, D_FF, D), lambda i: (layer, 0, 0), pipeline_mode=once),
        ],
        out_specs=out_specs,
        out_shape=out_shape,
        scratch_shapes=[pltpu.VMEM((2, tm, D), F32), pltpu.VMEM((2, tm, D), BF16)],
        compiler_params=_cparams(("arbitrary",)),
        name="outproj_ffn",
    )(*x_args, mod_l, mod_l, g_post, g_pre2, g_post2, o_pool, o_na, ret_f, ret_b, parts, gla_f, gla_b, parts, ng_lanes,
      w_out_b, wg, wu, wd)
    return tuple(out) if split_out else out[0]


def _rope_tables():
    nf = 16
    inv = (ROPE_BASE ** (-np.arange(nf, dtype=np.float32) / nf)).astype(np.float32)
    tok = np.arange(L_LAT)
    cos = np.zeros((L_LAT, HD), np.float32)
    sin = np.zeros((L_LAT, HD), np.float32)
    for axis, pos in enumerate((tok // GRID_W, tok % GRID_W)):
        ang = pos.astype(np.float32)[:, None] * inv[None, :]
        c, s = np.cos(ang), np.sin(ang)
        cos[:, axis * 32:axis * 32 + 32] = np.concatenate([c, c], axis=1)
        sin[:, axis * 32:axis * 32 + 32] = np.concatenate([-s, s], axis=1)
    return jnp.asarray(np.tile(cos, (1, HEADS))), jnp.asarray(np.tile(sin, (1, HEADS)))


def _block_diag(w):
    g, c, _ = w.shape
    out = jnp.zeros((g * c, g * c), w.dtype)
    for i in range(g):
        out = out.at[i * c:(i + 1) * c, i * c:(i + 1) * c].set(w[i])
    return out


def kernel(x_prompt, x_sample, cache_na_k, cache_na_v, state_ret, state_gla, c, c_ctx, w_mod, b_mod,
           g_pre_mix, g_post_mix, g_pre_ffn, g_post_ffn, w_in, w_out, pool_w, pool_scale, na_rpb,
           ret_decay_logit, gla_gate_up, gla_gate_b, gla_norm_g, w_ffn_gate, w_ffn_up, w_ffn_down):
    x = (x_prompt.reshape(T_CTX, D), x_sample.reshape(T_LAT, D))
    cv8 = jnp.concatenate([c_ctx[None, :], c, jnp.zeros((8 - 1 - B_LAT, D), F32)], axis=0)
    mods = _modulation(cv8, w_mod, b_mod).reshape(DEPTH, 8, 6, D)

    w_out_b = w_out.astype(BF16)
    wg_b, wu_b, wd_b = w_ffn_gate.astype(BF16), w_ffn_up.astype(BF16), w_ffn_down.astype(BF16)
    gate_up_pad = jnp.pad(gla_gate_up, ((0, 0), (0, 0), (0, 128 - GLA_LOWRANK), (0, 0)))
    rope_tabs = _rope_tables()
    ck = cache_na_k.reshape(B_LAT, DEPTH, PAST, GW)
    cv = cache_na_v.reshape(B_LAT, DEPTH, PAST, GW)
    s0_ret = state_ret.reshape(B_LAT, DEPTH, 2, GW, HD)
    s0_gla = state_gla.reshape(B_LAT, DEPTH, 2, GLA_QK, HD)

    ks, vs, srs, sgs = [], [], [], []
    for l in range(DEPTH):
        mod_l = mods[l]
        parts = _inproj(x, mod_l, g_pre_mix[l][None, :], w_in, l)
        o_pool = _pool(parts, _block_diag(pool_w[l]).astype(BF16), pool_scale[l][None, :])
        o_na = _attention(parts, ck, cv, na_rpb[l], l)
        lg_lanes = jnp.repeat(ret_decay_logit[l], HD, axis=1)
        rf, rb, s_ret = _retention(parts, lg_lanes, rope_tabs, s0_ret[:, l])
        gf, gbw, s_gla = _gla(parts, gate_up_pad[l], gla_gate_b[l][:, None, :], s0_gla[:, l])
        x = _outproj_ffn(x, mod_l, g_post_mix[l][None, :], g_pre_ffn[l][None, :], g_post_ffn[l][None, :],
                         o_pool, o_na, rf, rb, gf, gbw, parts, jnp.tile(gla_norm_g[l], HEADS)[None, :],
                         w_out_b, wg_b, wu_b, wd_b, l, split_out=(l == DEPTH - 1))
        ks.append(parts[:T_CTX, CB_NAK * GW:(CB_NAK + 1) * GW].reshape(B_CTX, L_CTX, HEADS, HD))
        vs.append(parts[:T_CTX, CB_NAV * GW:(CB_NAV + 1) * GW].reshape(B_CTX, L_CTX, HEADS, HD))
        srs.append(s_ret[:B_CTX].reshape(B_CTX, 2, HEADS, RET_DK, HD))
        sgs.append(s_gla[:B_CTX].reshape(B_CTX, 2, HEADS, GLA_DK, HD))

    return (x[0].reshape(B_CTX, L_CTX, D), x[1].reshape(B_LAT, L_LAT, D),
            jnp.stack(ks, axis=1), jnp.stack(vs, axis=1), jnp.stack(srs, axis=1), jnp.stack(sgs, axis=1))
```

```python
import functools

import numpy as np
import jax
import jax.numpy as jnp
from jax import lax
from jax.experimental import pallas as pl
from jax.experimental.pallas import tpu as pltpu

F32 = jnp.float32
BF16 = jnp.bfloat16

D = 1024
B_CTX, L_CTX = 32, 256
B_LAT, L_LAT = 2, 4096
DEPTH = 4
PAST = 256
GRID_W = 64
GRID_H = L_LAT // GRID_W
T_CTX = B_CTX * L_CTX
T_LAT = B_LAT * L_LAT
T = T_CTX + T_LAT
GW = 256
HEADS = 4
HD = 64
POOL_WINDOWS = (2, 4, 8, 16)
NA_ROWS, NA_COLS = 8, 16
RET_DK = 64
GLA_DK = 32
GLA_LOWRANK = 16
GLA_TAU = 16.0
D_FF = 2816
P_IN = 2832
P_PAD = 2944
ROPE_BASE = 10000.0
RMS_EPS = 1e-6
GN_EPS = 1e-5
NEG = -1e30

CB_POOL, CB_NAQ, CB_NAK, CB_NAV, CB_RQ, CB_RK, CB_RV, CB_RG, CB_AQK, CB_AV, CB_AG = range(11)
CB_LR128 = P_IN // 128

SEQ_TILE = L_CTX
SEQ_STEPS = T // SEQ_TILE
LAT_STEP0 = T_CTX // SEQ_TILE
LAT_TILES = L_LAT // SEQ_TILE
GLA_CHUNK = 64
GLA_SUB = 16
VMEM_LIMIT = 56 * 1024 * 1024


def _cparams(sem):
    return pltpu.CompilerParams(dimension_semantics=sem, vmem_limit_bytes=VMEM_LIMIT)


def _silu(x):
    return x / (1.0 + jnp.exp(-x))


def _log_sigmoid(z):
    return jnp.minimum(z, 0.0) - jnp.log1p(jnp.exp(-jnp.abs(z)))


def _rms(x, g):
    return x * lax.rsqrt(jnp.mean(x * x, axis=-1, keepdims=True) + RMS_EPS) * g


def _dot(a, b):
    return jnp.dot(a, b, preferred_element_type=F32)


def _dot_nt(a, b):
    return lax.dot_general(a, b, (((1,), (1,)), ((), ())), preferred_element_type=F32)


def _dot_tn(a, b):
    return lax.dot_general(a, b, (((0,), (0,)), ((), ())), preferred_element_type=F32)


def _split_hi_lo(x):
    hi = x.astype(BF16)
    return hi, (x - hi.astype(F32)).astype(BF16)


def _dot_exact01(a01, x):
    hi, lo = _split_hi_lo(x)
    return _dot(a01, hi) + _dot(a01, lo)


def _iota(shape, dim):
    return lax.broadcasted_iota(jnp.int32, shape, dim)


def _expand_heads(x, head_w):
    n, w = x.shape
    xe = jnp.concatenate([x] * HEADS, axis=0)
    rowh = _iota((HEADS * n, w), 0) // n
    laneh = (_iota((HEADS * n, w), 1) // head_w) % HEADS
    return jnp.where(rowh == laneh, xe, jnp.zeros_like(xe))


def _extract_heads(p, n):
    laneh = _iota((n, GW), 1) // HD
    out = p[0:n]
    for h in range(1, HEADS):
        out = jnp.where(laneh == h, p[h * n:(h + 1) * n], out)
    return out


def _head_mean(x, avg):
    hi, lo = _split_hi_lo(x)
    return _dot(hi, avg) + _dot(lo, avg)


def _interleave(*gens):
    results = [None] * len(gens)
    live = list(range(len(gens)))
    while live:
        for i in list(live):
            try:
                next(gens[i])
            except StopIteration as stop:
                results[i] = stop.value
                live.remove(i)
    return results


def _mod_row(i, tm):
    return jnp.where(i < T_CTX // tm, 0, 1 + (i * tm - T_CTX) // L_LAT)


def _mod_kernel(cv_ref, w_ref, b_ref, o_ref):
    s = _silu(cv_ref[...]).astype(BF16)
    o_ref[0] = _dot(s, w_ref[0].astype(BF16)) + b_ref[0]


def _modulation(cv8, w_mod, b_mod):
    tn = 1536
    return pl.pallas_call(
        _mod_kernel,
        grid=(DEPTH, 6 * D // tn),
        in_specs=[
            pl.BlockSpec((8, D), lambda l, j: (0, 0)),
            pl.BlockSpec((1, D, tn), lambda l, j: (l, 0, j)),
            pl.BlockSpec((1, 1, tn), lambda l, j: (l, 0, j)),
        ],
        out_specs=pl.BlockSpec((1, 8, tn), lambda l, j: (l, 0, j)),
        out_shape=jax.ShapeDtypeStruct((DEPTH, 8, 6 * D), F32),
        compiler_params=_cparams(("arbitrary", "arbitrary")),
        name="modulation",
    )(cv8, w_mod, b_mod.reshape(DEPTH, 1, 6 * D))


IN_TM = 512


def _x_specs(x, tm, tile_of=lambda i: i):
    if not isinstance(x, tuple):
        return [pl.BlockSpec((tm, D), lambda i: (tile_of(i), 0))], [x]
    nc = T_CTX // tm
    return ([pl.BlockSpec((tm, D), lambda i: (jnp.minimum(tile_of(i), nc - 1), 0)),
             pl.BlockSpec((tm, D), lambda i: (jnp.maximum(tile_of(i) - nc, 0), 0))], list(x))


def _x_tile(x_refs, tm, tile):
    if len(x_refs) == 1:
        return x_refs[0][...]
    return jnp.where(tile < T_CTX // tm, x_refs[0][...], x_refs[1][...])


def _inproj_kernel(*refs, n_x):
    x_refs, (mod_ref, g_ref, w_ref, o_ref) = refs[:n_x], refs[n_x:]
    x = _x_tile(x_refs, IN_TM, pl.program_id(0))

    def half_tile(rows):
        h = _rms(x[rows], g_ref[...]) * (1.0 + mod_ref[0, 1:2, :]) + mod_ref[0, 0:1, :]
        hb = h.astype(BF16)
        yield
        for a in range(0, P_PAD, 1024):
            b = min(a + 1024, P_PAD)
            o_ref[rows, a:b] = _dot(hb, w_ref[0, :, a:b])
            yield

    half = IN_TM // 2
    first, second = half_tile(slice(0, half)), half_tile(slice(half, IN_TM))
    next(first)
    _interleave(first, second)


def _inproj(x, mod_l, g_pre, w_in_b, layer):
    tm = IN_TM
    x_specs, x_args = _x_specs(x, tm)
    return pl.pallas_call(
        functools.partial(_inproj_kernel, n_x=len(x_args)),
        grid=(T // tm,),
        in_specs=x_specs + [
            pl.BlockSpec((1, 6, D), lambda i: (_mod_row(i, tm), 0, 0)),
            pl.BlockSpec((1, D), lambda i: (0, 0)),
            pl.BlockSpec((1, D, P_PAD), lambda i: (layer, 0, 0), pipeline_mode=pl.Buffered(1)),
        ],
        out_specs=pl.BlockSpec((tm, P_PAD), lambda i: (i, 0)),
        out_shape=jax.ShapeDtypeStruct((T, P_PAD), F32),
        compiler_params=_cparams(("arbitrary",)),
        name="inproj",
    )(*x_args, mod_l, g_pre, w_in_b)


POOL_TM = 512
POOL_SUB = 256


def _pool_kernel(v_ref, w_ref, scale_ref, o_ref, band_scr, cnt_scr):
    i = pl.program_id(0)
    n = POOL_SUB

    @pl.when(i == 0)
    def _():
        t = _iota((n, n), 0)
        s = _iota((n, n), 1)
        lane_g = _iota((n, GW), 1) // HD
        for kind, seg_len in enumerate((L_CTX, GRID_W)):
            seg0 = t & ~(seg_len - 1)
            seg1 = seg0 + seg_len
            cnt = jnp.zeros((n, GW), F32)
            for gi, win in enumerate(POOL_WINDOWS):
                lo = jnp.maximum(t - win // 2, seg0)
                hi = jnp.minimum(t - win // 2 + win, seg1)
                band_scr[kind, gi] = jnp.where(s >= lo, jnp.where(s < hi, 1.0, 0.0), 0.0).astype(BF16)
                cnt = jnp.where(lane_g == gi, (hi - lo).astype(F32), cnt)
            cnt_scr[kind] = cnt

    kind = jnp.where(i < T_CTX // POOL_TM, 0, 1)

    def piece(rows):
        v = v_ref[rows, :]
        vh, vl = _split_hi_lo(v)
        lane_g = _iota((n, GW), 1) // HD
        yield
        mean = None
        for gi in range(len(POOL_WINDOWS)):
            band = band_scr[kind, gi]
            m = _dot(band, vh) + _dot(band, vl)
            mean = m if mean is None else jnp.where(lane_g == gi, m, mean)
        yield
        d = (mean / cnt_scr[kind] - v).astype(BF16)
        o_ref[rows, :] = (_dot(d, w_ref[...]) * scale_ref[...]).astype(o_ref.dtype)

    _interleave(*[piece(slice(j * n, (j + 1) * n)) for j in range(POOL_TM // n)])


def _pool(parts, w_bd, scale):
    tm = POOL_TM
    n_win = len(POOL_WINDOWS)
    return pl.pallas_call(
        _pool_kernel,
        grid=(T // tm,),
        in_specs=[
            pl.BlockSpec((tm, GW), lambda i: (i, CB_POOL)),
            pl.BlockSpec((GW, GW), lambda i: (0, 0)),
            pl.BlockSpec((1, GW), lambda i: (0, 0)),
        ],
        out_specs=pl.BlockSpec((tm, GW), lambda i: (i, 0)),
        out_shape=jax.ShapeDtypeStruct((T, GW), BF16),
        scratch_shapes=[pltpu.VMEM((2, n_win, POOL_SUB, POOL_SUB), BF16), pltpu.VMEM((2, POOL_SUB, GW), F32)],
        compiler_params=_cparams(("arbitrary",)),
        name="pool",
    )(parts, w_bd, scale)


def _softmax_rows(s):
    m = jnp.max(s, axis=-1, keepdims=True)
    p = jnp.exp(s - m)
    return p / jnp.sum(p, axis=-1, keepdims=True)


def _ctx_attn_rows(q_ref, k_ref, v_ref, o_ref, rows):
    qe = _expand_heads(q_ref[rows, :].astype(BF16), HD)
    yield
    s = _dot_nt(qe, k_ref[rows, :].astype(BF16)) * (HD ** -0.5)
    yield
    p = _softmax_rows(s).astype(BF16)
    yield
    o_ref[rows, :] = _extract_heads(_dot(p, v_ref[rows, :].astype(BF16)), L_CTX).astype(o_ref.dtype)


NA_ROWS_PER_STEP = 8
NA_INTERLEAVE = 8
ATT_ROWS = NA_ROWS_PER_STEP * GRID_W
ATT_CTX_STEPS = T_CTX // ATT_ROWS
ATT_LAT_STEPS = L_LAT // ATT_ROWS
NA_WIN = NA_ROWS * GRID_W
NA_DR = 2 * NA_ROWS - 1
NA_DC = 2 * NA_COLS - 1


def _na_bias_table(rpb_ref, e2_ref):
    shape = (GRID_W, 2 * GRID_W)
    qc = _iota(shape, 0)
    lane = _iota(shape, 1)
    kc = lane % GRID_W
    upper = lane >= GRID_W
    c0 = jnp.clip(qc - NA_COLS // 2, 0, GRID_W - NA_COLS)
    dc = jnp.where((kc >= c0) & (kc < c0 + NA_COLS), kc - qc + (NA_COLS - 1), -1)

    def one(ha, carry):
        h = ha // (NA_DR - 1)
        a = ha % (NA_DR - 1)
        acc = jnp.full(shape, NEG, F32)
        for j in range(NA_DC):
            val = jnp.where(upper, rpb_ref[h * NA_DR + a + 1, j], rpb_ref[h * NA_DR + a, j])
            acc = jnp.where(dc == j, val, acc)
        e2_ref[h, a] = acc
        return carry

    lax.fori_loop(0, HEADS * (NA_DR - 1), one, 0)


def _attn_kernel(q_ref, k_ref, v_ref, kseq_ref, vseq_ref, ck_ref, cv_ref, rpb_ref, o_ref, ko_ref, vo_ref,
                 kb_ref, vb_ref, e2_ref):
    s = pl.program_id(0)

    @pl.when(s < ATT_CTX_STEPS)
    def _():
        ko_ref[...] = k_ref[...]
        vo_ref[...] = v_ref[...]
        _interleave(*[_ctx_attn_rows(q_ref, k_ref, v_ref, o_ref, slice(i * L_CTX, (i + 1) * L_CTX))
                      for i in range(ATT_ROWS // L_CTX)])

    @pl.when(s >= ATT_CTX_STEPS)
    def _():
        step = (s - ATT_CTX_STEPS) % ATT_LAT_STEPS

        @pl.when(s == ATT_CTX_STEPS)
        def _():
            _na_bias_table(rpb_ref, e2_ref)

        @pl.when(step == 0)
        def _():
            kb_ref[...] = kseq_ref[...].astype(BF16)
            vb_ref[...] = vseq_ref[...].astype(BF16)

        _na_rows(step, q_ref, ck_ref, cv_ref, o_ref, kb_ref, vb_ref, e2_ref)


def _na_rows(step, q_ref, ck_ref, cv_ref, o_ref, kb_ref, vb_ref, e2_ref):
    ckb = ck_ref[0, 0].astype(BF16)
    cvb = cv_ref[0, 0].astype(BF16)
    scale = HD ** -0.5

    def one_row(rr):
        r = step * NA_ROWS_PER_STEP + rr
        r0 = jnp.clip(r - NA_ROWS // 2, 0, GRID_H - NA_ROWS)
        base = r0 - r + (NA_ROWS - 1)
        q0 = pl.multiple_of(rr * GRID_W, GRID_W)
        k0 = pl.multiple_of(r0 * GRID_W, GRID_W)
        qe = _expand_heads(q_ref[pl.ds(q0, GRID_W), :].astype(BF16), HD)
        kw = kb_ref[pl.ds(k0, NA_WIN), :]
        vw = vb_ref[pl.ds(k0, NA_WIN), :]
        bias = jnp.concatenate(
            [jnp.concatenate([e2_ref[h, base + 2 * p] for p in range(NA_ROWS // 2)], axis=1)
             for h in range(HEADS)], axis=0)
        yield
        s_loc = _dot_nt(qe, kw) * scale + bias
        s_ctx = _dot_nt(qe, ckb) * scale
        yield
        m = jnp.maximum(jnp.max(s_loc, axis=-1, keepdims=True), jnp.max(s_ctx, axis=-1, keepdims=True))
        p_loc = jnp.exp(s_loc - m)
        p_ctx = jnp.exp(s_ctx - m)
        inv = 1.0 / (jnp.sum(p_loc, axis=-1, keepdims=True) + jnp.sum(p_ctx, axis=-1, keepdims=True))
        yield
        pv = _dot((p_loc * inv).astype(BF16), vw) + _dot((p_ctx * inv).astype(BF16), cvb)
        yield
        o_ref[pl.ds(q0, GRID_W), :] = _extract_heads(pv, GRID_W).astype(o_ref.dtype)

    def row_group(i, carry):
        _interleave(*[one_row(NA_INTERLEAVE * i + j) for j in range(NA_INTERLEAVE)])
        return carry

    lax.fori_loop(0, NA_ROWS_PER_STEP // NA_INTERLEAVE, row_group, 0)


def _attention(parts, ck, cv, rpb, layer):
    lat_req = lambda s: jnp.maximum(s - ATT_CTX_STEPS, 0) // ATT_LAT_STEPS
    seq_blk0 = T_CTX // L_LAT
    return pl.pallas_call(
        _attn_kernel,
        grid=(T // ATT_ROWS,),
        in_specs=[
            pl.BlockSpec((ATT_ROWS, GW), lambda s: (s, CB_NAQ)),
            pl.BlockSpec((ATT_ROWS, GW), lambda s: (s, CB_NAK)),
            pl.BlockSpec((ATT_ROWS, GW), lambda s: (s, CB_NAV)),
            pl.BlockSpec((L_LAT, GW), lambda s: (seq_blk0 + lat_req(s), CB_NAK)),
            pl.BlockSpec((L_LAT, GW), lambda s: (seq_blk0 + lat_req(s), CB_NAV)),
            pl.BlockSpec((1, 1, PAST, GW), lambda s: (lat_req(s), layer, 0, 0)),
            pl.BlockSpec((1, 1, PAST, GW), lambda s: (lat_req(s), layer, 0, 0)),
            pl.BlockSpec(memory_space=pltpu.SMEM),
        ],
        out_specs=[pl.BlockSpec((ATT_ROWS, GW), lambda s: (s, 0))]
        + [pl.BlockSpec((ATT_ROWS, GW), lambda s: (jnp.minimum(s, ATT_CTX_STEPS - 1), 0))] * 2,
        out_shape=[jax.ShapeDtypeStruct((T, GW), BF16)] + [jax.ShapeDtypeStruct((T_CTX, GW), F32)] * 2,
        scratch_shapes=[pltpu.VMEM((L_LAT, GW), BF16), pltpu.VMEM((L_LAT, GW), BF16),
                        pltpu.VMEM((HEADS, NA_DR - 1, GRID_W, 2 * GRID_W), F32)],
        compiler_params=_cparams(("arbitrary",)),
        name="attention",
    )(parts, parts, parts, parts, parts, ck, cv, rpb.reshape(HEADS * NA_DR, NA_DC))


def _rope(x, cos, sin_signed):
    lane = _iota(x.shape, 1)
    partner = jnp.where(lane % 32 < 16, pltpu.roll(x, GW - 16, 1), pltpu.roll(x, 16, 1))
    return x * cos + partner * sin_signed


def _seq_step(s):
    is_lat = s >= LAT_STEP0
    u = jnp.maximum(s - LAT_STEP0, 0)
    b = u // LAT_TILES
    t = u % LAT_TILES
    bwd = jnp.where(is_lat, LAT_STEP0 + b * LAT_TILES + (LAT_TILES - 1 - t), s)
    return is_lat, b, t, s, bwd


def _ret_kernel(qf_ref, kf_ref, vf_ref, qb_ref, kb_ref, vb_ref, cf_ref, sf_ref, cb_ref, sb_ref, lg_ref, s0_ref,
                of_ref, ob_ref, st_ref, sf_scr, sb_scr, w_scr, dq_scr, dk_scr):
    s = pl.program_id(0)
    is_lat, _, t, _, _ = _seq_step(s)
    n = SEQ_TILE
    blockdiag = _iota((GW, GW), 0) // HD == _iota((GW, GW), 1) // HD
    lg = _log_sigmoid(lg_ref[...])

    @pl.when(s == 0)
    def _():
        ti = _iota((n, GW), 0).astype(F32)
        i_ = _iota((n, n), 0).astype(F32)
        j_ = _iota((n, n), 1).astype(F32)
        for d in range(2):
            lgd = lg[d:d + 1, :]
            diff = (i_ - j_) if d == 0 else (j_ - i_)
            pos = ti if d == 0 else (n - 1.0) - ti
            w_scr[d] = jnp.concatenate(
                [jnp.where(diff >= 0, jnp.exp(jnp.maximum(diff, 0.0) * lgd[:, h * HD:h * HD + 1]), 0.0)
                 for h in range(HEADS)], axis=0)
            dq_scr[d] = jnp.exp((pos + 1.0) * lgd)
            dk_scr[d] = jnp.exp((n - 1.0 - pos) * lgd)

    @pl.when(jnp.logical_and(is_lat, t == 0))
    def _():
        for d, scr in ((0, sf_scr), (1, sb_scr)):
            s0 = jnp.concatenate([s0_ref[0, d]] * HEADS, axis=1)
            scr[...] = jnp.where(blockdiag, s0, 0.0)

    def stream(d, q_ref, k_ref, v_ref, c_ref, s_ref, o_ref, scr, latent):
        q = q_ref[...] * (RET_DK ** -0.5)
        k = k_ref[...]
        if latent:
            q = _rope(q, c_ref[...], s_ref[...])
            k = _rope(k, c_ref[...], s_ref[...])
        vb16 = v_ref[...].astype(BF16)
        yield
        a = _dot_nt(_expand_heads(q.astype(BF16), HD), k.astype(BF16)) * w_scr[d]
        yield
        o = _extract_heads(_dot(a.astype(BF16), vb16), n)
        yield
        if latent:
            s_old = scr[...]
            o = o + _dot(q.astype(BF16), s_old.astype(BF16)) * dq_scr[d]
        o_ref[...] = o
        yield
        upd = jnp.where(blockdiag, _dot_tn((k * dk_scr[d]).astype(BF16), vb16), 0.0)
        scr[...] = s_old * jnp.exp(float(n) * lg[d:d + 1, :]) + upd if latent else upd

    for latent in (False, True):
        @pl.when(is_lat if latent else jnp.logical_not(is_lat))
        def _():
            _interleave(stream(0, qf_ref, kf_ref, vf_ref, cf_ref, sf_ref, of_ref, sf_scr, latent),
                        stream(1, qb_ref, kb_ref, vb_ref, cb_ref, sb_ref, ob_ref, sb_scr, latent))

    @pl.when(jnp.logical_or(jnp.logical_not(is_lat), t == LAT_TILES - 1))
    def _():
        for d, scr in ((0, sf_scr), (1, sb_scr)):
            st = scr[...]
            st_ref[0, d] = st[:, 0:HD] + st[:, HD:2 * HD] + st[:, 2 * HD:3 * HD] + st[:, 3 * HD:4 * HD]


def _retention(parts, lg_lanes, rope_tabs, s0):
    n = SEQ_TILE
    fwd = lambda s: _seq_step(s)[3]
    bwd = lambda s: _seq_step(s)[4]
    in_specs = [pl.BlockSpec((n, GW), lambda s, c=c: (fwd(s), c)) for c in (CB_RQ, CB_RK, CB_RV)]
    in_specs += [pl.BlockSpec((n, GW), lambda s, c=c: (bwd(s), c)) for c in (CB_RQ, CB_RK, CB_RV)]
    in_specs += [pl.BlockSpec((n, GW), lambda s: (_seq_step(s)[2], 0))] * 2
    in_specs += [pl.BlockSpec((n, GW), lambda s: (LAT_TILES - 1 - _seq_step(s)[2], 0))] * 2
    in_specs += [pl.BlockSpec((2, GW), lambda s: (0, 0)),
                 pl.BlockSpec((1, 2, GW, HD), lambda s: (_seq_step(s)[1], 0, 0, 0))]
    return pl.pallas_call(
        _ret_kernel,
        grid=(SEQ_STEPS,),
        in_specs=in_specs,
        out_specs=[
            pl.BlockSpec((n, GW), lambda s: (fwd(s), 0)),
            pl.BlockSpec((n, GW), lambda s: (bwd(s), 0)),
            pl.BlockSpec((1, 2, GW, HD), lambda s: (jnp.minimum(s, B_CTX), 0, 0, 0)),
        ],
        out_shape=[
            jax.ShapeDtypeStruct((T, GW), F32),
            jax.ShapeDtypeStruct((T, GW), F32),
            jax.ShapeDtypeStruct((B_CTX + 1, 2, GW, HD), F32),
        ],
        scratch_shapes=[pltpu.VMEM((GW, GW), F32), pltpu.VMEM((GW, GW), F32),
                        pltpu.VMEM((2, HEADS * n, n), F32), pltpu.VMEM((2, n, GW), F32),
                        pltpu.VMEM((2, n, GW), F32)],
        compiler_params=_cparams(("arbitrary",)),
        name="retention",
    )(parts, parts, parts, parts, parts, parts, rope_tabs[0], rope_tabs[1], rope_tabs[0], rope_tabs[1],
      lg_lanes, s0)


GLA_QK = HEADS * GLA_DK
N_SUB = GLA_CHUNK // GLA_SUB
GLA_SAFE_DECAY = 60.0


def _gla_tile(q, k, v, la, st, rev):
    n, c = SEQ_TILE, GLA_CHUNK
    n_chunks = n // c
    ri = _iota((n, n), 0)
    ci = _iota((n, n), 1)
    same_chunk = ri // c == ci // c
    causal = (ci >= ri) if rev else (ci <= ri)
    tri = jnp.where(same_chunk, jnp.where(causal, 1.0, 0.0), 0.0).astype(BF16)
    b = _dot_exact01(tri, la)
    yield

    def rows_of(idx, count):
        if idx is None:
            return jnp.zeros((count, GLA_QK), F32)
        return jnp.broadcast_to(b[idx:idx + 1, :], (count, GLA_QK))

    def ref_row(cc, s):
        if rev:
            return cc * c + (s + 1) * GLA_SUB if s < N_SUB - 1 else None
        return cc * c + s * GLA_SUB - 1 if s > 0 else None

    end_rows = [cc * c if rev else cc * c + c - 1 for cc in range(n_chunks)]
    own_ref = jnp.concatenate([rows_of(ref_row(cc, s), GLA_SUB) for cc in range(n_chunks) for s in range(N_SUB)],
                              axis=0)
    b_end = jnp.concatenate([rows_of(r, c) for r in end_rows], axis=0)
    sub = (_iota((n, GLA_QK), 0) // GLA_SUB) % N_SUB
    qh = q * jnp.exp(b - own_ref)
    q_parts, k_parts = [], []
    for s in range(N_SUB):
        kvalid = (sub >= s) if rev else (sub <= s)
        ref_s = jnp.concatenate([rows_of(ref_row(cc, s), c) for cc in range(n_chunks)], axis=0)
        q_parts.append(jnp.where(sub == s, qh, 0.0))
        k_parts.append(jnp.where(kvalid, k * jnp.exp(jnp.minimum(ref_s - b, GLA_SAFE_DECAY)), 0.0))
    q_cat = _expand_heads(jnp.concatenate(q_parts, axis=1).astype(BF16), GLA_DK)
    k_cat = jnp.concatenate(k_parts, axis=1).astype(BF16)
    yield
    a = _dot_nt(q_cat, k_cat)
    yield
    qi = _iota((HEADS * n, n), 0) % n
    kj = _iota((HEADS * n, n), 1)
    keep = (qi // c == kj // c) & ((kj >= qi) if rev else (kj <= qi))
    vb16 = v.astype(BF16)
    o_intra = _extract_heads(_dot(jnp.where(keep, a, 0.0).astype(BF16), vb16), n)
    yield

    qt = (q * jnp.exp(b)).astype(BF16)
    kt = (k * jnp.exp(b_end - b)).astype(BF16)
    blockdiag = _iota((GW, GLA_QK), 0) // HD == _iota((GW, GLA_QK), 1) // GLA_DK
    upd = [jnp.where(blockdiag, _dot_tn(vb16[cc * c:(cc + 1) * c], kt[cc * c:(cc + 1) * c]), 0.0)
           for cc in range(n_chunks)]
    yield
    o_inter = [None] * n_chunks
    for cc in (reversed(range(n_chunks)) if rev else range(n_chunks)):
        o_inter[cc] = _dot_nt(qt[cc * c:(cc + 1) * c], st.astype(BF16))
        st = st * jnp.exp(b[end_rows[cc]:end_rows[cc] + 1, :]) + upd[cc]
        yield
    return o_intra + jnp.concatenate(o_inter, axis=0), st


def _gla_chunk(q, k, v, la, st, rev):
    c = GLA_CHUNK
    ri = _iota((c, c), 0)
    ci = _iota((c, c), 1)
    tri = jnp.where((ci >= ri) if rev else (ci <= ri), 1.0, 0.0).astype(BF16)
    b = _dot_exact01(tri, la)
    b_end = b[0:1, :] if rev else b[c - 1:c, :]
    row = _iota((c, GLA_QK), 0)
    sub = row // GLA_SUB
    off = row % GLA_SUB

    o = _dot_nt((q * jnp.exp(b)).astype(BF16), st.astype(BF16))
    kt = (k * jnp.exp(b_end - b)).astype(BF16)
    lane_h = _iota((GW, GLA_QK), 1) // GLA_DK
    row_h = _iota((GW, GLA_QK), 0) // HD
    st_new = st * jnp.exp(b_end) + jnp.where(row_h == lane_h, _dot_tn(v.astype(BF16), kt), 0.0)

    q_parts, k_parts = [], []
    for s in range(1, N_SUB):
        if rev:
            qsub, brow = N_SUB - 1 - s, b[(N_SUB - s) * GLA_SUB:(N_SUB - s) * GLA_SUB + 1, :]
            kvalid = sub > qsub
        else:
            qsub, brow = s, b[s * GLA_SUB - 1:s * GLA_SUB, :]
            kvalid = sub < qsub
        q_parts.append(jnp.where(sub == qsub, q * jnp.exp(jnp.where(sub == qsub, b - brow, 0.0)), 0.0))
        k_parts.append(jnp.where(kvalid, k * jnp.exp(jnp.where(kvalid, brow - b, 0.0)), 0.0))
    q_cat = _expand_heads(jnp.concatenate(q_parts, axis=1).astype(BF16), GLA_DK)
    k_cat = jnp.concatenate(k_parts, axis=1).astype(BF16)
    a_off = _dot_nt(q_cat, k_cat)
    o = o + _extract_heads(_dot(a_off.astype(BF16), v.astype(BF16)), c)

    red = jnp.where(_iota((GLA_QK, GW), 0) // GLA_DK == _iota((GLA_QK, GW), 1) // HD, 1.0, 0.0).astype(BF16)
    rowv = _iota((c, GW), 0) % GLA_SUB
    for dl in range(GLA_SUB):
        if dl == 0:
            x = q * k
            vs = v
        else:
            sh = dl if not rev else c - dl
            valid = (off + dl < GLA_SUB) if rev else (off >= dl)
            ks = pltpu.roll(k, sh, 0)
            bs = pltpu.roll(b, sh, 0)
            vs = pltpu.roll(v, sh, 0)
            x = jnp.where(valid, q * ks * jnp.exp(jnp.where(valid, b - bs, 0.0)), 0.0)
            validv = (rowv + dl < GLA_SUB) if rev else (rowv >= dl)
            vs = jnp.where(validv, vs, 0.0)
        o = o + _dot(x.astype(BF16), red) * vs
    return o, st_new


def _gla_kernel(qkf_ref, vf_ref, lrf_ref, qkb_ref, vb_ref, lrb_ref, lrf_next_ref, lrb_next_ref, gu_ref, gb_ref,
                s0_ref, of_ref, ob_ref, st_ref, sf_scr, sb_scr, la_scr, decay_scr):
    step = pl.program_id(0)
    slot = step % 2
    is_lat, _, t, _, _ = _seq_step(step)
    lane_h = _iota((GW, GLA_QK), 1) // GLA_DK
    row_h = _iota((GW, GLA_QK), 0) // HD
    blockdiag = row_h == lane_h

    @pl.when(jnp.logical_or(jnp.logical_not(is_lat), t == 0))
    def _():
        for d, scr in ((0, sf_scr), (1, sb_scr)):
            s0t = jnp.concatenate([s0_ref[0, d].T] * HEADS, axis=0)
            scr[...] = jnp.where(blockdiag, s0t, 0.0) * jnp.where(is_lat, 1.0, 0.0)

    c = GLA_CHUNK
    n_chunks = SEQ_TILE // c

    def gates(lrf, lrb, dst):
        zf = _dot(lrf[...].astype(BF16), gu_ref[0].astype(BF16)) + gb_ref[0]
        zb = _dot(lrb[...].astype(BF16), gu_ref[1].astype(BF16)) + gb_ref[1]
        yield
        laf = _log_sigmoid(zf) / GLA_TAU
        lab = _log_sigmoid(zb) / GLA_TAU
        la_scr[dst, 0] = laf
        la_scr[dst, 1] = lab
        yield
        decay_scr[dst] = GLA_SUB * jnp.max(jnp.maximum(-laf, -lab))

    @pl.when(step == 0)
    def _():
        _interleave(gates(lrf_ref, lrb_ref, 0))

    next_gates = lambda: gates(lrf_next_ref, lrb_next_ref, 1 - slot)

    def tile_stream(d, qk_ref, v_ref, la, scr):
        return _gla_tile(qk_ref[:, 0:GLA_QK] * (GLA_DK ** -0.5), qk_ref[:, GLA_QK:2 * GLA_QK],
                         v_ref[...], la, scr[...], rev=(d == 1))

    def chunk_stream(d, qk_ref, v_ref, la, o_ref, scr):
        st = scr[...]
        for cc in (range(n_chunks) if d == 0 else reversed(range(n_chunks))):
            rows = slice(cc * c, (cc + 1) * c)
            q = qk_ref[rows, 0:GLA_QK] * (GLA_DK ** -0.5)
            k = qk_ref[rows, GLA_QK:2 * GLA_QK]
            o, st = _gla_chunk(q, k, v_ref[rows, :], la[rows, :], st, rev=(d == 1))
            o_ref[rows, :] = o
        scr[...] = st

    la_f = la_scr[slot, 0]
    la_b = la_scr[slot, 1]
    decay = decay_scr[slot]

    @pl.when(decay <= GLA_SAFE_DECAY)
    def _():
        (o_f, st_f), (o_b, st_b), _ = _interleave(tile_stream(0, qkf_ref, vf_ref, la_f, sf_scr),
                                                  tile_stream(1, qkb_ref, vb_ref, la_b, sb_scr), next_gates())
        of_ref[...] = o_f
        ob_ref[...] = o_b
        sf_scr[...] = st_f
        sb_scr[...] = st_b

    @pl.when(decay > GLA_SAFE_DECAY)
    def _():
        chunk_stream(0, qkf_ref, vf_ref, la_f, of_ref, sf_scr)
        chunk_stream(1, qkb_ref, vb_ref, la_b, ob_ref, sb_scr)
        _interleave(next_gates())

    @pl.when(jnp.logical_or(jnp.logical_not(is_lat), t == LAT_TILES - 1))
    def _():
        for d, scr in ((0, sf_scr), (1, sb_scr)):
            s = scr[...].T
            st_ref[0, d] = s[:, 0:HD] + s[:, HD:2 * HD] + s[:, 2 * HD:3 * HD] + s[:, 3 * HD:4 * HD]


def _gla(parts, gate_up_pad, gate_b, s0):
    n = SEQ_TILE
    in_specs = []
    for blk in (3, 4):
        in_specs += [
            pl.BlockSpec((n, GW), lambda s, blk=blk: (_seq_step(s)[blk], CB_AQK)),
            pl.BlockSpec((n, GW), lambda s, blk=blk: (_seq_step(s)[blk], CB_AV)),
            pl.BlockSpec((n, 128), lambda s, blk=blk: (_seq_step(s)[blk], CB_LR128)),
        ]
    nxt = lambda s: jnp.minimum(s + 1, SEQ_STEPS - 1)
    in_specs += [pl.BlockSpec((n, 128), lambda s, blk=blk: (_seq_step(nxt(s))[blk], CB_LR128)) for blk in (3, 4)]
    in_specs += [pl.BlockSpec((2, 128, GLA_QK), lambda s: (0, 0, 0)),
                 pl.BlockSpec((2, 1, GLA_QK), lambda s: (0, 0, 0)),
                 pl.BlockSpec((1, 2, GLA_QK, HD), lambda s: (_seq_step(s)[1], 0, 0, 0))]
    return pl.pallas_call(
        _gla_kernel,
        grid=(SEQ_STEPS,),
        in_specs=in_specs,
        out_specs=[
            pl.BlockSpec((n, GW), lambda s: (_seq_step(s)[3], 0)),
            pl.BlockSpec((n, GW), lambda s: (_seq_step(s)[4], 0)),
            pl.BlockSpec((1, 2, GLA_QK, HD), lambda s: (jnp.minimum(s, B_CTX), 0, 0, 0)),
        ],
        out_shape=[
            jax.ShapeDtypeStruct((T, GW), F32),
            jax.ShapeDtypeStruct((T, GW), F32),
            jax.ShapeDtypeStruct((B_CTX + 1, 2, GLA_QK, HD), F32),
        ],
        scratch_shapes=[pltpu.VMEM((GW, GLA_QK), F32), pltpu.VMEM((GW, GLA_QK), F32),
                        pltpu.VMEM((2, 2, n, GLA_QK), F32), pltpu.SMEM((2,), F32)],
        compiler_params=_cparams(("arbitrary",)),
        name="gla",
    )(parts, parts, parts, parts, parts, parts, parts, parts, gate_up_pad, gate_b, s0)


OUT_TM = 512
FFN_TF = 1408
FFN_PASSES = D_FF // FFN_TF
OUT_TILES = T // OUT_TM


def _run_in_order(order):
    results = {}
    for gen in order:
        try:
            next(gen)
        except StopIteration as stop:
            results[gen] = stop.value
    return results


def _outproj_ffn_kernel(*refs, n_x, n_out):
    x_refs = refs[:n_x]
    (mod_ref, modp_ref, gpost_ref, gpre2_ref, gpost2_ref, pool_ref, na_ref, rf_ref, rb_ref, rg_ref, af_ref, ab_ref,
     ag_ref, ng_ref, w_ref, wg_ref, wu_ref, wd_ref) = refs[n_x:n_x + 18]
    o_refs = refs[n_x + 18:n_x + 18 + n_out]
    x1_scr, hb_scr = refs[n_x + 18 + n_out:]
    i = pl.program_id(0)
    wslot = i % 2
    rslot = 1 - wslot

    @pl.when(i == 0)
    def _():
        x1_scr[1] = jnp.zeros((OUT_TM, D), F32)
        hb_scr[1] = jnp.zeros((OUT_TM, D), BF16)

    avg = jnp.where(_iota((GW, GW), 0) // HD == _iota((GW, GW), 1) // HD, 1.0 / HD, 0.0).astype(BF16)
    x_tile = _x_tile(x_refs, OUT_TM, jnp.minimum(i, OUT_TILES - 1))

    def mixer_tail(rows):
        r = rf_ref[rows, :] + rb_ref[rows, :]
        r = r - _head_mean(r, avg)
        r = r * lax.rsqrt(_head_mean(r * r, avg) + GN_EPS) * _silu(rg_ref[rows, :])
        a = af_ref[rows, :] + ab_ref[rows, :]
        a = a * lax.rsqrt(_head_mean(a * a, avg) + RMS_EPS) * ng_ref[...] * _silu(ag_ref[rows, :])
        yield
        y = _dot(pool_ref[rows, :].astype(BF16), w_ref[0, 0:GW, :])
        y = y + _dot(na_ref[rows, :].astype(BF16), w_ref[0, GW:2 * GW, :])
        y = y + _dot(r.astype(BF16), w_ref[0, 2 * GW:3 * GW, :])
        y = y + _dot(a.astype(BF16), w_ref[0, 3 * GW:4 * GW, :])
        yield
        x1 = x_tile[rows] + mod_ref[0, 2:3, :] * _rms(y, gpost_ref[...])
        x1_scr[wslot, rows, :] = x1
        hb_scr[wslot, rows, :] = (_rms(x1, gpre2_ref[...]) * (1.0 + mod_ref[0, 4:5, :])
                                  + mod_ref[0, 3:4, :]).astype(BF16)

    def ffn(rows):
        hb = hb_scr[rslot, rows, :]
        y = None
        for j in range(FFN_PASSES):
            cols = slice(j * FFN_TF, (j + 1) * FFN_TF)
            act = (_silu(_dot(hb, wg_ref[0, :, cols])) * _dot(hb, wu_ref[0, :, cols])).astype(BF16)
            yield
            part = _dot(act, wd_ref[0, cols, :])
            y = part if y is None else y + part
            yield
        return x1_scr[rslot, rows, :] + modp_ref[0, 5:6, :] * _rms(y, gpost2_ref[...])

    half = OUT_TM // 2
    fa, fb = ffn(slice(0, half)), ffn(slice(half, OUT_TM))
    ta, tb = mixer_tail(slice(0, half)), mixer_tail(slice(half, OUT_TM))
    done = _run_in_order([fa, ta, fa, ta, fa, ta, fa, fb, tb, fa, fb, tb, fb, tb, fb, fb])
    x2 = jnp.concatenate([done[fa], done[fb]], axis=0)
    if len(o_refs) == 1:
        o_refs[0][...] = x2
    else:
        is_ctx = i - 1 < T_CTX // OUT_TM

        @pl.when(is_ctx)
        def _():
            o_refs[0][...] = x2

        @pl.when(jnp.logical_not(is_ctx))
        def _():
            o_refs[1][...] = x2


def _outproj_ffn(x, mod_l, g_post, g_pre2, g_post2, o_pool, o_na, ret_f, ret_b, gla_f, gla_b, parts, ng_lanes,
                 w_out_b, wg, wu, wd, layer, split_out):
    tm = OUT_TM
    cur = lambda i: jnp.minimum(i, OUT_TILES - 1)
    prev = lambda i: jnp.maximum(i - 1, 0)
    act = pl.BlockSpec((tm, GW), lambda i: (cur(i), 0))
    vec = pl.BlockSpec((1, D), lambda i: (0, 0))
    once = pl.Buffered(1)
    x_specs, x_args = _x_specs(x, tm, cur)
    if split_out:
        out_specs, _ = _x_specs((None, None), tm, prev)
        out_shape = [jax.ShapeDtypeStruct((T_CTX, D), F32), jax.ShapeDtypeStruct((T_LAT, D), F32)]
    else:
        out_specs = [pl.BlockSpec((tm, D), lambda i: (prev(i), 0))]
        out_shape = [jax.ShapeDtypeStruct((T, D), F32)]
    out = pl.pallas_call(
        functools.partial(_outproj_ffn_kernel, n_x=len(x_args), n_out=len(out_shape)),
        grid=(OUT_TILES + 1,),
        in_specs=x_specs + [
            pl.BlockSpec((1, 6, D), lambda i: (_mod_row(cur(i), tm), 0, 0)),
            pl.BlockSpec((1, 6, D), lambda i: (_mod_row(prev(i), tm), 0, 0)),
            vec, vec, vec,
            act, act, act, act,
            pl.BlockSpec((tm, GW), lambda i: (cur(i), CB_RG)),
            act, act,
            pl.BlockSpec((tm, GW), lambda i: (cur(i), CB_AG)),
            pl.BlockSpec((1, GW), lambda i: (0, 0)),
            pl.BlockSpec((1, D, D), lambda i: (layer, 0, 0), pipeline_mode=once),
            pl.BlockSpec((1, D, D_FF), lambda i: (layer, 0, 0), pipeline_mode=once),
            pl.BlockSpec((1, D, D_FF), lambda i: (layer, 0, 0), pipeline_mode=once),
            pl.BlockSpec((1, D_FF, D), lambda i: (layer, 0, 0), pipeline_mode=once),
        ],
        out_specs=out_specs,
        out_shape=out_shape,
        scratch_shapes=[pltpu.VMEM((2, tm, D), F32), pltpu.VMEM((2, tm, D), BF16)],
        compiler_params=_cparams(("arbitrary",)),
        name="outproj_ffn",
    )(*x_args, mod_l, mod_l, g_post, g_pre2, g_post2, o_pool, o_na, ret_f, ret_b, parts, gla_f, gla_b, parts, ng_lanes,
      w_out_b, wg, wu, wd)
    return tuple(out) if split_out else out[0]


def _rope_tables():
    nf = 16
    inv = (ROPE_BASE ** (-np.arange(nf, dtype=np.float32) / nf)).astype(np.float32)
    tok = np.arange(L_LAT)
    cos = np.zeros((L_LAT, HD), np.float32)
    sin = np.zeros((L_LAT, HD), np.float32)
    for axis, pos in enumerate((tok // GRID_W, tok % GRID_W)):
        ang = pos.astype(np.float32)[:, None] * inv[None, :]
        c, s = np.cos(ang), np.sin(ang)
        cos[:, axis * 32:axis * 32 + 32] = np.concatenate([c, c], axis=1)
        sin[:, axis * 32:axis * 32 + 32] = np.concatenate([-s, s], axis=1)
    return jnp.asarray(np.tile(cos, (1, HEADS))), jnp.asarray(np.tile(sin, (1, HEADS)))


def _block_diag(w):
    g, c, _ = w.shape
    out = jnp.zeros((g * c, g * c), w.dtype)
    for i in range(g):
        out = out.at[i * c:(i + 1) * c, i * c:(i + 1) * c].set(w[i])
    return out


def kernel(x_prompt, x_sample, cache_na_k, cache_na_v, state_ret, state_gla, c, c_ctx, w_mod, b_mod,
           g_pre_mix, g_post_mix, g_pre_ffn, g_post_ffn, w_in, w_out, pool_w, pool_scale, na_rpb,
           ret_decay_logit, gla_gate_up, gla_gate_b, gla_norm_g, w_ffn_gate, w_ffn_up, w_ffn_down):
    x = (x_prompt.reshape(T_CTX, D), x_sample.reshape(T_LAT, D))
    cv8 = jnp.concatenate([c_ctx[None, :], c, jnp.zeros((8 - 1 - B_LAT, D), F32)], axis=0)
    mods = _modulation(cv8, w_mod, b_mod).reshape(DEPTH, 8, 6, D)

    w_in_b = jnp.pad(w_in, ((0, 0), (0, 0), (0, P_PAD - P_IN))).astype(BF16)
    w_out_b = w_out.astype(BF16)
    wg_b, wu_b, wd_b = w_ffn_gate.astype(BF16), w_ffn_up.astype(BF16), w_ffn_down.astype(BF16)
    gate_up_pad = jnp.pad(gla_gate_up, ((0, 0), (0, 0), (0, 128 - GLA_LOWRANK), (0, 0)))
    rope_tabs = _rope_tables()
    ck = cache_na_k.reshape(B_LAT, DEPTH, PAST, GW)
    cv = cache_na_v.reshape(B_LAT, DEPTH, PAST, GW)
    s0_ret = state_ret.reshape(B_LAT, DEPTH, 2, GW, HD)
    s0_gla = state_gla.reshape(B_LAT, DEPTH, 2, GLA_QK, HD)

    ks, vs, srs, sgs = [], [], [], []
    for l in range(DEPTH):
        mod_l = mods[l]
        parts = _inproj(x, mod_l, g_pre_mix[l][None, :], w_in_b, l)
        o_pool = _pool(parts, _block_diag(pool_w[l]).astype(BF16), pool_scale[l][None, :])
        o_na, k_ctx, v_ctx = _attention(parts, ck, cv, na_rpb[l], l)
        lg_lanes = jnp.repeat(ret_decay_logit[l], HD, axis=1)
        rf, rb, s_ret = _retention(parts, lg_lanes, rope_tabs, s0_ret[:, l])
        gf, gbw, s_gla = _gla(parts, gate_up_pad[l], gla_gate_b[l][:, None, :], s0_gla[:, l])
        x = _outproj_ffn(x, mod_l, g_post_mix[l][None, :], g_pre_ffn[l][None, :], g_post_ffn[l][None, :],
                         o_pool, o_na, rf, rb, gf, gbw, parts, jnp.tile(gla_norm_g[l], HEADS)[None, :],
                         w_out_b, wg_b, wu_b, wd_b, l, split_out=(l == DEPTH - 1))
        ks.append(k_ctx.reshape(B_CTX, L_CTX, HEADS, HD))
        vs.append(v_ctx.reshape(B_CTX, L_CTX, HEADS, HD))
        srs.append(s_ret[:B_CTX].reshape(B_CTX, 2, HEADS, RET_DK, HD))
        sgs.append(s_gla[:B_CTX].reshape(B_CTX, 2, HEADS, GLA_DK, HD))

    return (x[0].reshape(B_CTX, L_CTX, D), x[1].reshape(B_LAT, L_LAT, D),
            jnp.stack(ks, axis=1), jnp.stack(vs, axis=1), jnp.stack(srs, axis=1), jnp.stack(sgs, axis=1))
```

```python
import functools

import numpy as np
import jax
import jax.numpy as jnp
from jax import lax
from jax.experimental import pallas as pl
from jax.experimental.pallas import tpu as pltpu

F32 = jnp.float32
BF16 = jnp.bfloat16

D = 1024
B_CTX, L_CTX = 32, 256
B_LAT, L_LAT = 2, 4096
DEPTH = 4
PAST = 256
GRID_W = 64
GRID_H = L_LAT // GRID_W
T_CTX = B_CTX * L_CTX
T_LAT = B_LAT * L_LAT
T = T_CTX + T_LAT
GW = 256
HEADS = 4
HD = 64
POOL_WINDOWS = (2, 4, 8, 16)
NA_ROWS, NA_COLS = 8, 16
RET_DK = 64
GLA_DK = 32
GLA_LOWRANK = 16
GLA_TAU = 16.0
D_FF = 2816
P_IN = 2832
P_PAD = 2944
P_MAIN = 2816
ROPE_BASE = 10000.0
RMS_EPS = 1e-6
GN_EPS = 1e-5
NEG = -1e30

CB_POOL, CB_NAQ, CB_NAK, CB_NAV, CB_RQ, CB_RK, CB_RV, CB_RG, CB_AQK, CB_AV, CB_AG = range(11)
CB_LR128 = P_IN // 128

SEQ_TILE = L_CTX
SEQ_STEPS = T // SEQ_TILE
LAT_STEP0 = T_CTX // SEQ_TILE
LAT_TILES = L_LAT // SEQ_TILE
GLA_CHUNK = 64
GLA_SUB = 16
VMEM_LIMIT = 56 * 1024 * 1024


def _cparams(sem):
    return pltpu.CompilerParams(dimension_semantics=sem, vmem_limit_bytes=VMEM_LIMIT)


def _silu(x):
    return x / (1.0 + jnp.exp(-x))


def _log_sigmoid(z):
    return jnp.minimum(z, 0.0) - jnp.log1p(jnp.exp(-jnp.abs(z)))


def _rms(x, g):
    return x * lax.rsqrt(jnp.mean(x * x, axis=-1, keepdims=True) + RMS_EPS) * g


def _dot(a, b):
    return jnp.dot(a, b, preferred_element_type=F32)


def _dot_nt(a, b):
    return lax.dot_general(a, b, (((1,), (1,)), ((), ())), preferred_element_type=F32)


def _dot_tn(a, b):
    return lax.dot_general(a, b, (((0,), (0,)), ((), ())), preferred_element_type=F32)


def _split_hi_lo(x):
    hi = x.astype(BF16)
    return hi, (x - hi.astype(F32)).astype(BF16)


def _dot_exact01(a01, x):
    hi, lo = _split_hi_lo(x)
    return _dot(a01, hi) + _dot(a01, lo)


def _iota(shape, dim):
    return lax.broadcasted_iota(jnp.int32, shape, dim)


def _expand_heads(x, head_w):
    n, w = x.shape
    xe = jnp.concatenate([x] * HEADS, axis=0)
    rowh = _iota((HEADS * n, w), 0) // n
    laneh = (_iota((HEADS * n, w), 1) // head_w) % HEADS
    return jnp.where(rowh == laneh, xe, jnp.zeros_like(xe))


def _extract_heads(p, n):
    laneh = _iota((n, GW), 1) // HD
    out = p[0:n]
    for h in range(1, HEADS):
        out = jnp.where(laneh == h, p[h * n:(h + 1) * n], out)
    return out


def _head_mean(x, avg):
    hi, lo = _split_hi_lo(x)
    return _dot(hi, avg) + _dot(lo, avg)


def _interleave(*gens):
    results = [None] * len(gens)
    live = list(range(len(gens)))
    while live:
        for i in list(live):
            try:
                next(gens[i])
            except StopIteration as stop:
                results[i] = stop.value
                live.remove(i)
    return results


def _mod_row(i, tm):
    return jnp.where(i < T_CTX // tm, 0, 1 + (i * tm - T_CTX) // L_LAT)


def _mod_kernel(cv_ref, w_ref, b_ref, o_ref):
    s = _silu(cv_ref[...]).astype(BF16)
    o_ref[0] = _dot(s, w_ref[0].astype(BF16)) + b_ref[0]


def _modulation(cv8, w_mod, b_mod):
    tn = 1536
    return pl.pallas_call(
        _mod_kernel,
        grid=(DEPTH, 6 * D // tn),
        in_specs=[
            pl.BlockSpec((8, D), lambda l, j: (0, 0)),
            pl.BlockSpec((1, D, tn), lambda l, j: (l, 0, j)),
            pl.BlockSpec((1, 1, tn), lambda l, j: (l, 0, j)),
        ],
        out_specs=pl.BlockSpec((1, 8, tn), lambda l, j: (l, 0, j)),
        out_shape=jax.ShapeDtypeStruct((DEPTH, 8, 6 * D), F32),
        compiler_params=_cparams(("arbitrary", "arbitrary")),
        name="modulation",
    )(cv8, w_mod, b_mod.reshape(DEPTH, 1, 6 * D))


IN_TM = 512


def _x_specs(x, tm, tile_of=lambda i: i):
    if not isinstance(x, tuple):
        return [pl.BlockSpec((tm, D), lambda i: (tile_of(i), 0))], [x]
    nc = T_CTX // tm
    return ([pl.BlockSpec((tm, D), lambda i: (jnp.minimum(tile_of(i), nc - 1), 0)),
             pl.BlockSpec((tm, D), lambda i: (jnp.maximum(tile_of(i) - nc, 0), 0))], list(x))


def _x_tile(x_refs, tm, tile):
    if len(x_refs) == 1:
        return x_refs[0][...]
    return jnp.where(tile < T_CTX // tm, x_refs[0][...], x_refs[1][...])


def _inproj_kernel(*refs, n_x):
    x_refs, (mod_ref, g_ref, w_ref, wlr_ref, o_ref) = refs[:n_x], refs[n_x:]
    x = _x_tile(x_refs, IN_TM, pl.program_id(0))

    def half_tile(rows):
        h = _rms(x[rows], g_ref[...]) * (1.0 + mod_ref[0, 1:2, :]) + mod_ref[0, 0:1, :]
        hb = h.astype(BF16)
        yield
        for a in range(0, P_MAIN, 1024):
            b = min(a + 1024, P_MAIN)
            o_ref[rows, a:b] = _dot(hb, w_ref[0, :, a:b])
            yield
        o_ref[rows, P_MAIN:P_PAD] = _dot(hb, wlr_ref[0])

    half = IN_TM // 2
    first, second = half_tile(slice(0, half)), half_tile(slice(half, IN_TM))
    next(first)
    _interleave(first, second)


def _inproj(x, mod_l, g_pre, w_main_b, w_lr_b, layer):
    tm = IN_TM
    x_specs, x_args = _x_specs(x, tm)
    return pl.pallas_call(
        functools.partial(_inproj_kernel, n_x=len(x_args)),
        grid=(T // tm,),
        in_specs=x_specs + [
            pl.BlockSpec((1, 6, D), lambda i: (_mod_row(i, tm), 0, 0)),
            pl.BlockSpec((1, D), lambda i: (0, 0)),
            pl.BlockSpec((1, D, P_MAIN), lambda i: (layer, 0, 0), pipeline_mode=pl.Buffered(1)),
            pl.BlockSpec((1, D, P_PAD - P_MAIN), lambda i: (layer, 0, 0), pipeline_mode=pl.Buffered(1)),
        ],
        out_specs=pl.BlockSpec((tm, P_PAD), lambda i: (i, 0)),
        out_shape=jax.ShapeDtypeStruct((T, P_PAD), F32),
        compiler_params=_cparams(("arbitrary",)),
        name="inproj",
    )(*x_args, mod_l, g_pre, w_main_b, w_lr_b)


POOL_TM = 512
POOL_SUB = 256


def _pool_kernel(v_ref, w_ref, scale_ref, o_ref, band_scr, cnt_scr):
    i = pl.program_id(0)
    n = POOL_SUB

    @pl.when(i == 0)
    def _():
        t = _iota((n, n), 0)
        s = _iota((n, n), 1)
        lane_g = _iota((n, GW), 1) // HD
        for kind, seg_len in enumerate((L_CTX, GRID_W)):
            seg0 = t & ~(seg_len - 1)
            seg1 = seg0 + seg_len
            cnt = jnp.zeros((n, GW), F32)
            for gi, win in enumerate(POOL_WINDOWS):
                lo = jnp.maximum(t - win // 2, seg0)
                hi = jnp.minimum(t - win // 2 + win, seg1)
                band_scr[kind, gi] = jnp.where(s >= lo, jnp.where(s < hi, 1.0, 0.0), 0.0).astype(BF16)
                cnt = jnp.where(lane_g == gi, (hi - lo).astype(F32), cnt)
            cnt_scr[kind] = cnt

    kind = jnp.where(i < T_CTX // POOL_TM, 0, 1)

    def piece(rows):
        v = v_ref[rows, :]
        vh, vl = _split_hi_lo(v)
        lane_g = _iota((n, GW), 1) // HD
        yield
        mean = None
        for gi in range(len(POOL_WINDOWS)):
            band = band_scr[kind, gi]
            m = _dot(band, vh) + _dot(band, vl)
            mean = m if mean is None else jnp.where(lane_g == gi, m, mean)
        yield
        d = (mean / cnt_scr[kind] - v).astype(BF16)
        o_ref[rows, :] = (_dot(d, w_ref[...]) * scale_ref[...]).astype(o_ref.dtype)

    _interleave(*[piece(slice(j * n, (j + 1) * n)) for j in range(POOL_TM // n)])


def _pool(parts, w_bd, scale):
    tm = POOL_TM
    n_win = len(POOL_WINDOWS)
    return pl.pallas_call(
        _pool_kernel,
        grid=(T // tm,),
        in_specs=[
            pl.BlockSpec((tm, GW), lambda i: (i, CB_POOL)),
            pl.BlockSpec((GW, GW), lambda i: (0, 0)),
            pl.BlockSpec((1, GW), lambda i: (0, 0)),
        ],
        out_specs=pl.BlockSpec((tm, GW), lambda i: (i, 0)),
        out_shape=jax.ShapeDtypeStruct((T, GW), BF16),
        scratch_shapes=[pltpu.VMEM((2, n_win, POOL_SUB, POOL_SUB), BF16), pltpu.VMEM((2, POOL_SUB, GW), F32)],
        compiler_params=_cparams(("arbitrary",)),
        name="pool",
    )(parts, w_bd, scale)


def _softmax_rows(s):
    m = jnp.max(s, axis=-1, keepdims=True)
    p = jnp.exp(s - m)
    return p / jnp.sum(p, axis=-1, keepdims=True)


def _ctx_attn_rows(q_ref, k_ref, v_ref, o_ref, rows):
    qe = _expand_heads(q_ref[rows, :].astype(BF16), HD)
    yield
    s = _dot_nt(qe, k_ref[rows, :].astype(BF16)) * (HD ** -0.5)
    yield
    p = _softmax_rows(s).astype(BF16)
    yield
    o_ref[rows, :] = _extract_heads(_dot(p, v_ref[rows, :].astype(BF16)), L_CTX).astype(o_ref.dtype)


NA_ROWS_PER_STEP = 8
NA_INTERLEAVE = 8
ATT_ROWS = NA_ROWS_PER_STEP * GRID_W
ATT_CTX_STEPS = T_CTX // ATT_ROWS
ATT_LAT_STEPS = L_LAT // ATT_ROWS
NA_WIN = NA_ROWS * GRID_W
NA_DR = 2 * NA_ROWS - 1
NA_DC = 2 * NA_COLS - 1


def _na_bias_table(rpb_ref, e2_ref):
    shape = (GRID_W, 2 * GRID_W)
    qc = _iota(shape, 0)
    lane = _iota(shape, 1)
    kc = lane % GRID_W
    upper = lane >= GRID_W
    c0 = jnp.clip(qc - NA_COLS // 2, 0, GRID_W - NA_COLS)
    dc = jnp.where((kc >= c0) & (kc < c0 + NA_COLS), kc - qc + (NA_COLS - 1), -1)

    def one(ha, carry):
        h = ha // (NA_DR - 1)
        a = ha % (NA_DR - 1)
        acc = jnp.full(shape, NEG, F32)
        for j in range(NA_DC):
            val = jnp.where(upper, rpb_ref[h * NA_DR + a + 1, j], rpb_ref[h * NA_DR + a, j])
            acc = jnp.where(dc == j, val, acc)
        e2_ref[h, a] = acc
        return carry

    lax.fori_loop(0, HEADS * (NA_DR - 1), one, 0)


def _attn_kernel(q_ref, k_ref, v_ref, kseq_ref, vseq_ref, ck_ref, cv_ref, rpb_ref, o_ref, ko_ref, vo_ref,
                 kb_ref, vb_ref, e2_ref):
    s = pl.program_id(0)

    @pl.when(s < ATT_CTX_STEPS)
    def _():
        ko_ref[...] = k_ref[...]
        vo_ref[...] = v_ref[...]
        _interleave(*[_ctx_attn_rows(q_ref, k_ref, v_ref, o_ref, slice(i * L_CTX, (i + 1) * L_CTX))
                      for i in range(ATT_ROWS // L_CTX)])

    @pl.when(s >= ATT_CTX_STEPS)
    def _():
        step = (s - ATT_CTX_STEPS) % ATT_LAT_STEPS

        @pl.when(s == ATT_CTX_STEPS)
        def _():
            _na_bias_table(rpb_ref, e2_ref)

        @pl.when(step == 0)
        def _():
            kb_ref[...] = kseq_ref[...].astype(BF16)
            vb_ref[...] = vseq_ref[...].astype(BF16)

        _na_rows(step, q_ref, ck_ref, cv_ref, o_ref, kb_ref, vb_ref, e2_ref)


def _na_rows(step, q_ref, ck_ref, cv_ref, o_ref, kb_ref, vb_ref, e2_ref):
    ckb = ck_ref[0, 0].astype(BF16)
    cvb = cv_ref[0, 0].astype(BF16)
    scale = HD ** -0.5

    def one_row(rr):
        r = step * NA_ROWS_PER_STEP + rr
        r0 = jnp.clip(r - NA_ROWS // 2, 0, GRID_H - NA_ROWS)
        base = r0 - r + (NA_ROWS - 1)
        q0 = pl.multiple_of(rr * GRID_W, GRID_W)
        k0 = pl.multiple_of(r0 * GRID_W, GRID_W)
        qe = _expand_heads(q_ref[pl.ds(q0, GRID_W), :].astype(BF16), HD)
        kw = kb_ref[pl.ds(k0, NA_WIN), :]
        vw = vb_ref[pl.ds(k0, NA_WIN), :]
        bias = jnp.concatenate(
            [jnp.concatenate([e2_ref[h, base + 2 * p] for p in range(NA_ROWS // 2)], axis=1)
             for h in range(HEADS)], axis=0)
        yield
        s_loc = _dot_nt(qe, kw) * scale + bias
        s_ctx = _dot_nt(qe, ckb) * scale
        yield
        m = jnp.maximum(jnp.max(s_loc, axis=-1, keepdims=True), jnp.max(s_ctx, axis=-1, keepdims=True))
        p_loc = jnp.exp(s_loc - m)
        p_ctx = jnp.exp(s_ctx - m)
        inv = 1.0 / (jnp.sum(p_loc, axis=-1, keepdims=True) + jnp.sum(p_ctx, axis=-1, keepdims=True))
        yield
        pv = _dot((p_loc * inv).astype(BF16), vw) + _dot((p_ctx * inv).astype(BF16), cvb)
        yield
        o_ref[pl.ds(q0, GRID_W), :] = _extract_heads(pv, GRID_W).astype(o_ref.dtype)

    def row_group(i, carry):
        _interleave(*[one_row(NA_INTERLEAVE * i + j) for j in range(NA_INTERLEAVE)])
        return carry

    lax.fori_loop(0, NA_ROWS_PER_STEP // NA_INTERLEAVE, row_group, 0)


def _attention(parts, ck, cv, rpb, layer):
    lat_req = lambda s: jnp.maximum(s - ATT_CTX_STEPS, 0) // ATT_LAT_STEPS
    seq_blk0 = T_CTX // L_LAT
    return pl.pallas_call(
        _attn_kernel,
        grid=(T // ATT_ROWS,),
        in_specs=[
            pl.BlockSpec((ATT_ROWS, GW), lambda s: (s, CB_NAQ)),
            pl.BlockSpec((ATT_ROWS, GW), lambda s: (s, CB_NAK)),
            pl.BlockSpec((ATT_ROWS, GW), lambda s: (s, CB_NAV)),
            pl.BlockSpec((L_LAT, GW), lambda s: (seq_blk0 + lat_req(s), CB_NAK)),
            pl.BlockSpec((L_LAT, GW), lambda s: (seq_blk0 + lat_req(s), CB_NAV)),
            pl.BlockSpec((1, 1, PAST, GW), lambda s: (lat_req(s), layer, 0, 0)),
            pl.BlockSpec((1, 1, PAST, GW), lambda s: (lat_req(s), layer, 0, 0)),
            pl.BlockSpec(memory_space=pltpu.SMEM),
        ],
        out_specs=[pl.BlockSpec((ATT_ROWS, GW), lambda s: (s, 0))]
        + [pl.BlockSpec((ATT_ROWS, GW), lambda s: (jnp.minimum(s, ATT_CTX_STEPS - 1), 0))] * 2,
        out_shape=[jax.ShapeDtypeStruct((T, GW), BF16)] + [jax.ShapeDtypeStruct((T_CTX, GW), F32)] * 2,
        scratch_shapes=[pltpu.VMEM((L_LAT, GW), BF16), pltpu.VMEM((L_LAT, GW), BF16),
                        pltpu.VMEM((HEADS, NA_DR - 1, GRID_W, 2 * GRID_W), F32)],
        compiler_params=_cparams(("arbitrary",)),
        name="attention",
    )(parts, parts, parts, parts, parts, ck, cv, rpb.reshape(HEADS * NA_DR, NA_DC))


def _rope(x, cos, sin_signed):
    lane = _iota(x.shape, 1)
    partner = jnp.where(lane % 32 < 16, pltpu.roll(x, GW - 16, 1), pltpu.roll(x, 16, 1))
    return x * cos + partner * sin_signed


def _seq_step(s):
    is_lat = s >= LAT_STEP0
    u = jnp.maximum(s - LAT_STEP0, 0)
    b = u // LAT_TILES
    t = u % LAT_TILES
    bwd = jnp.where(is_lat, LAT_STEP0 + b * LAT_TILES + (LAT_TILES - 1 - t), s)
    return is_lat, b, t, s, bwd


def _ret_kernel(qf_ref, kf_ref, vf_ref, qb_ref, kb_ref, vb_ref, cf_ref, sf_ref, cb_ref, sb_ref, lg_ref, s0_ref,
                of_ref, ob_ref, st_ref, sf_scr, sb_scr, w_scr, dq_scr, dk_scr):
    s = pl.program_id(0)
    is_lat, _, t, _, _ = _seq_step(s)
    n = SEQ_TILE
    blockdiag = _iota((GW, GW), 0) // HD == _iota((GW, GW), 1) // HD
    lg = _log_sigmoid(lg_ref[...])

    @pl.when(s == 0)
    def _():
        ti = _iota((n, GW), 0).astype(F32)
        i_ = _iota((n, n), 0).astype(F32)
        j_ = _iota((n, n), 1).astype(F32)
        for d in range(2):
            lgd = lg[d:d + 1, :]
            diff = (i_ - j_) if d == 0 else (j_ - i_)
            pos = ti if d == 0 else (n - 1.0) - ti
            w_scr[d] = jnp.concatenate(
                [jnp.where(diff >= 0, jnp.exp(jnp.maximum(diff, 0.0) * lgd[:, h * HD:h * HD + 1]), 0.0)
                 for h in range(HEADS)], axis=0)
            dq_scr[d] = jnp.exp((pos + 1.0) * lgd)
            dk_scr[d] = jnp.exp((n - 1.0 - pos) * lgd)

    @pl.when(jnp.logical_and(is_lat, t == 0))
    def _():
        for d, scr in ((0, sf_scr), (1, sb_scr)):
            s0 = jnp.concatenate([s0_ref[0, d]] * HEADS, axis=1)
            scr[...] = jnp.where(blockdiag, s0, 0.0)

    def stream(d, q_ref, k_ref, v_ref, c_ref, s_ref, o_ref, scr, latent):
        q = q_ref[...] * (RET_DK ** -0.5)
        k = k_ref[...]
        if latent:
            q = _rope(q, c_ref[...], s_ref[...])
            k = _rope(k, c_ref[...], s_ref[...])
        vb16 = v_ref[...].astype(BF16)
        yield
        a = _dot_nt(_expand_heads(q.astype(BF16), HD), k.astype(BF16)) * w_scr[d]
        yield
        o = _extract_heads(_dot(a.astype(BF16), vb16), n)
        yield
        if latent:
            s_old = scr[...]
            o = o + _dot(q.astype(BF16), s_old.astype(BF16)) * dq_scr[d]
        o_ref[...] = o
        yield
        upd = jnp.where(blockdiag, _dot_tn((k * dk_scr[d]).astype(BF16), vb16), 0.0)
        scr[...] = s_old * jnp.exp(float(n) * lg[d:d + 1, :]) + upd if latent else upd

    for latent in (False, True):
        @pl.when(is_lat if latent else jnp.logical_not(is_lat))
        def _():
            _interleave(stream(0, qf_ref, kf_ref, vf_ref, cf_ref, sf_ref, of_ref, sf_scr, latent),
                        stream(1, qb_ref, kb_ref, vb_ref, cb_ref, sb_ref, ob_ref, sb_scr, latent))

    @pl.when(jnp.logical_not(is_lat))
    def _():
        for d, scr in ((0, sf_scr), (1, sb_scr)):
            st = scr[...]
            st_ref[0, d] = st[:, 0:HD] + st[:, HD:2 * HD] + st[:, 2 * HD:3 * HD] + st[:, 3 * HD:4 * HD]


def _retention(parts, lg_lanes, rope_tabs, s0):
    n = SEQ_TILE
    fwd = lambda s: _seq_step(s)[3]
    bwd = lambda s: _seq_step(s)[4]
    in_specs = [pl.BlockSpec((n, GW), lambda s, c=c: (fwd(s), c)) for c in (CB_RQ, CB_RK, CB_RV)]
    in_specs += [pl.BlockSpec((n, GW), lambda s, c=c: (bwd(s), c)) for c in (CB_RQ, CB_RK, CB_RV)]
    in_specs += [pl.BlockSpec((n, GW), lambda s: (_seq_step(s)[2], 0))] * 2
    in_specs += [pl.BlockSpec((n, GW), lambda s: (LAT_TILES - 1 - _seq_step(s)[2], 0))] * 2
    in_specs += [pl.BlockSpec((2, GW), lambda s: (0, 0)),
                 pl.BlockSpec((1, 2, GW, HD), lambda s: (_seq_step(s)[1], 0, 0, 0))]
    return pl.pallas_call(
        _ret_kernel,
        grid=(SEQ_STEPS,),
        in_specs=in_specs,
        out_specs=[
            pl.BlockSpec((n, GW), lambda s: (fwd(s), 0)),
            pl.BlockSpec((n, GW), lambda s: (bwd(s), 0)),
            pl.BlockSpec((1, 2, GW, HD), lambda s: (jnp.minimum(s, B_CTX - 1), 0, 0, 0)),
        ],
        out_shape=[
            jax.ShapeDtypeStruct((T, GW), F32),
            jax.ShapeDtypeStruct((T, GW), F32),
            jax.ShapeDtypeStruct((B_CTX, 2, GW, HD), F32),
        ],
        scratch_shapes=[pltpu.VMEM((GW, GW), F32), pltpu.VMEM((GW, GW), F32),
                        pltpu.VMEM((2, HEADS * n, n), F32), pltpu.VMEM((2, n, GW), F32),
                        pltpu.VMEM((2, n, GW), F32)],
        compiler_params=_cparams(("arbitrary",)),
        name="retention",
    )(parts, parts, parts, parts, parts, parts, rope_tabs[0], rope_tabs[1], rope_tabs[0], rope_tabs[1],
      lg_lanes, s0)


GLA_QK = HEADS * GLA_DK
N_SUB = GLA_CHUNK // GLA_SUB
GLA_SAFE_DECAY = 60.0


def _gla_tile(q, k, v, la, st, rev):
    n, c = SEQ_TILE, GLA_CHUNK
    n_chunks = n // c
    ri = _iota((n, n), 0)
    ci = _iota((n, n), 1)
    same_chunk = ri // c == ci // c
    causal = (ci >= ri) if rev else (ci <= ri)
    tri = jnp.where(same_chunk, jnp.where(causal, 1.0, 0.0), 0.0).astype(BF16)
    b = _dot_exact01(tri, la)
    yield

    def rows_of(idx, count):
        if idx is None:
            return jnp.zeros((count, GLA_QK), F32)
        return jnp.broadcast_to(b[idx:idx + 1, :], (count, GLA_QK))

    def ref_row(cc, s):
        if rev:
            return cc * c + (s + 1) * GLA_SUB if s < N_SUB - 1 else None
        return cc * c + s * GLA_SUB - 1 if s > 0 else None

    end_rows = [cc * c if rev else cc * c + c - 1 for cc in range(n_chunks)]
    own_ref = jnp.concatenate([rows_of(ref_row(cc, s), GLA_SUB) for cc in range(n_chunks) for s in range(N_SUB)],
                              axis=0)
    b_end = jnp.concatenate([rows_of(r, c) for r in end_rows], axis=0)
    sub = (_iota((n, GLA_QK), 0) // GLA_SUB) % N_SUB
    qh = q * jnp.exp(b - own_ref)
    q_parts, k_parts = [], []
    for s in range(N_SUB):
        kvalid = (sub >= s) if rev else (sub <= s)
        ref_s = jnp.concatenate([rows_of(ref_row(cc, s), c) for cc in range(n_chunks)], axis=0)
        q_parts.append(jnp.where(sub == s, qh, 0.0))
        k_parts.append(jnp.where(kvalid, k * jnp.exp(jnp.minimum(ref_s - b, GLA_SAFE_DECAY)), 0.0))
    q_cat = _expand_heads(jnp.concatenate(q_parts, axis=1).astype(BF16), GLA_DK)
    k_cat = jnp.concatenate(k_parts, axis=1).astype(BF16)
    yield
    a = _dot_nt(q_cat, k_cat)
    yield
    qi = _iota((HEADS * n, n), 0) % n
    kj = _iota((HEADS * n, n), 1)
    keep = (qi // c == kj // c) & ((kj >= qi) if rev else (kj <= qi))
    vb16 = v.astype(BF16)
    o_intra = _extract_heads(_dot(jnp.where(keep, a, 0.0).astype(BF16), vb16), n)
    yield

    qt = (q * jnp.exp(b)).astype(BF16)
    kt = (k * jnp.exp(b_end - b)).astype(BF16)
    blockdiag = _iota((GW, GLA_QK), 0) // HD == _iota((GW, GLA_QK), 1) // GLA_DK
    upd = [jnp.where(blockdiag, _dot_tn(vb16[cc * c:(cc + 1) * c], kt[cc * c:(cc + 1) * c]), 0.0)
           for cc in range(n_chunks)]
    yield
    o_inter = [None] * n_chunks
    for cc in (reversed(range(n_chunks)) if rev else range(n_chunks)):
        o_inter[cc] = _dot_nt(qt[cc * c:(cc + 1) * c], st.astype(BF16))
        st = st * jnp.exp(b[end_rows[cc]:end_rows[cc] + 1, :]) + upd[cc]
        yield
    return o_intra + jnp.concatenate(o_inter, axis=0), st


def _gla_chunk(q, k, v, la, st, rev):
    c = GLA_CHUNK
    ri = _iota((c, c), 0)
    ci = _iota((c, c), 1)
    tri = jnp.where((ci >= ri) if rev else (ci <= ri), 1.0, 0.0).astype(BF16)
    b = _dot_exact01(tri, la)
    b_end = b[0:1, :] if rev else b[c - 1:c, :]
    row = _iota((c, GLA_QK), 0)
    sub = row // GLA_SUB
    off = row % GLA_SUB

    o = _dot_nt((q * jnp.exp(b)).astype(BF16), st.astype(BF16))
    kt = (k * jnp.exp(b_end - b)).astype(BF16)
    lane_h = _iota((GW, GLA_QK), 1) // GLA_DK
    row_h = _iota((GW, GLA_QK), 0) // HD
    st_new = st * jnp.exp(b_end) + jnp.where(row_h == lane_h, _dot_tn(v.astype(BF16), kt), 0.0)

    q_parts, k_parts = [], []
    for s in range(1, N_SUB):
        if rev:
            qsub, brow = N_SUB - 1 - s, b[(N_SUB - s) * GLA_SUB:(N_SUB - s) * GLA_SUB + 1, :]
            kvalid = sub > qsub
        else:
            qsub, brow = s, b[s * GLA_SUB - 1:s * GLA_SUB, :]
            kvalid = sub < qsub
        q_parts.append(jnp.where(sub == qsub, q * jnp.exp(jnp.where(sub == qsub, b - brow, 0.0)), 0.0))
        k_parts.append(jnp.where(kvalid, k * jnp.exp(jnp.where(kvalid, brow - b, 0.0)), 0.0))
    q_cat = _expand_heads(jnp.concatenate(q_parts, axis=1).astype(BF16), GLA_DK)
    k_cat = jnp.concatenate(k_parts, axis=1).astype(BF16)
    a_off = _dot_nt(q_cat, k_cat)
    o = o + _extract_heads(_dot(a_off.astype(BF16), v.astype(BF16)), c)

    red = jnp.where(_iota((GLA_QK, GW), 0) // GLA_DK == _iota((GLA_QK, GW), 1) // HD, 1.0, 0.0).astype(BF16)
    rowv = _iota((c, GW), 0) % GLA_SUB
    for dl in range(GLA_SUB):
        if dl == 0:
            x = q * k
            vs = v
        else:
            sh = dl if not rev else c - dl
            valid = (off + dl < GLA_SUB) if rev else (off >= dl)
            ks = pltpu.roll(k, sh, 0)
            bs = pltpu.roll(b, sh, 0)
            vs = pltpu.roll(v, sh, 0)
            x = jnp.where(valid, q * ks * jnp.exp(jnp.where(valid, b - bs, 0.0)), 0.0)
            validv = (rowv + dl < GLA_SUB) if rev else (rowv >= dl)
            vs = jnp.where(validv, vs, 0.0)
        o = o + _dot(x.astype(BF16), red) * vs
    return o, st_new


def _gla_kernel(qkf_ref, vf_ref, lrf_ref, qkb_ref, vb_ref, lrb_ref, lrf_next_ref, lrb_next_ref, gu_ref, gb_ref,
                s0_ref, of_ref, ob_ref, st_ref, sf_scr, sb_scr, la_scr, decay_scr):
    step = pl.program_id(0)
    slot = step % 2
    is_lat, _, t, _, _ = _seq_step(step)
    lane_h = _iota((GW, GLA_QK), 1) // GLA_DK
    row_h = _iota((GW, GLA_QK), 0) // HD
    blockdiag = row_h == lane_h

    @pl.when(jnp.logical_or(jnp.logical_not(is_lat), t == 0))
    def _():
        for d, scr in ((0, sf_scr), (1, sb_scr)):
            s0t = jnp.concatenate([s0_ref[0, d].T] * HEADS, axis=0)
            scr[...] = jnp.where(blockdiag, s0t, 0.0) * jnp.where(is_lat, 1.0, 0.0)

    c = GLA_CHUNK
    n_chunks = SEQ_TILE // c

    def gates(lrf, lrb, dst):
        zf = _dot(lrf[...].astype(BF16), gu_ref[0].astype(BF16)) + gb_ref[0]
        zb = _dot(lrb[...].astype(BF16), gu_ref[1].astype(BF16)) + gb_ref[1]
        yield
        laf = _log_sigmoid(zf) / GLA_TAU
        lab = _log_sigmoid(zb) / GLA_TAU
        la_scr[dst, 0] = laf
        la_scr[dst, 1] = lab
        yield
        decay_scr[dst] = GLA_SUB * jnp.max(jnp.maximum(-laf, -lab))

    @pl.when(step == 0)
    def _():
        _interleave(gates(lrf_ref, lrb_ref, 0))

    next_gates = lambda: gates(lrf_next_ref, lrb_next_ref, 1 - slot)

    def tile_stream(d, qk_ref, v_ref, la, scr):
        return _gla_tile(qk_ref[:, 0:GLA_QK] * (GLA_DK ** -0.5), qk_ref[:, GLA_QK:2 * GLA_QK],
                         v_ref[...], la, scr[...], rev=(d == 1))

    def chunk_stream(d, qk_ref, v_ref, la, o_ref, scr):
        st = scr[...]
        for cc in (range(n_chunks) if d == 0 else reversed(range(n_chunks))):
            rows = slice(cc * c, (cc + 1) * c)
            q = qk_ref[rows, 0:GLA_QK] * (GLA_DK ** -0.5)
            k = qk_ref[rows, GLA_QK:2 * GLA_QK]
            o, st = _gla_chunk(q, k, v_ref[rows, :], la[rows, :], st, rev=(d == 1))
            o_ref[rows, :] = o
        scr[...] = st

    la_f = la_scr[slot, 0]
    la_b = la_scr[slot, 1]
    decay = decay_scr[slot]

    @pl.when(decay <= GLA_SAFE_DECAY)
    def _():
        (o_f, st_f), (o_b, st_b), _ = _interleave(tile_stream(0, qkf_ref, vf_ref, la_f, sf_scr),
                                                  tile_stream(1, qkb_ref, vb_ref, la_b, sb_scr), next_gates())
        of_ref[...] = o_f
        ob_ref[...] = o_b
        sf_scr[...] = st_f
        sb_scr[...] = st_b

    @pl.when(decay > GLA_SAFE_DECAY)
    def _():
        chunk_stream(0, qkf_ref, vf_ref, la_f, of_ref, sf_scr)
        chunk_stream(1, qkb_ref, vb_ref, la_b, ob_ref, sb_scr)
        _interleave(next_gates())

    @pl.when(jnp.logical_not(is_lat))
    def _():
        for d, scr in ((0, sf_scr), (1, sb_scr)):
            s = scr[...].T
            st_ref[0, d] = s[:, 0:HD] + s[:, HD:2 * HD] + s[:, 2 * HD:3 * HD] + s[:, 3 * HD:4 * HD]


def _gla(parts, gate_up_pad, gate_b, s0):
    n = SEQ_TILE
    in_specs = []
    for blk in (3, 4):
        in_specs += [
            pl.BlockSpec((n, GW), lambda s, blk=blk: (_seq_step(s)[blk], CB_AQK)),
            pl.BlockSpec((n, GW), lambda s, blk=blk: (_seq_step(s)[blk], CB_AV)),
            pl.BlockSpec((n, 128), lambda s, blk=blk: (_seq_step(s)[blk], CB_LR128)),
        ]
    nxt = lambda s: jnp.minimum(s + 1, SEQ_STEPS - 1)
    in_specs += [pl.BlockSpec((n, 128), lambda s, blk=blk: (_seq_step(nxt(s))[blk], CB_LR128)) for blk in (3, 4)]
    in_specs += [pl.BlockSpec((2, 128, GLA_QK), lambda s: (0, 0, 0)),
                 pl.BlockSpec((2, 1, GLA_QK), lambda s: (0, 0, 0)),
                 pl.BlockSpec((1, 2, GLA_QK, HD), lambda s: (_seq_step(s)[1], 0, 0, 0))]
    return pl.pallas_call(
        _gla_kernel,
        grid=(SEQ_STEPS,),
        in_specs=in_specs,
        out_specs=[
            pl.BlockSpec((n, GW), lambda s: (_seq_step(s)[3], 0)),
            pl.BlockSpec((n, GW), lambda s: (_seq_step(s)[4], 0)),
            pl.BlockSpec((1, 2, GLA_QK, HD), lambda s: (jnp.minimum(s, B_CTX - 1), 0, 0, 0)),
        ],
        out_shape=[
            jax.ShapeDtypeStruct((T, GW), F32),
            jax.ShapeDtypeStruct((T, GW), F32),
            jax.ShapeDtypeStruct((B_CTX, 2, GLA_QK, HD), F32),
        ],
        scratch_shapes=[pltpu.VMEM((GW, GLA_QK), F32), pltpu.VMEM((GW, GLA_QK), F32),
                        pltpu.VMEM((2, 2, n, GLA_QK), F32), pltpu.SMEM((2,), F32)],
        compiler_params=_cparams(("arbitrary",)),
        name="gla",
    )(parts, parts, parts, parts, parts, parts, parts, parts, gate_up_pad, gate_b, s0)


OUT_TM = 512
FFN_TF = 1408
FFN_PASSES = D_FF // FFN_TF
OUT_TILES = T // OUT_TM


def _run_in_order(order):
    results = {}
    for gen in order:
        try:
            next(gen)
        except StopIteration as stop:
            results[gen] = stop.value
    return results


def _outproj_ffn_kernel(*refs, n_x, n_out):
    x_refs = refs[:n_x]
    (mod_ref, modp_ref, gpost_ref, gpre2_ref, gpost2_ref, pool_ref, na_ref, rf_ref, rb_ref, rg_ref, af_ref, ab_ref,
     ag_ref, ng_ref, w_ref, wg_ref, wu_ref, wd_ref) = refs[n_x:n_x + 18]
    o_refs = refs[n_x + 18:n_x + 18 + n_out]
    x1_scr, hb_scr = refs[n_x + 18 + n_out:]
    i = pl.program_id(0)
    wslot = i % 2
    rslot = 1 - wslot

    @pl.when(i == 0)
    def _():
        x1_scr[1] = jnp.zeros((OUT_TM, D), F32)
        hb_scr[1] = jnp.zeros((OUT_TM, D), BF16)

    avg = jnp.where(_iota((GW, GW), 0) // HD == _iota((GW, GW), 1) // HD, 1.0 / HD, 0.0).astype(BF16)
    x_tile = _x_tile(x_refs, OUT_TM, jnp.minimum(i, OUT_TILES - 1))

    def mixer_tail(rows):
        r = rf_ref[rows, :] + rb_ref[rows, :]
        r = r - _head_mean(r, avg)
        r = r * lax.rsqrt(_head_mean(r * r, avg) + GN_EPS) * _silu(rg_ref[rows, :])
        a = af_ref[rows, :] + ab_ref[rows, :]
        a = a * lax.rsqrt(_head_mean(a * a, avg) + RMS_EPS) * ng_ref[...] * _silu(ag_ref[rows, :])
        yield
        y = _dot(pool_ref[rows, :].astype(BF16), w_ref[0, 0:GW, :])
        y = y + _dot(na_ref[rows, :].astype(BF16), w_ref[0, GW:2 * GW, :])
        y = y + _dot(r.astype(BF16), w_ref[0, 2 * GW:3 * GW, :])
        y = y + _dot(a.astype(BF16), w_ref[0, 3 * GW:4 * GW, :])
        yield
        x1 = x_tile[rows] + mod_ref[0, 2:3, :] * _rms(y, gpost_ref[...])
        x1_scr[wslot, rows, :] = x1
        hb_scr[wslot, rows, :] = (_rms(x1, gpre2_ref[...]) * (1.0 + mod_ref[0, 4:5, :])
                                  + mod_ref[0, 3:4, :]).astype(BF16)

    def ffn(rows):
        hb = hb_scr[rslot, rows, :]
        y = None
        for j in range(FFN_PASSES):
            cols = slice(j * FFN_TF, (j + 1) * FFN_TF)
            act = (_silu(_dot(hb, wg_ref[0, :, cols])) * _dot(hb, wu_ref[0, :, cols])).astype(BF16)
            yield
            part = _dot(act, wd_ref[0, cols, :])
            y = part if y is None else y + part
            yield
        return x1_scr[rslot, rows, :] + modp_ref[0, 5:6, :] * _rms(y, gpost2_ref[...])

    half = OUT_TM // 2
    fa, fb = ffn(slice(0, half)), ffn(slice(half, OUT_TM))
    ta, tb = mixer_tail(slice(0, half)), mixer_tail(slice(half, OUT_TM))
    done = _run_in_order([fa, ta, fa, ta, fa, ta, fa, fb, tb, fa, fb, tb, fb, tb, fb, fb])
    x2 = jnp.concatenate([done[fa], done[fb]], axis=0)
    if len(o_refs) == 1:
        o_refs[0][...] = x2
    else:
        is_ctx = i - 1 < T_CTX // OUT_TM

        @pl.when(is_ctx)
        def _():
            o_refs[0][...] = x2

        @pl.when(jnp.logical_not(is_ctx))
        def _():
            o_refs[1][...] = x2


def _outproj_ffn(x, mod_l, g_post, g_pre2, g_post2, o_pool, o_na, ret_f, ret_b, gla_f, gla_b, parts, ng_lanes,
                 w_out_b, wg, wu, wd, layer, split_out):
    tm = OUT_TM
    cur = lambda i: jnp.minimum(i, OUT_TILES - 1)
    prev = lambda i: jnp.maximum(i - 1, 0)
    act = pl.BlockSpec((tm, GW), lambda i: (cur(i), 0))
    vec = pl.BlockSpec((1, D), lambda i: (0, 0))
    once = pl.Buffered(1)
    x_specs, x_args = _x_specs(x, tm, cur)
    if split_out:
        out_specs, _ = _x_specs((None, None), tm, prev)
        out_shape = [jax.ShapeDtypeStruct((T_CTX, D), F32), jax.ShapeDtypeStruct((T_LAT, D), F32)]
    else:
        out_specs = [pl.BlockSpec((tm, D), lambda i: (prev(i), 0))]
        out_shape = [jax.ShapeDtypeStruct((T, D), F32)]
    out = pl.pallas_call(
        functools.partial(_outproj_ffn_kernel, n_x=len(x_args), n_out=len(out_shape)),
        grid=(OUT_TILES + 1,),
        in_specs=x_specs + [
            pl.BlockSpec((1, 6, D), lambda i: (_mod_row(cur(i), tm), 0, 0)),
            pl.BlockSpec((1, 6, D), lambda i: (_mod_row(prev(i), tm), 0, 0)),
            vec, vec, vec,
            act, act, act, act,
            pl.BlockSpec((tm, GW), lambda i: (cur(i), CB_RG)),
            act, act,
            pl.BlockSpec((tm, GW), lambda i: (cur(i), CB_AG)),
            pl.BlockSpec((1, GW), lambda i: (0, 0)),
            pl.BlockSpec((1, D, D), lambda i: (layer, 0, 0), pipeline_mode=once),
            pl.BlockSpec((1, D, D_FF), lambda i: (layer, 0, 0), pipeline_mode=once),
            pl.BlockSpec((1, D, D_FF), lambda i: (layer, 0, 0), pipeline_mode=once),
            pl.BlockSpec((1, D_FF, D), lambda i: (layer, 0, 0), pipeline_mode=once),
        ],
        out_specs=out_specs,
        out_shape=out_shape,
        scratch_shapes=[pltpu.VMEM((2, tm, D), F32), pltpu.VMEM((2, tm, D), BF16)],
        compiler_params=_cparams(("arbitrary",)),
        name="outproj_ffn",
    )(*x_args, mod_l, mod_l, g_post, g_pre2, g_post2, o_pool, o_na, ret_f, ret_b, parts, gla_f, gla_b, parts, ng_lanes,
      w_out_b, wg, wu, wd)
    return tuple(out) if split_out else out[0]


def _rope_tables():
    nf = 16
    inv = (ROPE_BASE ** (-np.arange(nf, dtype=np.float32) / nf)).astype(np.float32)
    tok = np.arange(L_LAT)
    cos = np.zeros((L_LAT, HD), np.float32)
    sin = np.zeros((L_LAT, HD), np.float32)
    for axis, pos in enumerate((tok // GRID_W, tok % GRID_W)):
        ang = pos.astype(np.float32)[:, None] * inv[None, :]
        c, s = np.cos(ang), np.sin(ang)
        cos[:, axis * 32:axis * 32 + 32] = np.concatenate([c, c], axis=1)
        sin[:, axis * 32:axis * 32 + 32] = np.concatenate([-s, s], axis=1)
    return jnp.asarray(np.tile(cos, (1, HEADS))), jnp.asarray(np.tile(sin, (1, HEADS)))


def _block_diag(w):
    g, c, _ = w.shape
    out = jnp.zeros((g * c, g * c), w.dtype)
    for i in range(g):
        out = out.at[i * c:(i + 1) * c, i * c:(i + 1) * c].set(w[i])
    return out


def kernel(x_prompt, x_sample, cache_na_k, cache_na_v, state_ret, state_gla, c, c_ctx, w_mod, b_mod,
           g_pre_mix, g_post_mix, g_pre_ffn, g_post_ffn, w_in, w_out, pool_w, pool_scale, na_rpb,
           ret_decay_logit, gla_gate_up, gla_gate_b, gla_norm_g, w_ffn_gate, w_ffn_up, w_ffn_down):
    x = (x_prompt.reshape(T_CTX, D), x_sample.reshape(T_LAT, D))
    cv8 = jnp.concatenate([c_ctx[None, :], c, jnp.zeros((8 - 1 - B_LAT, D), F32)], axis=0)
    mods = _modulation(cv8, w_mod, b_mod).reshape(DEPTH, 8, 6, D)

    w_main_b = w_in[:, :, :P_MAIN].astype(BF16)
    w_lr_b = jnp.pad(w_in[:, :, P_MAIN:], ((0, 0), (0, 0), (0, P_PAD - P_IN))).astype(BF16)
    w_out_b = w_out.astype(BF16)
    wg_b, wu_b, wd_b = w_ffn_gate.astype(BF16), w_ffn_up.astype(BF16), w_ffn_down.astype(BF16)
    gate_up_pad = jnp.pad(gla_gate_up, ((0, 0), (0, 0), (0, 128 - GLA_LOWRANK), (0, 0)))
    rope_tabs = _rope_tables()
    ck = cache_na_k.reshape(B_LAT, DEPTH, PAST, GW)
    cv = cache_na_v.reshape(B_LAT, DEPTH, PAST, GW)
    s0_ret = state_ret.reshape(B_LAT, DEPTH, 2, GW, HD)
    s0_gla = state_gla.reshape(B_LAT, DEPTH, 2, GLA_QK, HD)

    ks, vs, srs, sgs = [], [], [], []
    for l in range(DEPTH):
        mod_l = mods[l]
        parts = _inproj(x, mod_l, g_pre_mix[l][None, :], w_main_b, w_lr_b, l)
        o_pool = _pool(parts, _block_diag(pool_w[l]).astype(BF16), pool_scale[l][None, :])
        o_na, k_ctx, v_ctx = _attention(parts, ck, cv, na_rpb[l], l)
        lg_lanes = jnp.repeat(ret_decay_logit[l], HD, axis=1)
        rf, rb, s_ret = _retention(parts, lg_lanes, rope_tabs, s0_ret[:, l])
        gf, gbw, s_gla = _gla(parts, gate_up_pad[l], gla_gate_b[l][:, None, :], s0_gla[:, l])
        x = _outproj_ffn(x, mod_l, g_post_mix[l][None, :], g_pre_ffn[l][None, :], g_post_ffn[l][None, :],
                         o_pool, o_na, rf, rb, gf, gbw, parts, jnp.tile(gla_norm_g[l], HEADS)[None, :],
                         w_out_b, wg_b, wu_b, wd_b, l, split_out=(l == DEPTH - 1))
        ks.append(k_ctx.reshape(B_CTX, L_CTX, HEADS, HD))
        vs.append(v_ctx.reshape(B_CTX, L_CTX, HEADS, HD))
        srs.append(s_ret.reshape(B_CTX, 2, HEADS, RET_DK, HD))
        sgs.append(s_gla.reshape(B_CTX, 2, HEADS, GLA_DK, HD))

    return (x[0].reshape(B_CTX, L_CTX, D), x[1].reshape(B_LAT, L_LAT, D),
            jnp.stack(ks, axis=1), jnp.stack(vs, axis=1), jnp.stack(srs, axis=1), jnp.stack(sgs, axis=1))
```

```python
import functools

import numpy as np
import jax
import jax.numpy as jnp
from jax import lax
from jax.experimental import pallas as pl
from jax.experimental.pallas import tpu as pltpu

F32 = jnp.float32
BF16 = jnp.bfloat16

D = 1024
B_CTX, L_CTX = 32, 256
B_LAT, L_LAT = 2, 4096
DEPTH = 4
PAST = 256
GRID_W = 64
GRID_H = L_LAT // GRID_W
T_CTX = B_CTX * L_CTX
T_LAT = B_LAT * L_LAT
T = T_CTX + T_LAT
GW = 256
HEADS = 4
HD = 64
POOL_WINDOWS = (2, 4, 8, 16)
NA_ROWS, NA_COLS = 8, 16
RET_DK = 64
GLA_DK = 32
GLA_LOWRANK = 16
GLA_TAU = 16.0
D_FF = 2816
P_IN = 2832
P_PAD = 2944
P_MAIN = 2816
ROPE_BASE = 10000.0
RMS_EPS = 1e-6
GN_EPS = 1e-5
NEG = -1e30

CB_POOL, CB_NAQ, CB_NAK, CB_NAV, CB_RQ, CB_RK, CB_RV, CB_RG, CB_AQK, CB_AV, CB_AG = range(11)
CB_LR128 = P_IN // 128

SEQ_TILE = L_CTX
SEQ_SUBS = 2
SEQ_BLOCK = SEQ_SUBS * SEQ_TILE
SEQ_STEPS = T // SEQ_BLOCK
LAT_STEP0 = T_CTX // SEQ_BLOCK
LAT_BLOCKS = L_LAT // SEQ_BLOCK
GLA_CHUNK = 64
GLA_SUB = 16
VMEM_LIMIT = 56 * 1024 * 1024


def _cparams(sem):
    return pltpu.CompilerParams(dimension_semantics=sem, vmem_limit_bytes=VMEM_LIMIT)


def _silu(x):
    return x / (1.0 + jnp.exp(-x))


def _log_sigmoid(z):
    return jnp.minimum(z, 0.0) - jnp.log1p(jnp.exp(-jnp.abs(z)))


def _rms(x, g):
    return x * lax.rsqrt(jnp.mean(x * x, axis=-1, keepdims=True) + RMS_EPS) * g


def _dot(a, b):
    return jnp.dot(a, b, preferred_element_type=F32)


def _dot_nt(a, b):
    return lax.dot_general(a, b, (((1,), (1,)), ((), ())), preferred_element_type=F32)


def _dot_tn(a, b):
    return lax.dot_general(a, b, (((0,), (0,)), ((), ())), preferred_element_type=F32)


def _split_hi_lo(x):
    hi = x.astype(BF16)
    return hi, (x - hi.astype(F32)).astype(BF16)


def _dot_exact01(a01, x):
    hi, lo = _split_hi_lo(x)
    return _dot(a01, hi) + _dot(a01, lo)


def _iota(shape, dim):
    return lax.broadcasted_iota(jnp.int32, shape, dim)


def _expand_heads(x, head_w):
    n, w = x.shape
    xe = jnp.concatenate([x] * HEADS, axis=0)
    rowh = _iota((HEADS * n, w), 0) // n
    laneh = (_iota((HEADS * n, w), 1) // head_w) % HEADS
    return jnp.where(rowh == laneh, xe, jnp.zeros_like(xe))


def _extract_heads(p, n):
    laneh = _iota((n, GW), 1) // HD
    out = p[0:n]
    for h in range(1, HEADS):
        out = jnp.where(laneh == h, p[h * n:(h + 1) * n], out)
    return out


def _head_mean(x, avg):
    hi, lo = _split_hi_lo(x)
    return _dot(hi, avg) + _dot(lo, avg)


def _interleave(*gens):
    results = [None] * len(gens)
    live = list(range(len(gens)))
    while live:
        for i in list(live):
            try:
                next(gens[i])
            except StopIteration as stop:
                results[i] = stop.value
                live.remove(i)
    return results


def _mod_row(i, tm):
    return jnp.where(i < T_CTX // tm, 0, 1 + (i * tm - T_CTX) // L_LAT)


def _mod_kernel(cv_ref, w_ref, b_ref, o_ref):
    s = _silu(cv_ref[...]).astype(BF16)
    o_ref[0] = _dot(s, w_ref[0].astype(BF16)) + b_ref[0]


def _modulation(cv8, w_mod, b_mod):
    tn = 1536
    return pl.pallas_call(
        _mod_kernel,
        grid=(DEPTH, 6 * D // tn),
        in_specs=[
            pl.BlockSpec((8, D), lambda l, j: (0, 0)),
            pl.BlockSpec((1, D, tn), lambda l, j: (l, 0, j)),
            pl.BlockSpec((1, 1, tn), lambda l, j: (l, 0, j)),
        ],
        out_specs=pl.BlockSpec((1, 8, tn), lambda l, j: (l, 0, j)),
        out_shape=jax.ShapeDtypeStruct((DEPTH, 8, 6 * D), F32),
        compiler_params=_cparams(("arbitrary", "arbitrary")),
        name="modulation",
    )(cv8, w_mod, b_mod.reshape(DEPTH, 1, 6 * D))


IN_TM = 512


def _x_specs(x, tm, tile_of=lambda i: i):
    if not isinstance(x, tuple):
        return [pl.BlockSpec((tm, D), lambda i: (tile_of(i), 0))], [x]
    nc = T_CTX // tm
    return ([pl.BlockSpec((tm, D), lambda i: (jnp.minimum(tile_of(i), nc - 1), 0)),
             pl.BlockSpec((tm, D), lambda i: (jnp.maximum(tile_of(i) - nc, 0), 0))], list(x))


def _x_tile(x_refs, tm, tile):
    if len(x_refs) == 1:
        return x_refs[0][...]
    return jnp.where(tile < T_CTX // tm, x_refs[0][...], x_refs[1][...])


def _inproj_kernel(*refs, n_x):
    x_refs, (mod_ref, g_ref, w_ref, wlr_ref, o_ref) = refs[:n_x], refs[n_x:]
    x = _x_tile(x_refs, IN_TM, pl.program_id(0))

    def half_tile(rows):
        h = _rms(x[rows], g_ref[...]) * (1.0 + mod_ref[0, 1:2, :]) + mod_ref[0, 0:1, :]
        hb = h.astype(BF16)
        yield
        for a in range(0, P_MAIN, 1024):
            b = min(a + 1024, P_MAIN)
            o_ref[rows, a:b] = _dot(hb, w_ref[0, :, a:b])
            yield
        o_ref[rows, P_MAIN:P_PAD] = _dot(hb, wlr_ref[0])

    half = IN_TM // 2
    first, second = half_tile(slice(0, half)), half_tile(slice(half, IN_TM))
    next(first)
    _interleave(first, second)


def _inproj(x, mod_l, g_pre, w_main_b, w_lr_b, layer):
    tm = IN_TM
    x_specs, x_args = _x_specs(x, tm)
    return pl.pallas_call(
        functools.partial(_inproj_kernel, n_x=len(x_args)),
        grid=(T // tm,),
        in_specs=x_specs + [
            pl.BlockSpec((1, 6, D), lambda i: (_mod_row(i, tm), 0, 0)),
            pl.BlockSpec((1, D), lambda i: (0, 0)),
            pl.BlockSpec((1, D, P_MAIN), lambda i: (layer, 0, 0), pipeline_mode=pl.Buffered(1)),
            pl.BlockSpec((1, D, P_PAD - P_MAIN), lambda i: (layer, 0, 0), pipeline_mode=pl.Buffered(1)),
        ],
        out_specs=pl.BlockSpec((tm, P_PAD), lambda i: (i, 0)),
        out_shape=jax.ShapeDtypeStruct((T, P_PAD), F32),
        compiler_params=_cparams(("arbitrary",)),
        name="inproj",
    )(*x_args, mod_l, g_pre, w_main_b, w_lr_b)


POOL_TM = 512
POOL_SUB = 256


def _pool_kernel(v_ref, w_ref, scale_ref, o_ref, band_scr, cnt_scr):
    i = pl.program_id(0)
    n = POOL_SUB

    @pl.when(i == 0)
    def _():
        t = _iota((n, n), 0)
        s = _iota((n, n), 1)
        lane_g = _iota((n, GW), 1) // HD
        for kind, seg_len in enumerate((L_CTX, GRID_W)):
            seg0 = t & ~(seg_len - 1)
            seg1 = seg0 + seg_len
            cnt = jnp.zeros((n, GW), F32)
            for gi, win in enumerate(POOL_WINDOWS):
                lo = jnp.maximum(t - win // 2, seg0)
                hi = jnp.minimum(t - win // 2 + win, seg1)
                band_scr[kind, gi] = jnp.where(s >= lo, jnp.where(s < hi, 1.0, 0.0), 0.0).astype(BF16)
                cnt = jnp.where(lane_g == gi, (hi - lo).astype(F32), cnt)
            cnt_scr[kind] = cnt

    kind = jnp.where(i < T_CTX // POOL_TM, 0, 1)

    def piece(rows):
        v = v_ref[rows, :]
        vh, vl = _split_hi_lo(v)
        lane_g = _iota((n, GW), 1) // HD
        yield
        mean = None
        for gi in range(len(POOL_WINDOWS)):
            band = band_scr[kind, gi]
            m = _dot(band, vh) + _dot(band, vl)
            mean = m if mean is None else jnp.where(lane_g == gi, m, mean)
        yield
        d = (mean / cnt_scr[kind] - v).astype(BF16)
        o_ref[rows, :] = (_dot(d, w_ref[...]) * scale_ref[...]).astype(o_ref.dtype)

    _interleave(*[piece(slice(j * n, (j + 1) * n)) for j in range(POOL_TM // n)])


def _pool(parts, w_bd, scale):
    tm = POOL_TM
    n_win = len(POOL_WINDOWS)
    return pl.pallas_call(
        _pool_kernel,
        grid=(T // tm,),
        in_specs=[
            pl.BlockSpec((tm, GW), lambda i: (i, CB_POOL)),
            pl.BlockSpec((GW, GW), lambda i: (0, 0)),
            pl.BlockSpec((1, GW), lambda i: (0, 0)),
        ],
        out_specs=pl.BlockSpec((tm, GW), lambda i: (i, 0)),
        out_shape=jax.ShapeDtypeStruct((T, GW), BF16),
        scratch_shapes=[pltpu.VMEM((2, n_win, POOL_SUB, POOL_SUB), BF16), pltpu.VMEM((2, POOL_SUB, GW), F32)],
        compiler_params=_cparams(("arbitrary",)),
        name="pool",
    )(parts, w_bd, scale)


def _softmax_rows(s):
    m = jnp.max(s, axis=-1, keepdims=True)
    p = jnp.exp(s - m)
    return p / jnp.sum(p, axis=-1, keepdims=True)


def _ctx_attn_rows(q_ref, k_ref, v_ref, o_ref, rows):
    qe = _expand_heads(q_ref[rows, :].astype(BF16), HD)
    yield
    s = _dot_nt(qe, k_ref[rows, :].astype(BF16)) * (HD ** -0.5)
    yield
    p = _softmax_rows(s).astype(BF16)
    yield
    o_ref[rows, :] = _extract_heads(_dot(p, v_ref[rows, :].astype(BF16)), L_CTX).astype(o_ref.dtype)


NA_ROWS_PER_STEP = 8
NA_INTERLEAVE = 8
ATT_ROWS = NA_ROWS_PER_STEP * GRID_W
ATT_CTX_STEPS = T_CTX // ATT_ROWS
ATT_LAT_STEPS = L_LAT // ATT_ROWS
NA_WIN = NA_ROWS * GRID_W
NA_DR = 2 * NA_ROWS - 1
NA_DC = 2 * NA_COLS - 1


def _na_bias_table(rpb_ref, e2_ref):
    shape = (GRID_W, 2 * GRID_W)
    qc = _iota(shape, 0)
    lane = _iota(shape, 1)
    kc = lane % GRID_W
    upper = lane >= GRID_W
    c0 = jnp.clip(qc - NA_COLS // 2, 0, GRID_W - NA_COLS)
    dc = jnp.where((kc >= c0) & (kc < c0 + NA_COLS), kc - qc + (NA_COLS - 1), -1)

    def one(ha, carry):
        h = ha // (NA_DR - 1)
        a = ha % (NA_DR - 1)
        acc = jnp.full(shape, NEG, F32)
        for j in range(NA_DC):
            val = jnp.where(upper, rpb_ref[h * NA_DR + a + 1, j], rpb_ref[h * NA_DR + a, j])
            acc = jnp.where(dc == j, val, acc)
        e2_ref[h, a] = acc
        return carry

    lax.fori_loop(0, HEADS * (NA_DR - 1), one, 0)


def _attn_kernel(q_ref, k_ref, v_ref, kseq_ref, vseq_ref, ck_ref, cv_ref, rpb_ref, kprev_ref, vprev_ref,
                 o_ref, ko_ref, vo_ref, kb_ref, vb_ref, e2_ref):
    del kprev_ref, vprev_ref
    s = pl.program_id(0)

    @pl.when(s < ATT_CTX_STEPS)
    def _():
        for i in range(ATT_ROWS // L_CTX):
            ko_ref[i, 0] = k_ref[i * L_CTX:(i + 1) * L_CTX, :]
            vo_ref[i, 0] = v_ref[i * L_CTX:(i + 1) * L_CTX, :]
        _interleave(*[_ctx_attn_rows(q_ref, k_ref, v_ref, o_ref, slice(i * L_CTX, (i + 1) * L_CTX))
                      for i in range(ATT_ROWS // L_CTX)])

    @pl.when(s >= ATT_CTX_STEPS)
    def _():
        step = (s - ATT_CTX_STEPS) % ATT_LAT_STEPS

        @pl.when(s == ATT_CTX_STEPS)
        def _():
            _na_bias_table(rpb_ref, e2_ref)

        @pl.when(step == 0)
        def _():
            kb_ref[...] = kseq_ref[...].astype(BF16)
            vb_ref[...] = vseq_ref[...].astype(BF16)

        _na_rows(step, q_ref, ck_ref, cv_ref, o_ref, kb_ref, vb_ref, e2_ref)


def _na_rows(step, q_ref, ck_ref, cv_ref, o_ref, kb_ref, vb_ref, e2_ref):
    ckb = ck_ref[0, 0].astype(BF16)
    cvb = cv_ref[0, 0].astype(BF16)
    scale = HD ** -0.5

    def one_row(rr):
        r = step * NA_ROWS_PER_STEP + rr
        r0 = jnp.clip(r - NA_ROWS // 2, 0, GRID_H - NA_ROWS)
        base = r0 - r + (NA_ROWS - 1)
        q0 = pl.multiple_of(rr * GRID_W, GRID_W)
        k0 = pl.multiple_of(r0 * GRID_W, GRID_W)
        qe = _expand_heads(q_ref[pl.ds(q0, GRID_W), :].astype(BF16), HD)
        kw = kb_ref[pl.ds(k0, NA_WIN), :]
        vw = vb_ref[pl.ds(k0, NA_WIN), :]
        bias = jnp.concatenate(
            [jnp.concatenate([e2_ref[h, base + 2 * p] for p in range(NA_ROWS // 2)], axis=1)
             for h in range(HEADS)], axis=0)
        yield
        s_loc = _dot_nt(qe, kw) * scale + bias
        s_ctx = _dot_nt(qe, ckb) * scale
        yield
        m = jnp.maximum(jnp.max(s_loc, axis=-1, keepdims=True), jnp.max(s_ctx, axis=-1, keepdims=True))
        p_loc = jnp.exp(s_loc - m)
        p_ctx = jnp.exp(s_ctx - m)
        inv = 1.0 / (jnp.sum(p_loc, axis=-1, keepdims=True) + jnp.sum(p_ctx, axis=-1, keepdims=True))
        yield
        pv = _dot((p_loc * inv).astype(BF16), vw) + _dot((p_ctx * inv).astype(BF16), cvb)
        yield
        o_ref[pl.ds(q0, GRID_W), :] = _extract_heads(pv, GRID_W).astype(o_ref.dtype)

    def row_group(i, carry):
        _interleave(*[one_row(NA_INTERLEAVE * i + j) for j in range(NA_INTERLEAVE)])
        return carry

    lax.fori_loop(0, NA_ROWS_PER_STEP // NA_INTERLEAVE, row_group, 0)


def _attention(parts, ck, cv, rpb, layer, new_k, new_v):
    lat_req = lambda s: jnp.maximum(s - ATT_CTX_STEPS, 0) // ATT_LAT_STEPS
    seq_blk0 = T_CTX // L_LAT
    per_step = ATT_ROWS // L_CTX
    cache_spec = pl.BlockSpec((per_step, 1, L_CTX, GW), lambda s: (jnp.minimum(s, ATT_CTX_STEPS - 1), layer, 0, 0))
    return pl.pallas_call(
        _attn_kernel,
        grid=(T // ATT_ROWS,),
        in_specs=[
            pl.BlockSpec((ATT_ROWS, GW), lambda s: (s, CB_NAQ)),
            pl.BlockSpec((ATT_ROWS, GW), lambda s: (s, CB_NAK)),
            pl.BlockSpec((ATT_ROWS, GW), lambda s: (s, CB_NAV)),
            pl.BlockSpec((L_LAT, GW), lambda s: (seq_blk0 + lat_req(s), CB_NAK)),
            pl.BlockSpec((L_LAT, GW), lambda s: (seq_blk0 + lat_req(s), CB_NAV)),
            pl.BlockSpec((1, 1, PAST, GW), lambda s: (lat_req(s), layer, 0, 0)),
            pl.BlockSpec((1, 1, PAST, GW), lambda s: (lat_req(s), layer, 0, 0)),
            pl.BlockSpec(memory_space=pltpu.SMEM),
            pl.BlockSpec(memory_space=pl.ANY),
            pl.BlockSpec(memory_space=pl.ANY),
        ],
        out_specs=[pl.BlockSpec((ATT_ROWS, GW), lambda s: (s, 0)), cache_spec, cache_spec],
        out_shape=[jax.ShapeDtypeStruct((T, GW), BF16), jax.ShapeDtypeStruct(new_k.shape, F32),
                   jax.ShapeDtypeStruct(new_v.shape, F32)],
        input_output_aliases={8: 1, 9: 2},
        scratch_shapes=[pltpu.VMEM((L_LAT, GW), BF16), pltpu.VMEM((L_LAT, GW), BF16),
                        pltpu.VMEM((HEADS, NA_DR - 1, GRID_W, 2 * GRID_W), F32)],
        compiler_params=_cparams(("arbitrary",)),
        name="attention",
    )(parts, parts, parts, parts, parts, ck, cv, rpb.reshape(HEADS * NA_DR, NA_DC), new_k, new_v)


def _rope(x, cos, sin_signed):
    lane = _iota(x.shape, 1)
    partner = jnp.where(lane % 32 < 16, pltpu.roll(x, GW - 16, 1), pltpu.roll(x, 16, 1))
    return x * cos + partner * sin_signed


def _seq_step(s):
    is_lat = s >= LAT_STEP0
    u = jnp.maximum(s - LAT_STEP0, 0)
    b = u // LAT_BLOCKS
    t = u % LAT_BLOCKS
    bwd = jnp.where(is_lat, LAT_STEP0 + b * LAT_BLOCKS + (LAT_BLOCKS - 1 - t), s)
    return is_lat, b, t, s, bwd


def _sum_head_blocks(st):
    return st[:, 0:HD] + st[:, HD:2 * HD] + st[:, 2 * HD:3 * HD] + st[:, 3 * HD:4 * HD]


def _ret_kernel(qf_ref, kf_ref, vf_ref, qb_ref, kb_ref, vb_ref, cf_ref, sf_ref, cb_ref, sb_ref, lg_ref, s0_ref,
                st_prev_ref, of_ref, ob_ref, st_ref, sf_scr, sb_scr, w_scr, dq_scr, dk_scr):
    del st_prev_ref
    s = pl.program_id(0)
    is_lat, _, t, _, _ = _seq_step(s)
    n = SEQ_TILE
    blockdiag = _iota((GW, GW), 0) // HD == _iota((GW, GW), 1) // HD
    lg = _log_sigmoid(lg_ref[...])

    @pl.when(s == 0)
    def _():
        ti = _iota((n, GW), 0).astype(F32)
        i_ = _iota((n, n), 0).astype(F32)
        j_ = _iota((n, n), 1).astype(F32)
        for d in range(2):
            lgd = lg[d:d + 1, :]
            diff = (i_ - j_) if d == 0 else (j_ - i_)
            pos = ti if d == 0 else (n - 1.0) - ti
            w_scr[d] = jnp.concatenate(
                [jnp.where(diff >= 0, jnp.exp(jnp.maximum(diff, 0.0) * lgd[:, h * HD:h * HD + 1]), 0.0)
                 for h in range(HEADS)], axis=0)
            dq_scr[d] = jnp.exp((pos + 1.0) * lgd)
            dk_scr[d] = jnp.exp((n - 1.0 - pos) * lgd)

    @pl.when(jnp.logical_and(is_lat, t == 0))
    def _():
        for d, scr in ((0, sf_scr), (1, sb_scr)):
            s0 = jnp.concatenate([s0_ref[0, d]] * HEADS, axis=1)
            scr[...] = jnp.where(blockdiag, s0, 0.0)

    fwd_refs = (qf_ref, kf_ref, vf_ref, cf_ref, sf_ref, of_ref)
    bwd_refs = (qb_ref, kb_ref, vb_ref, cb_ref, sb_ref, ob_ref)

    def tile(d, refs, sub, scr):
        q_ref, k_ref, v_ref, c_ref, s_ref, o_ref = refs
        rows = slice(sub * n, (sub + 1) * n)
        q = q_ref[rows, :] * (RET_DK ** -0.5)
        k = k_ref[rows, :]
        if scr is not None:
            q = _rope(q, c_ref[rows, :], s_ref[rows, :])
            k = _rope(k, c_ref[rows, :], s_ref[rows, :])
        vb16 = v_ref[rows, :].astype(BF16)
        yield
        a = _dot_nt(_expand_heads(q.astype(BF16), HD), k.astype(BF16)) * w_scr[d]
        yield
        o = _extract_heads(_dot(a.astype(BF16), vb16), n)
        yield
        if scr is not None:
            s_old = scr[...]
            o = o + _dot(q.astype(BF16), s_old.astype(BF16)) * dq_scr[d]
        o_ref[rows, :] = o
        yield
        upd = jnp.where(blockdiag, _dot_tn((k * dk_scr[d]).astype(BF16), vb16), 0.0)
        if scr is None:
            st_ref[sub, 0, d] = _sum_head_blocks(upd)
        else:
            scr[...] = s_old * jnp.exp(float(n) * lg[d:d + 1, :]) + upd

    def chain(d, refs, order, scr):
        for sub in order:
            yield from tile(d, refs, sub, scr)

    @pl.when(jnp.logical_not(is_lat))
    def _():
        _interleave(*[tile(d, (fwd_refs, bwd_refs)[d], sub, None) for sub in range(SEQ_SUBS) for d in range(2)])

    @pl.when(is_lat)
    def _():
        _interleave(chain(0, fwd_refs, range(SEQ_SUBS), sf_scr),
                    chain(1, bwd_refs, reversed(range(SEQ_SUBS)), sb_scr))


def _retention(parts, lg_lanes, rope_tabs, s0, layer, new_state):
    n, blk = SEQ_TILE, SEQ_BLOCK
    fwd = lambda s: _seq_step(s)[3]
    bwd = lambda s: _seq_step(s)[4]
    in_specs = [pl.BlockSpec((blk, GW), lambda s, c=c: (fwd(s), c)) for c in (CB_RQ, CB_RK, CB_RV)]
    in_specs += [pl.BlockSpec((blk, GW), lambda s, c=c: (bwd(s), c)) for c in (CB_RQ, CB_RK, CB_RV)]
    in_specs += [pl.BlockSpec((blk, GW), lambda s: (_seq_step(s)[2], 0))] * 2
    in_specs += [pl.BlockSpec((blk, GW), lambda s: (LAT_BLOCKS - 1 - _seq_step(s)[2], 0))] * 2
    in_specs += [pl.BlockSpec((2, GW), lambda s: (0, 0)),
                 pl.BlockSpec((1, 2, GW, HD), lambda s: (_seq_step(s)[1], 0, 0, 0)),
                 pl.BlockSpec(memory_space=pl.ANY)]
    return pl.pallas_call(
        _ret_kernel,
        input_output_aliases={len(in_specs) - 1: 2},
        grid=(SEQ_STEPS,),
        in_specs=in_specs,
        out_specs=[
            pl.BlockSpec((blk, GW), lambda s: (fwd(s), 0)),
            pl.BlockSpec((blk, GW), lambda s: (bwd(s), 0)),
            pl.BlockSpec((SEQ_SUBS, 1, 2, GW, HD), lambda s: (jnp.minimum(s, LAT_STEP0 - 1), layer, 0, 0, 0)),
        ],
        out_shape=[
            jax.ShapeDtypeStruct((T, GW), F32),
            jax.ShapeDtypeStruct((T, GW), F32),
            jax.ShapeDtypeStruct(new_state.shape, F32),
        ],
        scratch_shapes=[pltpu.VMEM((GW, GW), F32), pltpu.VMEM((GW, GW), F32),
                        pltpu.VMEM((2, HEADS * n, n), F32), pltpu.VMEM((2, n, GW), F32),
                        pltpu.VMEM((2, n, GW), F32)],
        compiler_params=_cparams(("arbitrary",)),
        name="retention",
    )(parts, parts, parts, parts, parts, parts, rope_tabs[0], rope_tabs[1], rope_tabs[0], rope_tabs[1],
      lg_lanes, s0, new_state)


GLA_QK = HEADS * GLA_DK
N_SUB = GLA_CHUNK // GLA_SUB
GLA_SAFE_DECAY = 60.0


def _gla_tile(q, k, v, la, st, rev):
    n, c = SEQ_TILE, GLA_CHUNK
    n_chunks = n // c
    ri = _iota((n, n), 0)
    ci = _iota((n, n), 1)
    same_chunk = ri // c == ci // c
    causal = (ci >= ri) if rev else (ci <= ri)
    tri = jnp.where(same_chunk, jnp.where(causal, 1.0, 0.0), 0.0).astype(BF16)
    b = _dot_exact01(tri, la)
    yield

    def rows_of(idx, count):
        if idx is None:
            return jnp.zeros((count, GLA_QK), F32)
        return jnp.broadcast_to(b[idx:idx + 1, :], (count, GLA_QK))

    def ref_row(cc, s):
        if rev:
            return cc * c + (s + 1) * GLA_SUB if s < N_SUB - 1 else None
        return cc * c + s * GLA_SUB - 1 if s > 0 else None

    end_rows = [cc * c if rev else cc * c + c - 1 for cc in range(n_chunks)]
    own_ref = jnp.concatenate([rows_of(ref_row(cc, s), GLA_SUB) for cc in range(n_chunks) for s in range(N_SUB)],
                              axis=0)
    b_end = jnp.concatenate([rows_of(r, c) for r in end_rows], axis=0)
    sub = (_iota((n, GLA_QK), 0) // GLA_SUB) % N_SUB
    qh = q * jnp.exp(b - own_ref)
    q_parts, k_parts = [], []
    for s in range(N_SUB):
        kvalid = (sub >= s) if rev else (sub <= s)
        ref_s = jnp.concatenate([rows_of(ref_row(cc, s), c) for cc in range(n_chunks)], axis=0)
        q_parts.append(jnp.where(sub == s, qh, 0.0))
        k_parts.append(jnp.where(kvalid, k * jnp.exp(jnp.minimum(ref_s - b, GLA_SAFE_DECAY)), 0.0))
    q_cat = _expand_heads(jnp.concatenate(q_parts, axis=1).astype(BF16), GLA_DK)
    k_cat = jnp.concatenate(k_parts, axis=1).astype(BF16)
    yield
    a = _dot_nt(q_cat, k_cat)
    yield
    qi = _iota((HEADS * n, n), 0) % n
    kj = _iota((HEADS * n, n), 1)
    keep = (qi // c == kj // c) & ((kj >= qi) if rev else (kj <= qi))
    vb16 = v.astype(BF16)
    o_intra = _extract_heads(_dot(jnp.where(keep, a, 0.0).astype(BF16), vb16), n)
    yield

    qt = (q * jnp.exp(b)).astype(BF16)
    kt = (k * jnp.exp(b_end - b)).astype(BF16)
    blockdiag = _iota((GW, GLA_QK), 0) // HD == _iota((GW, GLA_QK), 1) // GLA_DK
    upd = [jnp.where(blockdiag, _dot_tn(vb16[cc * c:(cc + 1) * c], kt[cc * c:(cc + 1) * c]), 0.0)
           for cc in range(n_chunks)]
    yield
    o_inter = [None] * n_chunks
    for cc in (reversed(range(n_chunks)) if rev else range(n_chunks)):
        o_inter[cc] = _dot_nt(qt[cc * c:(cc + 1) * c], st.astype(BF16))
        st = st * jnp.exp(b[end_rows[cc]:end_rows[cc] + 1, :]) + upd[cc]
        yield
    return o_intra + jnp.concatenate(o_inter, axis=0), st


def _gla_chunk(q, k, v, la, st, rev):
    c = GLA_CHUNK
    ri = _iota((c, c), 0)
    ci = _iota((c, c), 1)
    tri = jnp.where((ci >= ri) if rev else (ci <= ri), 1.0, 0.0).astype(BF16)
    b = _dot_exact01(tri, la)
    b_end = b[0:1, :] if rev else b[c - 1:c, :]
    row = _iota((c, GLA_QK), 0)
    sub = row // GLA_SUB
    off = row % GLA_SUB

    o = _dot_nt((q * jnp.exp(b)).astype(BF16), st.astype(BF16))
    kt = (k * jnp.exp(b_end - b)).astype(BF16)
    lane_h = _iota((GW, GLA_QK), 1) // GLA_DK
    row_h = _iota((GW, GLA_QK), 0) // HD
    st_new = st * jnp.exp(b_end) + jnp.where(row_h == lane_h, _dot_tn(v.astype(BF16), kt), 0.0)

    q_parts, k_parts = [], []
    for s in range(1, N_SUB):
        if rev:
            qsub, brow = N_SUB - 1 - s, b[(N_SUB - s) * GLA_SUB:(N_SUB - s) * GLA_SUB + 1, :]
            kvalid = sub > qsub
        else:
            qsub, brow = s, b[s * GLA_SUB - 1:s * GLA_SUB, :]
            kvalid = sub < qsub
        q_parts.append(jnp.where(sub == qsub, q * jnp.exp(jnp.where(sub == qsub, b - brow, 0.0)), 0.0))
        k_parts.append(jnp.where(kvalid, k * jnp.exp(jnp.where(kvalid, brow - b, 0.0)), 0.0))
    q_cat = _expand_heads(jnp.concatenate(q_parts, axis=1).astype(BF16), GLA_DK)
    k_cat = jnp.concatenate(k_parts, axis=1).astype(BF16)
    a_off = _dot_nt(q_cat, k_cat)
    o = o + _extract_heads(_dot(a_off.astype(BF16), v.astype(BF16)), c)

    red = jnp.where(_iota((GLA_QK, GW), 0) // GLA_DK == _iota((GLA_QK, GW), 1) // HD, 1.0, 0.0).astype(BF16)
    rowv = _iota((c, GW), 0) % GLA_SUB
    for dl in range(GLA_SUB):
        if dl == 0:
            x = q * k
            vs = v
        else:
            sh = dl if not rev else c - dl
            valid = (off + dl < GLA_SUB) if rev else (off >= dl)
            ks = pltpu.roll(k, sh, 0)
            bs = pltpu.roll(b, sh, 0)
            vs = pltpu.roll(v, sh, 0)
            x = jnp.where(valid, q * ks * jnp.exp(jnp.where(valid, b - bs, 0.0)), 0.0)
            validv = (rowv + dl < GLA_SUB) if rev else (rowv >= dl)
            vs = jnp.where(validv, vs, 0.0)
        o = o + _dot(x.astype(BF16), red) * vs
    return o, st_new


def _gla_kernel(qkf_ref, vf_ref, lrf_ref, qkb_ref, vb_ref, lrb_ref, lrf_next_ref, lrb_next_ref, gu_ref, gb_ref,
                s0_ref, st_prev_ref, of_ref, ob_ref, st_ref, sf_scr, sb_scr, la_scr, decay_scr):
    del st_prev_ref
    step = pl.program_id(0)
    slot = step % 2
    is_lat, _, t, _, _ = _seq_step(step)
    lane_h = _iota((GW, GLA_QK), 1) // GLA_DK
    row_h = _iota((GW, GLA_QK), 0) // HD
    blockdiag = row_h == lane_h

    @pl.when(jnp.logical_and(is_lat, t == 0))
    def _():
        for d, scr in ((0, sf_scr), (1, sb_scr)):
            s0t = jnp.concatenate([s0_ref[0, d].T] * HEADS, axis=0)
            scr[...] = jnp.where(blockdiag, s0t, 0.0)

    n, c = SEQ_TILE, GLA_CHUNK
    n_chunks = n // c

    def gates(lrf, lrb, dst):
        zf = _dot(lrf[...].astype(BF16), gu_ref[0].astype(BF16)) + gb_ref[0]
        zb = _dot(lrb[...].astype(BF16), gu_ref[1].astype(BF16)) + gb_ref[1]
        yield
        laf = _log_sigmoid(zf) / GLA_TAU
        lab = _log_sigmoid(zb) / GLA_TAU
        la_scr[dst, 0] = laf
        la_scr[dst, 1] = lab
        yield
        decay_scr[dst] = GLA_SUB * jnp.max(jnp.maximum(-laf, -lab))

    @pl.when(step == 0)
    def _():
        _interleave(gates(lrf_ref, lrb_ref, 0))

    next_gates = lambda: gates(lrf_next_ref, lrb_next_ref, 1 - slot)

    la = (la_scr[slot, 0], la_scr[slot, 1])
    decay = decay_scr[slot]
    refs = ((qkf_ref, vf_ref, of_ref, sf_scr), (qkb_ref, vb_ref, ob_ref, sb_scr))
    zero_state = jnp.zeros((GW, GLA_QK), F32)

    def tile(d, sub, st):
        qk_ref, v_ref, o_ref, _ = refs[d]
        rows = slice(sub * n, (sub + 1) * n)
        o, st = yield from _gla_tile(qk_ref[rows, 0:GLA_QK] * (GLA_DK ** -0.5), qk_ref[rows, GLA_QK:2 * GLA_QK],
                                     v_ref[rows, :], la[d][rows, :], st, rev=(d == 1))
        o_ref[rows, :] = o
        return st

    def tile_any_gates(d, sub, st):
        qk_ref, v_ref, o_ref, _ = refs[d]
        for cc in (range(n_chunks) if d == 0 else reversed(range(n_chunks))):
            rows = slice(sub * n + cc * c, sub * n + (cc + 1) * c)
            q = qk_ref[rows, 0:GLA_QK] * (GLA_DK ** -0.5)
            k = qk_ref[rows, GLA_QK:2 * GLA_QK]
            o, st = _gla_chunk(q, k, v_ref[rows, :], la[d][rows, :], st, rev=(d == 1))
            o_ref[rows, :] = o
            yield
        return st

    def context_request(method, d, sub):
        st = yield from method(d, sub, zero_state)
        st_ref[sub, 0, d] = _sum_head_blocks(st.T)

    def latent_block(method, d):
        st = refs[d][3][...]
        for sub in (range(SEQ_SUBS) if d == 0 else reversed(range(SEQ_SUBS))):
            st = yield from method(d, sub, st)
        refs[d][3][...] = st

    safe = decay <= GLA_SAFE_DECAY
    for method, cond in ((tile, safe), (tile_any_gates, jnp.logical_not(safe))):
        @pl.when(jnp.logical_and(cond, jnp.logical_not(is_lat)))
        def _():
            _interleave(*[context_request(method, d, sub) for sub in range(SEQ_SUBS) for d in range(2)],
                        next_gates())

        @pl.when(jnp.logical_and(cond, is_lat))
        def _():
            _interleave(latent_block(method, 0), latent_block(method, 1), next_gates())


def _gla(parts, gate_up_pad, gate_b, s0, layer, new_state):
    n = SEQ_BLOCK
    in_specs = []
    for blk in (3, 4):
        in_specs += [
            pl.BlockSpec((n, GW), lambda s, blk=blk: (_seq_step(s)[blk], CB_AQK)),
            pl.BlockSpec((n, GW), lambda s, blk=blk: (_seq_step(s)[blk], CB_AV)),
            pl.BlockSpec((n, 128), lambda s, blk=blk: (_seq_step(s)[blk], CB_LR128)),
        ]
    nxt = lambda s: jnp.minimum(s + 1, SEQ_STEPS - 1)
    in_specs += [pl.BlockSpec((n, 128), lambda s, blk=blk: (_seq_step(nxt(s))[blk], CB_LR128)) for blk in (3, 4)]
    in_specs += [pl.BlockSpec((2, 128, GLA_QK), lambda s: (0, 0, 0)),
                 pl.BlockSpec((2, 1, GLA_QK), lambda s: (0, 0, 0)),
                 pl.BlockSpec((1, 2, GLA_QK, HD), lambda s: (_seq_step(s)[1], 0, 0, 0)),
                 pl.BlockSpec(memory_space=pl.ANY)]
    return pl.pallas_call(
        _gla_kernel,
        input_output_aliases={len(in_specs) - 1: 2},
        grid=(SEQ_STEPS,),
        in_specs=in_specs,
        out_specs=[
            pl.BlockSpec((n, GW), lambda s: (_seq_step(s)[3], 0)),
            pl.BlockSpec((n, GW), lambda s: (_seq_step(s)[4], 0)),
            pl.BlockSpec((SEQ_SUBS, 1, 2, GLA_QK, HD), lambda s: (jnp.minimum(s, LAT_STEP0 - 1), layer, 0, 0, 0)),
        ],
        out_shape=[
            jax.ShapeDtypeStruct((T, GW), F32),
            jax.ShapeDtypeStruct((T, GW), F32),
            jax.ShapeDtypeStruct(new_state.shape, F32),
        ],
        scratch_shapes=[pltpu.VMEM((GW, GLA_QK), F32), pltpu.VMEM((GW, GLA_QK), F32),
                        pltpu.VMEM((2, 2, n, GLA_QK), F32), pltpu.SMEM((2,), F32)],
        compiler_params=_cparams(("arbitrary",)),
        name="gla",
    )(parts, parts, parts, parts, parts, parts, parts, parts, gate_up_pad, gate_b, s0, new_state)


OUT_TM = 512
FFN_TF = 1408
FFN_PASSES = D_FF // FFN_TF
OUT_TILES = T // OUT_TM


def _run_in_order(order):
    results = {}
    for gen in order:
        try:
            next(gen)
        except StopIteration as stop:
            results[gen] = stop.value
    return results


def _outproj_ffn_kernel(*refs, n_x, n_out):
    x_refs = refs[:n_x]
    (mod_ref, modp_ref, gpost_ref, gpre2_ref, gpost2_ref, pool_ref, na_ref, rf_ref, rb_ref, rg_ref, af_ref, ab_ref,
     ag_ref, ng_ref, w_ref, wg_ref, wu_ref, wd_ref) = refs[n_x:n_x + 18]
    o_refs = refs[n_x + 18:n_x + 18 + n_out]
    x1_scr, hb_scr = refs[n_x + 18 + n_out:]
    i = pl.program_id(0)
    wslot = i % 2
    rslot = 1 - wslot

    @pl.when(i == 0)
    def _():
        x1_scr[1] = jnp.zeros((OUT_TM, D), F32)
        hb_scr[1] = jnp.zeros((OUT_TM, D), BF16)

    avg = jnp.where(_iota((GW, GW), 0) // HD == _iota((GW, GW), 1) // HD, 1.0 / HD, 0.0).astype(BF16)
    x_tile = _x_tile(x_refs, OUT_TM, jnp.minimum(i, OUT_TILES - 1))

    def mixer_tail(rows):
        r = rf_ref[rows, :] + rb_ref[rows, :]
        r = r - _head_mean(r, avg)
        r = r * lax.rsqrt(_head_mean(r * r, avg) + GN_EPS) * _silu(rg_ref[rows, :])
        a = af_ref[rows, :] + ab_ref[rows, :]
        a = a * lax.rsqrt(_head_mean(a * a, avg) + RMS_EPS) * ng_ref[...] * _silu(ag_ref[rows, :])
        yield
        y = _dot(pool_ref[rows, :].astype(BF16), w_ref[0, 0:GW, :])
        y = y + _dot(na_ref[rows, :].astype(BF16), w_ref[0, GW:2 * GW, :])
        y = y + _dot(r.astype(BF16), w_ref[0, 2 * GW:3 * GW, :])
        y = y + _dot(a.astype(BF16), w_ref[0, 3 * GW:4 * GW, :])
        yield
        x1 = x_tile[rows] + mod_ref[0, 2:3, :] * _rms(y, gpost_ref[...])
        x1_scr[wslot, rows, :] = x1
        hb_scr[wslot, rows, :] = (_rms(x1, gpre2_ref[...]) * (1.0 + mod_ref[0, 4:5, :])
                                  + mod_ref[0, 3:4, :]).astype(BF16)

    def ffn(rows):
        hb = hb_scr[rslot, rows, :]
        y = None
        for j in range(FFN_PASSES):
            cols = slice(j * FFN_TF, (j + 1) * FFN_TF)
            act = (_silu(_dot(hb, wg_ref[0, :, cols])) * _dot(hb, wu_ref[0, :, cols])).astype(BF16)
            yield
            part = _dot(act, wd_ref[0, cols, :])
            y = part if y is None else y + part
            yield
        return x1_scr[rslot, rows, :] + modp_ref[0, 5:6, :] * _rms(y, gpost2_ref[...])

    half = OUT_TM // 2
    fa, fb = ffn(slice(0, half)), ffn(slice(half, OUT_TM))
    ta, tb = mixer_tail(slice(0, half)), mixer_tail(slice(half, OUT_TM))
    done = _run_in_order([fa, ta, fa, ta, fa, ta, fa, fb, tb, fa, fb, tb, fb, tb, fb, fb])
    x2 = jnp.concatenate([done[fa], done[fb]], axis=0)
    if len(o_refs) == 1:
        o_refs[0][...] = x2
    else:
        is_ctx = i - 1 < T_CTX // OUT_TM

        @pl.when(is_ctx)
        def _():
            o_refs[0][...] = x2

        @pl.when(jnp.logical_not(is_ctx))
        def _():
            o_refs[1][...] = x2


def _outproj_ffn(x, mod_l, g_post, g_pre2, g_post2, o_pool, o_na, ret_f, ret_b, gla_f, gla_b, parts, ng_lanes,
                 w_out_b, wg, wu, wd, layer, split_out):
    tm = OUT_TM
    cur = lambda i: jnp.minimum(i, OUT_TILES - 1)
    prev = lambda i: jnp.maximum(i - 1, 0)
    act = pl.BlockSpec((tm, GW), lambda i: (cur(i), 0))
    vec = pl.BlockSpec((1, D), lambda i: (0, 0))
    once = pl.Buffered(1)
    x_specs, x_args = _x_specs(x, tm, cur)
    if split_out:
        out_specs, _ = _x_specs((None, None), tm, prev)
        out_shape = [jax.ShapeDtypeStruct((T_CTX, D), F32), jax.ShapeDtypeStruct((T_LAT, D), F32)]
    else:
        out_specs = [pl.BlockSpec((tm, D), lambda i: (prev(i), 0))]
        out_shape = [jax.ShapeDtypeStruct((T, D), F32)]
    out = pl.pallas_call(
        functools.partial(_outproj_ffn_kernel, n_x=len(x_args), n_out=len(out_shape)),
        grid=(OUT_TILES + 1,),
        in_specs=x_specs + [
            pl.BlockSpec((1, 6, D), lambda i: (_mod_row(cur(i), tm), 0, 0)),
            pl.BlockSpec((1, 6, D), lambda i: (_mod_row(prev(i), tm), 0, 0)),
            vec, vec, vec,
            act, act, act, act,
            pl.BlockSpec((tm, GW), lambda i: (cur(i), CB_RG)),
            act, act,
            pl.BlockSpec((tm, GW), lambda i: (cur(i), CB_AG)),
            pl.BlockSpec((1, GW), lambda i: (0, 0)),
            pl.BlockSpec((1, D, D), lambda i: (layer, 0, 0), pipeline_mode=once),
            pl.BlockSpec((1, D, D_FF), lambda i: (layer, 0, 0), pipeline_mode=once),
            pl.BlockSpec((1, D, D_FF), lambda i: (layer, 0, 0), pipeline_mode=once),
            pl.BlockSpec((1, D_FF, D), lambda i: (layer, 0, 0), pipeline_mode=once),
        ],
        out_specs=out_specs,
        out_shape=out_shape,
        scratch_shapes=[pltpu.VMEM((2, tm, D), F32), pltpu.VMEM((2, tm, D), BF16)],
        compiler_params=_cparams(("arbitrary",)),
        name="outproj_ffn",
    )(*x_args, mod_l, mod_l, g_post, g_pre2, g_post2, o_pool, o_na, ret_f, ret_b, parts, gla_f, gla_b, parts, ng_lanes,
      w_out_b, wg, wu, wd)
    return tuple(out) if split_out else out[0]


def _rope_tables():
    nf = 16
    inv = (ROPE_BASE ** (-np.arange(nf, dtype=np.float32) / nf)).astype(np.float32)
    tok = np.arange(L_LAT)
    cos = np.zeros((L_LAT, HD), np.float32)
    sin = np.zeros((L_LAT, HD), np.float32)
    for axis, pos in enumerate((tok // GRID_W, tok % GRID_W)):
        ang = pos.astype(np.float32)[:, None] * inv[None, :]
        c, s = np.cos(ang), np.sin(ang)
        cos[:, axis * 32:axis * 32 + 32] = np.concatenate([c, c], axis=1)
        sin[:, axis * 32:axis * 32 + 32] = np.concatenate([-s, s], axis=1)
    return jnp.asarray(np.tile(cos, (1, HEADS))), jnp.asarray(np.tile(sin, (1, HEADS)))


def _block_diag(w):
    g, c, _ = w.shape
    out = jnp.zeros((g * c, g * c), w.dtype)
    for i in range(g):
        out = out.at[i * c:(i + 1) * c, i * c:(i + 1) * c].set(w[i])
    return out


def kernel(x_prompt, x_sample, cache_na_k, cache_na_v, state_ret, state_gla, c, c_ctx, w_mod, b_mod,
           g_pre_mix, g_post_mix, g_pre_ffn, g_post_ffn, w_in, w_out, pool_w, pool_scale, na_rpb,
           ret_decay_logit, gla_gate_up, gla_gate_b, gla_norm_g, w_ffn_gate, w_ffn_up, w_ffn_down):
    x = (x_prompt.reshape(T_CTX, D), x_sample.reshape(T_LAT, D))
    cv8 = jnp.concatenate([c_ctx[None, :], c, jnp.zeros((8 - 1 - B_LAT, D), F32)], axis=0)
    mods = _modulation(cv8, w_mod, b_mod).reshape(DEPTH, 8, 6, D)

    w_main_b = w_in[:, :, :P_MAIN].astype(BF16)
    w_lr_b = jnp.pad(w_in[:, :, P_MAIN:], ((0, 0), (0, 0), (0, P_PAD - P_IN))).astype(BF16)
    w_out_b = w_out.astype(BF16)
    wg_b, wu_b, wd_b = w_ffn_gate.astype(BF16), w_ffn_up.astype(BF16), w_ffn_down.astype(BF16)
    gate_up_pad = jnp.pad(gla_gate_up, ((0, 0), (0, 0), (0, 128 - GLA_LOWRANK), (0, 0)))
    rope_tabs = _rope_tables()
    ck = cache_na_k.reshape(B_LAT, DEPTH, PAST, GW)
    cv = cache_na_v.reshape(B_LAT, DEPTH, PAST, GW)
    s0_ret = state_ret.reshape(B_LAT, DEPTH, 2, GW, HD)
    s0_gla = state_gla.reshape(B_LAT, DEPTH, 2, GLA_QK, HD)

    new_k = jnp.zeros((B_CTX, DEPTH, L_CTX, GW), F32)
    new_v = jnp.zeros((B_CTX, DEPTH, L_CTX, GW), F32)
    new_ret = jnp.zeros((B_CTX, DEPTH, 2, GW, HD), F32)
    new_gla = jnp.zeros((B_CTX, DEPTH, 2, GLA_QK, HD), F32)
    for l in range(DEPTH):
        mod_l = mods[l]
        parts = _inproj(x, mod_l, g_pre_mix[l][None, :], w_main_b, w_lr_b, l)
        o_pool = _pool(parts, _block_diag(pool_w[l]).astype(BF16), pool_scale[l][None, :])
        o_na, new_k, new_v = _attention(parts, ck, cv, na_rpb[l], l, new_k, new_v)
        lg_lanes = jnp.repeat(ret_decay_logit[l], HD, axis=1)
        rf, rb, new_ret = _retention(parts, lg_lanes, rope_tabs, s0_ret[:, l], l, new_ret)
        gf, gbw, new_gla = _gla(parts, gate_up_pad[l], gla_gate_b[l][:, None, :], s0_gla[:, l], l, new_gla)
        x = _outproj_ffn(x, mod_l, g_post_mix[l][None, :], g_pre_ffn[l][None, :], g_post_ffn[l][None, :],
                         o_pool, o_na, rf, rb, gf, gbw, parts, jnp.tile(gla_norm_g[l], HEADS)[None, :],
                         w_out_b, wg_b, wu_b, wd_b, l, split_out=(l == DEPTH - 1))

    return (x[0].reshape(B_CTX, L_CTX, D), x[1].reshape(B_LAT, L_LAT, D),
            new_k.reshape(B_CTX, DEPTH, L_CTX, HEADS, HD), new_v.reshape(B_CTX, DEPTH, L_CTX, HEADS, HD),
            new_ret.reshape(B_CTX, DEPTH, 2, HEADS, RET_DK, HD), new_gla.reshape(B_CTX, DEPTH, 2, HEADS, GLA_DK, HD))
```

```python
import functools

import numpy as np
import jax
import jax.numpy as jnp
from jax import lax
from jax.experimental import pallas as pl
from jax.experimental.pallas import tpu as pltpu

F32 = jnp.float32
BF16 = jnp.bfloat16

D = 1024
B_CTX, L_CTX = 32, 256
B_LAT, L_LAT = 2, 4096
DEPTH = 4
PAST = 256
GRID_W = 64
GRID_H = L_LAT // GRID_W
T_CTX = B_CTX * L_CTX
T_LAT = B_LAT * L_LAT
T = T_CTX + T_LAT
GW = 256
HEADS = 4
HD = 64
POOL_WINDOWS = (2, 4, 8, 16)
NA_ROWS, NA_COLS = 8, 16
RET_DK = 64
GLA_DK = 32
GLA_LOWRANK = 16
GLA_TAU = 16.0
D_FF = 2816
P_IN = 2832
P_PAD = 2944
P_MAIN = 2816
ROPE_BASE = 10000.0
RMS_EPS = 1e-6
GN_EPS = 1e-5
NEG = -1e30

CB_POOL, CB_NAQ, CB_NAK, CB_NAV, CB_RQ, CB_RK, CB_RV, CB_RG, CB_AQK, CB_AV, CB_AG = range(11)
CB_LR128 = P_IN // 128

SEQ_TILE = L_CTX
SEQ_SUBS = 4
SEQ_BLOCK = SEQ_SUBS * SEQ_TILE
SEQ_STEPS = T // SEQ_BLOCK
LAT_STEP0 = T_CTX // SEQ_BLOCK
LAT_BLOCKS = L_LAT // SEQ_BLOCK
GLA_CHUNK = 64
GLA_SUB = 16
VMEM_LIMIT = 56 * 1024 * 1024


def _cparams(sem):
    return pltpu.CompilerParams(dimension_semantics=sem, vmem_limit_bytes=VMEM_LIMIT)


def _silu(x):
    return x / (1.0 + jnp.exp(-x))


def _log_sigmoid(z):
    return jnp.minimum(z, 0.0) - jnp.log1p(jnp.exp(-jnp.abs(z)))


def _rms(x, g):
    return x * lax.rsqrt(jnp.mean(x * x, axis=-1, keepdims=True) + RMS_EPS) * g


def _dot(a, b):
    return jnp.dot(a, b, preferred_element_type=F32)


def _dot_nt(a, b):
    return lax.dot_general(a, b, (((1,), (1,)), ((), ())), preferred_element_type=F32)


def _dot_tn(a, b):
    return lax.dot_general(a, b, (((0,), (0,)), ((), ())), preferred_element_type=F32)


def _split_hi_lo(x):
    hi = x.astype(BF16)
    return hi, (x - hi.astype(F32)).astype(BF16)


def _dot_exact01(a01, x):
    hi, lo = _split_hi_lo(x)
    return _dot(a01, hi) + _dot(a01, lo)


def _iota(shape, dim):
    return lax.broadcasted_iota(jnp.int32, shape, dim)


def _expand_heads(x, head_w):
    n, w = x.shape
    xe = jnp.concatenate([x] * HEADS, axis=0)
    rowh = _iota((HEADS * n, w), 0) // n
    laneh = (_iota((HEADS * n, w), 1) // head_w) % HEADS
    return jnp.where(rowh == laneh, xe, jnp.zeros_like(xe))


def _extract_heads(p, n):
    laneh = _iota((n, GW), 1) // HD
    out = p[0:n]
    for h in range(1, HEADS):
        out = jnp.where(laneh == h, p[h * n:(h + 1) * n], out)
    return out


def _head_mean(x, avg):
    hi, lo = _split_hi_lo(x)
    return _dot(hi, avg) + _dot(lo, avg)


def _interleave(*gens):
    results = [None] * len(gens)
    live = list(range(len(gens)))
    while live:
        for i in list(live):
            try:
                next(gens[i])
            except StopIteration as stop:
                results[i] = stop.value
                live.remove(i)
    return results


def _mod_row(i, tm):
    return jnp.where(i < T_CTX // tm, 0, 1 + (i * tm - T_CTX) // L_LAT)


def _mod_kernel(cv_ref, w_ref, b_ref, o_ref):
    s = _silu(cv_ref[...]).astype(BF16)
    o_ref[0] = _dot(s, w_ref[0].astype(BF16)) + b_ref[0]


def _modulation(cv8, w_mod, b_mod):
    tn = 1536
    return pl.pallas_call(
        _mod_kernel,
        grid=(DEPTH, 6 * D // tn),
        in_specs=[
            pl.BlockSpec((8, D), lambda l, j: (0, 0)),
            pl.BlockSpec((1, D, tn), lambda l, j: (l, 0, j)),
            pl.BlockSpec((1, 1, tn), lambda l, j: (l, 0, j)),
        ],
        out_specs=pl.BlockSpec((1, 8, tn), lambda l, j: (l, 0, j)),
        out_shape=jax.ShapeDtypeStruct((DEPTH, 8, 6 * D), F32),
        compiler_params=_cparams(("arbitrary", "arbitrary")),
        name="modulation",
    )(cv8, w_mod, b_mod.reshape(DEPTH, 1, 6 * D))


IN_TM = 512


def _x_specs(x, tm, tile_of=lambda i: i):
    if not isinstance(x, tuple):
        return [pl.BlockSpec((tm, D), lambda i: (tile_of(i), 0))], [x]
    nc = T_CTX // tm
    return ([pl.BlockSpec((tm, D), lambda i: (jnp.minimum(tile_of(i), nc - 1), 0)),
             pl.BlockSpec((tm, D), lambda i: (jnp.maximum(tile_of(i) - nc, 0), 0))], list(x))


def _x_tile(x_refs, tm, tile):
    if len(x_refs) == 1:
        return x_refs[0][...]
    return jnp.where(tile < T_CTX // tm, x_refs[0][...], x_refs[1][...])


def _inproj_kernel(*refs, n_x):
    x_refs, (mod_ref, g_ref, w_ref, wlr_ref, o_ref) = refs[:n_x], refs[n_x:]
    x = _x_tile(x_refs, IN_TM, pl.program_id(0))
    w_lr = jnp.where(_iota((D, P_PAD - P_MAIN), 1) < P_IN - P_MAIN, wlr_ref[0], jnp.zeros((), BF16))

    def half_tile(rows):
        h = _rms(x[rows], g_ref[...]) * (1.0 + mod_ref[0, 1:2, :]) + mod_ref[0, 0:1, :]
        hb = h.astype(BF16)
        yield
        for a in range(0, P_MAIN, 1024):
            b = min(a + 1024, P_MAIN)
            o_ref[rows, a:b] = _dot(hb, w_ref[0, :, a:b])
            yield
        o_ref[rows, P_MAIN:P_PAD] = _dot(hb, w_lr)

    half = IN_TM // 2
    first, second = half_tile(slice(0, half)), half_tile(slice(half, IN_TM))
    next(first)
    _interleave(first, second)


def _inproj(x, mod_l, g_pre, w_in_b, layer):
    tm = IN_TM
    x_specs, x_args = _x_specs(x, tm)
    return pl.pallas_call(
        functools.partial(_inproj_kernel, n_x=len(x_args)),
        grid=(T // tm,),
        in_specs=x_specs + [
            pl.BlockSpec((1, 6, D), lambda i: (_mod_row(i, tm), 0, 0)),
            pl.BlockSpec((1, D), lambda i: (0, 0)),
            pl.BlockSpec((1, D, P_MAIN), lambda i: (layer, 0, 0), pipeline_mode=pl.Buffered(1)),
            pl.BlockSpec((1, D, P_PAD - P_MAIN), lambda i: (layer, 0, P_MAIN // (P_PAD - P_MAIN)),
                         pipeline_mode=pl.Buffered(1)),
        ],
        out_specs=pl.BlockSpec((tm, P_PAD), lambda i: (i, 0)),
        out_shape=jax.ShapeDtypeStruct((T, P_PAD), F32),
        compiler_params=_cparams(("arbitrary",)),
        name="inproj",
    )(*x_args, mod_l, g_pre, w_in_b, w_in_b)


POOL_TM = 1024
POOL_SUB = 256


def _pool_kernel(v_ref, w_ref, scale_ref, o_ref, band_scr, cnt_scr):
    i = pl.program_id(0)
    n = POOL_SUB

    @pl.when(i == 0)
    def _():
        t = _iota((n, n), 0)
        s = _iota((n, n), 1)
        lane_g = _iota((n, GW), 1) // HD
        for kind, seg_len in enumerate((L_CTX, GRID_W)):
            seg0 = t & ~(seg_len - 1)
            seg1 = seg0 + seg_len
            cnt = jnp.zeros((n, GW), F32)
            for gi, win in enumerate(POOL_WINDOWS):
                lo = jnp.maximum(t - win // 2, seg0)
                hi = jnp.minimum(t - win // 2 + win, seg1)
                band_scr[kind, gi] = jnp.where(s >= lo, jnp.where(s < hi, 1.0, 0.0), 0.0).astype(BF16)
                cnt = jnp.where(lane_g == gi, (hi - lo).astype(F32), cnt)
            cnt_scr[kind] = cnt

    kind = jnp.where(i < T_CTX // POOL_TM, 0, 1)

    def piece(rows):
        v = v_ref[rows, :]
        vh, vl = _split_hi_lo(v)
        lane_g = _iota((n, GW), 1) // HD
        yield
        mean = None
        for gi in range(len(POOL_WINDOWS)):
            band = band_scr[kind, gi]
            m = _dot(band, vh) + _dot(band, vl)
            mean = m if mean is None else jnp.where(lane_g == gi, m, mean)
        yield
        d = (mean / cnt_scr[kind] - v).astype(BF16)
        o_ref[rows, :] = (_dot(d, w_ref[...]) * scale_ref[...]).astype(o_ref.dtype)

    _interleave(*[piece(slice(j * n, (j + 1) * n)) for j in range(POOL_TM // n)])


def _pool(parts, w_bd, scale):
    tm = POOL_TM
    n_win = len(POOL_WINDOWS)
    return pl.pallas_call(
        _pool_kernel,
        grid=(T // tm,),
        in_specs=[
            pl.BlockSpec((tm, GW), lambda i: (i, CB_POOL)),
            pl.BlockSpec((GW, GW), lambda i: (0, 0)),
            pl.BlockSpec((1, GW), lambda i: (0, 0)),
        ],
        out_specs=pl.BlockSpec((tm, GW), lambda i: (i, 0)),
        out_shape=jax.ShapeDtypeStruct((T, GW), BF16),
        scratch_shapes=[pltpu.VMEM((2, n_win, POOL_SUB, POOL_SUB), BF16), pltpu.VMEM((2, POOL_SUB, GW), F32)],
        compiler_params=_cparams(("arbitrary",)),
        name="pool",
    )(parts, w_bd, scale)


def _softmax_rows(s):
    m = jnp.max(s, axis=-1, keepdims=True)
    p = jnp.exp(s - m)
    return p / jnp.sum(p, axis=-1, keepdims=True)


def _ctx_attn_rows(q_ref, k_ref, v_ref, o_ref, rows):
    qe = _expand_heads(q_ref[rows, :].astype(BF16), HD)
    yield
    s = _dot_nt(qe, k_ref[rows, :].astype(BF16)) * (HD ** -0.5)
    yield
    p = _softmax_rows(s).astype(BF16)
    yield
    o_ref[rows, :] = _extract_heads(_dot(p, v_ref[rows, :].astype(BF16)), L_CTX).astype(o_ref.dtype)


NA_ROWS_PER_STEP = 8
NA_INTERLEAVE = 8
ATT_ROWS = NA_ROWS_PER_STEP * GRID_W
ATT_CTX_STEPS = T_CTX // ATT_ROWS
ATT_LAT_STEPS = L_LAT // ATT_ROWS
NA_WIN = NA_ROWS * GRID_W
NA_DR = 2 * NA_ROWS - 1
NA_DC = 2 * NA_COLS - 1


def _na_bias_table(rpb_ref, e2_ref):
    shape = (GRID_W, 2 * GRID_W)
    qc = _iota(shape, 0)
    lane = _iota(shape, 1)
    kc = lane % GRID_W
    upper = lane >= GRID_W
    c0 = jnp.clip(qc - NA_COLS // 2, 0, GRID_W - NA_COLS)
    dc = jnp.where((kc >= c0) & (kc < c0 + NA_COLS), kc - qc + (NA_COLS - 1), -1)

    def one(ha, carry):
        h = ha // (NA_DR - 1)
        a = ha % (NA_DR - 1)
        acc = jnp.full(shape, NEG, F32)
        for j in range(NA_DC):
            val = jnp.where(upper, rpb_ref[h * NA_DR + a + 1, j], rpb_ref[h * NA_DR + a, j])
            acc = jnp.where(dc == j, val, acc)
        e2_ref[h, a] = acc
        return carry

    lax.fori_loop(0, HEADS * (NA_DR - 1), one, 0)


def _attn_kernel(q_ref, k_ref, v_ref, kseq_ref, vseq_ref, ck_ref, cv_ref, rpb_ref, kprev_ref, vprev_ref,
                 o_ref, ko_ref, vo_ref, kb_ref, vb_ref, e2_ref):
    del kprev_ref, vprev_ref
    s = pl.program_id(0)

    @pl.when(s < ATT_CTX_STEPS)
    def _():
        for i in range(ATT_ROWS // L_CTX):
            ko_ref[i, 0] = k_ref[i * L_CTX:(i + 1) * L_CTX, :]
            vo_ref[i, 0] = v_ref[i * L_CTX:(i + 1) * L_CTX, :]
        _interleave(*[_ctx_attn_rows(q_ref, k_ref, v_ref, o_ref, slice(i * L_CTX, (i + 1) * L_CTX))
                      for i in range(ATT_ROWS // L_CTX)])

    @pl.when(s >= ATT_CTX_STEPS)
    def _():
        step = (s - ATT_CTX_STEPS) % ATT_LAT_STEPS

        @pl.when(s == ATT_CTX_STEPS)
        def _():
            _na_bias_table(rpb_ref, e2_ref)

        @pl.when(step == 0)
        def _():
            kb_ref[...] = kseq_ref[...].astype(BF16)
            vb_ref[...] = vseq_ref[...].astype(BF16)

        _na_rows(step, q_ref, ck_ref, cv_ref, o_ref, kb_ref, vb_ref, e2_ref)


def _na_rows(step, q_ref, ck_ref, cv_ref, o_ref, kb_ref, vb_ref, e2_ref):
    ckb = ck_ref[0, 0].astype(BF16)
    cvb = cv_ref[0, 0].astype(BF16)
    scale = HD ** -0.5

    def one_row(rr):
        r = step * NA_ROWS_PER_STEP + rr
        r0 = jnp.clip(r - NA_ROWS // 2, 0, GRID_H - NA_ROWS)
        base = r0 - r + (NA_ROWS - 1)
        q0 = pl.multiple_of(rr * GRID_W, GRID_W)
        k0 = pl.multiple_of(r0 * GRID_W, GRID_W)
        qe = _expand_heads(q_ref[pl.ds(q0, GRID_W), :].astype(BF16), HD)
        kw = kb_ref[pl.ds(k0, NA_WIN), :]
        vw = vb_ref[pl.ds(k0, NA_WIN), :]
        bias = jnp.concatenate(
            [jnp.concatenate([e2_ref[h, base + 2 * p] for p in range(NA_ROWS // 2)], axis=1)
             for h in range(HEADS)], axis=0)
        yield
        s_loc = _dot_nt(qe, kw) * scale + bias
        s_ctx = _dot_nt(qe, ckb) * scale
        yield
        m = jnp.maximum(jnp.max(s_loc, axis=-1, keepdims=True), jnp.max(s_ctx, axis=-1, keepdims=True))
        p_loc = jnp.exp(s_loc - m)
        p_ctx = jnp.exp(s_ctx - m)
        inv = 1.0 / (jnp.sum(p_loc, axis=-1, keepdims=True) + jnp.sum(p_ctx, axis=-1, keepdims=True))
        yield
        pv = _dot((p_loc * inv).astype(BF16), vw) + _dot((p_ctx * inv).astype(BF16), cvb)
        yield
        o_ref[pl.ds(q0, GRID_W), :] = _extract_heads(pv, GRID_W).astype(o_ref.dtype)

    def row_group(i, carry):
        _interleave(*[one_row(NA_INTERLEAVE * i + j) for j in range(NA_INTERLEAVE)])
        return carry

    lax.fori_loop(0, NA_ROWS_PER_STEP // NA_INTERLEAVE, row_group, 0)


def _attention(parts, ck, cv, rpb, layer, new_k, new_v):
    lat_req = lambda s: jnp.maximum(s - ATT_CTX_STEPS, 0) // ATT_LAT_STEPS
    seq_blk0 = T_CTX // L_LAT
    per_step = ATT_ROWS // L_CTX
    cache_spec = pl.BlockSpec((per_step, 1, L_CTX, GW), lambda s: (jnp.minimum(s, ATT_CTX_STEPS - 1), layer, 0, 0))
    return pl.pallas_call(
        _attn_kernel,
        grid=(T // ATT_ROWS,),
        in_specs=[
            pl.BlockSpec((ATT_ROWS, GW), lambda s: (s, CB_NAQ)),
            pl.BlockSpec((ATT_ROWS, GW), lambda s: (s, CB_NAK)),
            pl.BlockSpec((ATT_ROWS, GW), lambda s: (s, CB_NAV)),
            pl.BlockSpec((L_LAT, GW), lambda s: (seq_blk0 + lat_req(s), CB_NAK)),
            pl.BlockSpec((L_LAT, GW), lambda s: (seq_blk0 + lat_req(s), CB_NAV)),
            pl.BlockSpec((1, 1, PAST, GW), lambda s: (lat_req(s), layer, 0, 0)),
            pl.BlockSpec((1, 1, PAST, GW), lambda s: (lat_req(s), layer, 0, 0)),
            pl.BlockSpec(memory_space=pltpu.SMEM),
            pl.BlockSpec(memory_space=pl.ANY),
            pl.BlockSpec(memory_space=pl.ANY),
        ],
        out_specs=[pl.BlockSpec((ATT_ROWS, GW), lambda s: (s, 0)), cache_spec, cache_spec],
        out_shape=[jax.ShapeDtypeStruct((T, GW), BF16), jax.ShapeDtypeStruct(new_k.shape, F32),
                   jax.ShapeDtypeStruct(new_v.shape, F32)],
        input_output_aliases={8: 1, 9: 2},
        scratch_shapes=[pltpu.VMEM((L_LAT, GW), BF16), pltpu.VMEM((L_LAT, GW), BF16),
                        pltpu.VMEM((HEADS, NA_DR - 1, GRID_W, 2 * GRID_W), F32)],
        compiler_params=_cparams(("arbitrary",)),
        name="attention",
    )(parts, parts, parts, parts, parts, ck, cv, rpb.reshape(HEADS * NA_DR, NA_DC), new_k, new_v)


def _rope(x, cos, sin_signed):
    lane = _iota(x.shape, 1)
    partner = jnp.where(lane % 32 < 16, pltpu.roll(x, GW - 16, 1), pltpu.roll(x, 16, 1))
    return x * cos + partner * sin_signed


def _seq_step(s):
    is_lat = s >= LAT_STEP0
    u = jnp.maximum(s - LAT_STEP0, 0)
    b = u // LAT_BLOCKS
    t = u % LAT_BLOCKS
    bwd = jnp.where(is_lat, LAT_STEP0 + b * LAT_BLOCKS + (LAT_BLOCKS - 1 - t), s)
    return is_lat, b, t, s, bwd


def _sum_head_blocks(st):
    return st[:, 0:HD] + st[:, HD:2 * HD] + st[:, 2 * HD:3 * HD] + st[:, 3 * HD:4 * HD]


def _ret_kernel(qf_ref, kf_ref, vf_ref, qb_ref, kb_ref, vb_ref, cf_ref, sf_ref, cb_ref, sb_ref, lg_ref, s0_ref,
                st_prev_ref, of_ref, ob_ref, st_ref, sf_scr, sb_scr, w_scr, dq_scr, dk_scr):
    del st_prev_ref
    s = pl.program_id(0)
    is_lat, _, t, _, _ = _seq_step(s)
    n = SEQ_TILE
    blockdiag = _iota((GW, GW), 0) // HD == _iota((GW, GW), 1) // HD
    lg = _log_sigmoid(lg_ref[...])

    @pl.when(s == 0)
    def _():
        ti = _iota((n, GW), 0).astype(F32)
        i_ = _iota((n, n), 0).astype(F32)
        j_ = _iota((n, n), 1).astype(F32)
        for d in range(2):
            lgd = lg[d:d + 1, :]
            diff = (i_ - j_) if d == 0 else (j_ - i_)
            pos = ti if d == 0 else (n - 1.0) - ti
            w_scr[d] = jnp.concatenate(
                [jnp.where(diff >= 0, jnp.exp(jnp.maximum(diff, 0.0) * lgd[:, h * HD:h * HD + 1]), 0.0)
                 for h in range(HEADS)], axis=0)
            dq_scr[d] = jnp.exp((pos + 1.0) * lgd)
            dk_scr[d] = jnp.exp((n - 1.0 - pos) * lgd)

    @pl.when(jnp.logical_and(is_lat, t == 0))
    def _():
        for d, scr in ((0, sf_scr), (1, sb_scr)):
            s0 = jnp.concatenate([s0_ref[0, d]] * HEADS, axis=1)
            scr[...] = jnp.where(blockdiag, s0, 0.0)

    fwd_refs = (qf_ref, kf_ref, vf_ref, cf_ref, sf_ref, of_ref)
    bwd_refs = (qb_ref, kb_ref, vb_ref, cb_ref, sb_ref, ob_ref)

    def tile(d, refs, sub, scr):
        q_ref, k_ref, v_ref, c_ref, s_ref, o_ref = refs
        rows = slice(sub * n, (sub + 1) * n)
        q = q_ref[rows, :] * (RET_DK ** -0.5)
        k = k_ref[rows, :]
        if scr is not None:
            q = _rope(q, c_ref[rows, :], s_ref[rows, :])
            k = _rope(k, c_ref[rows, :], s_ref[rows, :])
        vb16 = v_ref[rows, :].astype(BF16)
        yield
        a = _dot_nt(_expand_heads(q.astype(BF16), HD), k.astype(BF16)) * w_scr[d]
        yield
        o = _extract_heads(_dot(a.astype(BF16), vb16), n)
        yield
        if scr is not None:
            s_old = scr[...]
            o = o + _dot(q.astype(BF16), s_old.astype(BF16)) * dq_scr[d]
        o_ref[rows, :] = o
        yield
        upd = jnp.where(blockdiag, _dot_tn((k * dk_scr[d]).astype(BF16), vb16), 0.0)
        if scr is None:
            st_ref[sub, 0, d] = _sum_head_blocks(upd)
        else:
            scr[...] = s_old * jnp.exp(float(n) * lg[d:d + 1, :]) + upd

    def chain(d, refs, order, scr):
        for sub in order:
            yield from tile(d, refs, sub, scr)

    @pl.when(jnp.logical_not(is_lat))
    def _():
        _interleave(*[tile(d, (fwd_refs, bwd_refs)[d], sub, None) for sub in range(SEQ_SUBS) for d in range(2)])

    @pl.when(is_lat)
    def _():
        _interleave(chain(0, fwd_refs, range(SEQ_SUBS), sf_scr),
                    chain(1, bwd_refs, reversed(range(SEQ_SUBS)), sb_scr))


def _retention(parts, lg_lanes, rope_tabs, s0, layer, new_state):
    n, blk = SEQ_TILE, SEQ_BLOCK
    fwd = lambda s: _seq_step(s)[3]
    bwd = lambda s: _seq_step(s)[4]
    in_specs = [pl.BlockSpec((blk, GW), lambda s, c=c: (fwd(s), c)) for c in (CB_RQ, CB_RK, CB_RV)]
    in_specs += [pl.BlockSpec((blk, GW), lambda s, c=c: (bwd(s), c)) for c in (CB_RQ, CB_RK, CB_RV)]
    in_specs += [pl.BlockSpec((blk, GW), lambda s: (_seq_step(s)[2], 0))] * 2
    in_specs += [pl.BlockSpec((blk, GW), lambda s: (LAT_BLOCKS - 1 - _seq_step(s)[2], 0))] * 2
    in_specs += [pl.BlockSpec((2, GW), lambda s: (0, 0)),
                 pl.BlockSpec((1, 2, GW, HD), lambda s: (_seq_step(s)[1], 0, 0, 0)),
                 pl.BlockSpec(memory_space=pl.ANY)]
    return pl.pallas_call(
        _ret_kernel,
        input_output_aliases={len(in_specs) - 1: 2},
        grid=(SEQ_STEPS,),
        in_specs=in_specs,
        out_specs=[
            pl.BlockSpec((blk, GW), lambda s: (fwd(s), 0)),
            pl.BlockSpec((blk, GW), lambda s: (bwd(s), 0)),
            pl.BlockSpec((SEQ_SUBS, 1, 2, GW, HD), lambda s: (jnp.minimum(s, LAT_STEP0 - 1), layer, 0, 0, 0)),
        ],
        out_shape=[
            jax.ShapeDtypeStruct((T, GW), F32),
            jax.ShapeDtypeStruct((T, GW), F32),
            jax.ShapeDtypeStruct(new_state.shape, F32),
        ],
        scratch_shapes=[pltpu.VMEM((GW, GW), F32), pltpu.VMEM((GW, GW), F32),
                        pltpu.VMEM((2, HEADS * n, n), F32), pltpu.VMEM((2, n, GW), F32),
                        pltpu.VMEM((2, n, GW), F32)],
        compiler_params=_cparams(("arbitrary",)),
        name="retention",
    )(parts, parts, parts, parts, parts, parts, rope_tabs[0], rope_tabs[1], rope_tabs[0], rope_tabs[1],
      lg_lanes, s0, new_state)


GLA_QK = HEADS * GLA_DK
N_SUB = GLA_CHUNK // GLA_SUB
GLA_SAFE_DECAY = 60.0


def _gla_tile(q, k, v, la, st, rev):
    n, c = SEQ_TILE, GLA_CHUNK
    n_chunks = n // c
    ri = _iota((n, n), 0)
    ci = _iota((n, n), 1)
    same_chunk = ri // c == ci // c
    causal = (ci >= ri) if rev else (ci <= ri)
    tri = jnp.where(same_chunk, jnp.where(causal, 1.0, 0.0), 0.0).astype(BF16)
    b = _dot_exact01(tri, la)
    yield

    def rows_of(idx, count):
        if idx is None:
            return jnp.zeros((count, GLA_QK), F32)
        return jnp.broadcast_to(b[idx:idx + 1, :], (count, GLA_QK))

    def ref_row(cc, s):
        if rev:
            return cc * c + (s + 1) * GLA_SUB if s < N_SUB - 1 else None
        return cc * c + s * GLA_SUB - 1 if s > 0 else None

    end_rows = [cc * c if rev else cc * c + c - 1 for cc in range(n_chunks)]
    own_ref = jnp.concatenate([rows_of(ref_row(cc, s), GLA_SUB) for cc in range(n_chunks) for s in range(N_SUB)],
                              axis=0)
    b_end = jnp.concatenate([rows_of(r, c) for r in end_rows], axis=0)
    sub = (_iota((n, GLA_QK), 0) // GLA_SUB) % N_SUB
    qh = q * jnp.exp(b - own_ref)
    q_parts, k_parts = [], []
    for s in range(N_SUB):
        kvalid = (sub >= s) if rev else (sub <= s)
        ref_s = jnp.concatenate([rows_of(ref_row(cc, s), c) for cc in range(n_chunks)], axis=0)
        q_parts.append(jnp.where(sub == s, qh, 0.0))
        k_parts.append(jnp.where(kvalid, k * jnp.exp(jnp.minimum(ref_s - b, GLA_SAFE_DECAY)), 0.0))
    q_cat = _expand_heads(jnp.concatenate(q_parts, axis=1).astype(BF16), GLA_DK)
    k_cat = jnp.concatenate(k_parts, axis=1).astype(BF16)
    yield
    a = _dot_nt(q_cat, k_cat)
    yield
    qi = _iota((HEADS * n, n), 0) % n
    kj = _iota((HEADS * n, n), 1)
    keep = (qi // c == kj // c) & ((kj >= qi) if rev else (kj <= qi))
    vb16 = v.astype(BF16)
    o_intra = _extract_heads(_dot(jnp.where(keep, a, 0.0).astype(BF16), vb16), n)
    yield

    qt = (q * jnp.exp(b)).astype(BF16)
    kt = (k * jnp.exp(b_end - b)).astype(BF16)
    blockdiag = _iota((GW, GLA_QK), 0) // HD == _iota((GW, GLA_QK), 1) // GLA_DK
    upd = [jnp.where(blockdiag, _dot_tn(vb16[cc * c:(cc + 1) * c], kt[cc * c:(cc + 1) * c]), 0.0)
           for cc in range(n_chunks)]
    yield
    o_inter = [None] * n_chunks
    for cc in (reversed(range(n_chunks)) if rev else range(n_chunks)):
        o_inter[cc] = _dot_nt(qt[cc * c:(cc + 1) * c], st.astype(BF16))
        st = st * jnp.exp(b[end_rows[cc]:end_rows[cc] + 1, :]) + upd[cc]
        yield
    return o_intra + jnp.concatenate(o_inter, axis=0), st


def _gla_chunk(q, k, v, la, st, rev):
    c = GLA_CHUNK
    ri = _iota((c, c), 0)
    ci = _iota((c, c), 1)
    tri = jnp.where((ci >= ri) if rev else (ci <= ri), 1.0, 0.0).astype(BF16)
    b = _dot_exact01(tri, la)
    b_end = b[0:1, :] if rev else b[c - 1:c, :]
    row = _iota((c, GLA_QK), 0)
    sub = row // GLA_SUB
    off = row % GLA_SUB

    o = _dot_nt((q * jnp.exp(b)).astype(BF16), st.astype(BF16))
    kt = (k * jnp.exp(b_end - b)).astype(BF16)
    lane_h = _iota((GW, GLA_QK), 1) // GLA_DK
    row_h = _iota((GW, GLA_QK), 0) // HD
    st_new = st * jnp.exp(b_end) + jnp.where(row_h == lane_h, _dot_tn(v.astype(BF16), kt), 0.0)

    q_parts, k_parts = [], []
    for s in range(1, N_SUB):
        if rev:
            qsub, brow = N_SUB - 1 - s, b[(N_SUB - s) * GLA_SUB:(N_SUB - s) * GLA_SUB + 1, :]
            kvalid = sub > qsub
        else:
            qsub, brow = s, b[s * GLA_SUB - 1:s * GLA_SUB, :]
            kvalid = sub < qsub
        q_parts.append(jnp.where(sub == qsub, q * jnp.exp(jnp.where(sub == qsub, b - brow, 0.0)), 0.0))
        k_parts.append(jnp.where(kvalid, k * jnp.exp(jnp.where(kvalid, brow - b, 0.0)), 0.0))
    q_cat = _expand_heads(jnp.concatenate(q_parts, axis=1).astype(BF16), GLA_DK)
    k_cat = jnp.concatenate(k_parts, axis=1).astype(BF16)
    a_off = _dot_nt(q_cat, k_cat)
    o = o + _extract_heads(_dot(a_off.astype(BF16), v.astype(BF16)), c)

    red = jnp.where(_iota((GLA_QK, GW), 0) // GLA_DK == _iota((GLA_QK, GW), 1) // HD, 1.0, 0.0).astype(BF16)
    rowv = _iota((c, GW), 0) % GLA_SUB
    for dl in range(GLA_SUB):
        if dl == 0:
            x = q * k
            vs = v
        else:
            sh = dl if not rev else c - dl
            valid = (off + dl < GLA_SUB) if rev else (off >= dl)
            ks = pltpu.roll(k, sh, 0)
            bs = pltpu.roll(b, sh, 0)
            vs = pltpu.roll(v, sh, 0)
            x = jnp.where(valid, q * ks * jnp.exp(jnp.where(valid, b - bs, 0.0)), 0.0)
            validv = (rowv + dl < GLA_SUB) if rev else (rowv >= dl)
            vs = jnp.where(validv, vs, 0.0)
        o = o + _dot(x.astype(BF16), red) * vs
    return o, st_new


def _gla_kernel(qkf_ref, vf_ref, lrf_ref, qkb_ref, vb_ref, lrb_ref, lrf_next_ref, lrb_next_ref, gu_ref, gb_ref,
                s0_ref, st_prev_ref, of_ref, ob_ref, st_ref, sf_scr, sb_scr, la_scr, decay_scr):
    del st_prev_ref
    step = pl.program_id(0)
    slot = step % 2
    is_lat, _, t, _, _ = _seq_step(step)
    lane_h = _iota((GW, GLA_QK), 1) // GLA_DK
    row_h = _iota((GW, GLA_QK), 0) // HD
    blockdiag = row_h == lane_h

    @pl.when(jnp.logical_and(is_lat, t == 0))
    def _():
        for d, scr in ((0, sf_scr), (1, sb_scr)):
            s0t = jnp.concatenate([s0_ref[0, d].T] * HEADS, axis=0)
            scr[...] = jnp.where(blockdiag, s0t, 0.0)

    n, c = SEQ_TILE, GLA_CHUNK
    n_chunks = n // c

    def gates(lrf, lrb, dst):
        zf = _dot(lrf[...].astype(BF16), gu_ref[0].astype(BF16)) + gb_ref[0]
        zb = _dot(lrb[...].astype(BF16), gu_ref[1].astype(BF16)) + gb_ref[1]
        yield
        laf = _log_sigmoid(zf) / GLA_TAU
        lab = _log_sigmoid(zb) / GLA_TAU
        la_scr[dst, 0] = laf
        la_scr[dst, 1] = lab
        yield
        decay_scr[dst] = GLA_SUB * jnp.max(jnp.maximum(-laf, -lab))

    @pl.when(step == 0)
    def _():
        _interleave(gates(lrf_ref, lrb_ref, 0))

    next_gates = lambda: gates(lrf_next_ref, lrb_next_ref, 1 - slot)

    la = (la_scr[slot, 0], la_scr[slot, 1])
    decay = decay_scr[slot]
    refs = ((qkf_ref, vf_ref, of_ref, sf_scr), (qkb_ref, vb_ref, ob_ref, sb_scr))
    zero_state = jnp.zeros((GW, GLA_QK), F32)

    def tile(d, sub, st):
        qk_ref, v_ref, o_ref, _ = refs[d]
        rows = slice(sub * n, (sub + 1) * n)
        o, st = yield from _gla_tile(qk_ref[rows, 0:GLA_QK] * (GLA_DK ** -0.5), qk_ref[rows, GLA_QK:2 * GLA_QK],
                                     v_ref[rows, :], la[d][rows, :], st, rev=(d == 1))
        o_ref[rows, :] = o
        return st

    def tile_any_gates(d, sub, st):
        qk_ref, v_ref, o_ref, _ = refs[d]
        for cc in (range(n_chunks) if d == 0 else reversed(range(n_chunks))):
            rows = slice(sub * n + cc * c, sub * n + (cc + 1) * c)
            q = qk_ref[rows, 0:GLA_QK] * (GLA_DK ** -0.5)
            k = qk_ref[rows, GLA_QK:2 * GLA_QK]
            o, st = _gla_chunk(q, k, v_ref[rows, :], la[d][rows, :], st, rev=(d == 1))
            o_ref[rows, :] = o
            yield
        return st

    def context_request(method, d, sub):
        st = yield from method(d, sub, zero_state)
        st_ref[sub, 0, d] = _sum_head_blocks(st.T)

    def latent_block(method, d):
        st = refs[d][3][...]
        for sub in (range(SEQ_SUBS) if d == 0 else reversed(range(SEQ_SUBS))):
            st = yield from method(d, sub, st)
        refs[d][3][...] = st

    safe = decay <= GLA_SAFE_DECAY
    for method, cond in ((tile, safe), (tile_any_gates, jnp.logical_not(safe))):
        @pl.when(jnp.logical_and(cond, jnp.logical_not(is_lat)))
        def _():
            _interleave(*[context_request(method, d, sub) for sub in range(SEQ_SUBS) for d in range(2)],
                        next_gates())

        @pl.when(jnp.logical_and(cond, is_lat))
        def _():
            _interleave(latent_block(method, 0), latent_block(method, 1), next_gates())


def _gla(parts, gate_up_pad, gate_b, s0, layer, new_state):
    n = SEQ_BLOCK
    in_specs = []
    for blk in (3, 4):
        in_specs += [
            pl.BlockSpec((n, GW), lambda s, blk=blk: (_seq_step(s)[blk], CB_AQK)),
            pl.BlockSpec((n, GW), lambda s, blk=blk: (_seq_step(s)[blk], CB_AV)),
            pl.BlockSpec((n, 128), lambda s, blk=blk: (_seq_step(s)[blk], CB_LR128)),
        ]
    nxt = lambda s: jnp.minimum(s + 1, SEQ_STEPS - 1)
    in_specs += [pl.BlockSpec((n, 128), lambda s, blk=blk: (_seq_step(nxt(s))[blk], CB_LR128)) for blk in (3, 4)]
    in_specs += [pl.BlockSpec((2, 128, GLA_QK), lambda s: (0, 0, 0)),
                 pl.BlockSpec((2, 1, GLA_QK), lambda s: (0, 0, 0)),
                 pl.BlockSpec((1, 2, GLA_QK, HD), lambda s: (_seq_step(s)[1], 0, 0, 0)),
                 pl.BlockSpec(memory_space=pl.ANY)]
    return pl.pallas_call(
        _gla_kernel,
        input_output_aliases={len(in_specs) - 1: 2},
        grid=(SEQ_STEPS,),
        in_specs=in_specs,
        out_specs=[
            pl.BlockSpec((n, GW), lambda s: (_seq_step(s)[3], 0)),
            pl.BlockSpec((n, GW), lambda s: (_seq_step(s)[4], 0)),
            pl.BlockSpec((SEQ_SUBS, 1, 2, GLA_QK, HD), lambda s: (jnp.minimum(s, LAT_STEP0 - 1), layer, 0, 0, 0)),
        ],
        out_shape=[
            jax.ShapeDtypeStruct((T, GW), F32),
            jax.ShapeDtypeStruct((T, GW), F32),
            jax.ShapeDtypeStruct(new_state.shape, F32),
        ],
        scratch_shapes=[pltpu.VMEM((GW, GLA_QK), F32), pltpu.VMEM((GW, GLA_QK), F32),
                        pltpu.VMEM((2, 2, n, GLA_QK), F32), pltpu.SMEM((2,), F32)],
        compiler_params=_cparams(("arbitrary",)),
        name="gla",
    )(parts, parts, parts, parts, parts, parts, parts, parts, gate_up_pad, gate_b, s0, new_state)


OUT_TM = 512
FFN_TF = 1408
FFN_PASSES = D_FF // FFN_TF
OUT_TILES = T // OUT_TM


def _run_in_order(order):
    results = {}
    for gen in order:
        try:
            next(gen)
        except StopIteration as stop:
            results[gen] = stop.value
    return results


def _outproj_ffn_kernel(*refs, n_x, n_out):
    x_refs = refs[:n_x]
    (mod_ref, modp_ref, gpost_ref, gpre2_ref, gpost2_ref, pool_ref, na_ref, rf_ref, rb_ref, rg_ref, af_ref, ab_ref,
     ag_ref, ng_ref, w_ref, wg_ref, wu_ref, wd_ref) = refs[n_x:n_x + 18]
    o_refs = refs[n_x + 18:n_x + 18 + n_out]
    x1_scr, hb_scr = refs[n_x + 18 + n_out:]
    i = pl.program_id(0)
    wslot = i % 2
    rslot = 1 - wslot

    @pl.when(i == 0)
    def _():
        x1_scr[1] = jnp.zeros((OUT_TM, D), F32)
        hb_scr[1] = jnp.zeros((OUT_TM, D), BF16)

    avg = jnp.where(_iota((GW, GW), 0) // HD == _iota((GW, GW), 1) // HD, 1.0 / HD, 0.0).astype(BF16)
    x_tile = _x_tile(x_refs, OUT_TM, jnp.minimum(i, OUT_TILES - 1))

    def mixer_tail(rows):
        r = rf_ref[rows, :] + rb_ref[rows, :]
        r = r - _head_mean(r, avg)
        r = r * lax.rsqrt(_head_mean(r * r, avg) + GN_EPS) * _silu(rg_ref[rows, :])
        a = af_ref[rows, :] + ab_ref[rows, :]
        a = a * lax.rsqrt(_head_mean(a * a, avg) + RMS_EPS) * ng_ref[...] * _silu(ag_ref[rows, :])
        yield
        y = _dot(pool_ref[rows, :].astype(BF16), w_ref[0, 0:GW, :])
        y = y + _dot(na_ref[rows, :].astype(BF16), w_ref[0, GW:2 * GW, :])
        y = y + _dot(r.astype(BF16), w_ref[0, 2 * GW:3 * GW, :])
        y = y + _dot(a.astype(BF16), w_ref[0, 3 * GW:4 * GW, :])
        yield
        x1 = x_tile[rows] + mod_ref[0, 2:3, :] * _rms(y, gpost_ref[...])
        x1_scr[wslot, rows, :] = x1
        hb_scr[wslot, rows, :] = (_rms(x1, gpre2_ref[...]) * (1.0 + mod_ref[0, 4:5, :])
                                  + mod_ref[0, 3:4, :]).astype(BF16)

    def ffn(rows):
        hb = hb_scr[rslot, rows, :]
        y = None
        for j in range(FFN_PASSES):
            cols = slice(j * FFN_TF, (j + 1) * FFN_TF)
            act = (_silu(_dot(hb, wg_ref[0, :, cols])) * _dot(hb, wu_ref[0, :, cols])).astype(BF16)
            yield
            part = _dot(act, wd_ref[0, cols, :])
            y = part if y is None else y + part
            yield
        return x1_scr[rslot, rows, :] + modp_ref[0, 5:6, :] * _rms(y, gpost2_ref[...])

    half = OUT_TM // 2
    fa, fb = ffn(slice(0, half)), ffn(slice(half, OUT_TM))
    ta, tb = mixer_tail(slice(0, half)), mixer_tail(slice(half, OUT_TM))
    done = _run_in_order([fa, ta, fa, ta, fa, ta, fa, fb, tb, fa, fb, tb, fb, tb, fb, fb])
    x2 = jnp.concatenate([done[fa], done[fb]], axis=0)
    if len(o_refs) == 1:
        o_refs[0][...] = x2
    else:
        is_ctx = i - 1 < T_CTX // OUT_TM

        @pl.when(is_ctx)
        def _():
            o_refs[0][...] = x2

        @pl.when(jnp.logical_not(is_ctx))
        def _():
            o_refs[1][...] = x2


def _outproj_ffn(x, mod_l, g_post, g_pre2, g_post2, o_pool, o_na, ret_f, ret_b, gla_f, gla_b, parts, ng_lanes,
                 w_out_b, wg, wu, wd, layer, split_out):
    tm = OUT_TM
    cur = lambda i: jnp.minimum(i, OUT_TILES - 1)
    prev = lambda i: jnp.maximum(i - 1, 0)
    act = pl.BlockSpec((tm, GW), lambda i: (cur(i), 0))
    vec = pl.BlockSpec((1, D), lambda i: (0, 0))
    once = pl.Buffered(1)
    x_specs, x_args = _x_specs(x, tm, cur)
    if split_out:
        out_specs, _ = _x_specs((None, None), tm, prev)
        out_shape = [jax.ShapeDtypeStruct((T_CTX, D), F32), jax.ShapeDtypeStruct((T_LAT, D), F32)]
    else:
        out_specs = [pl.BlockSpec((tm, D), lambda i: (prev(i), 0))]
        out_shape = [jax.ShapeDtypeStruct((T, D), F32)]
    out = pl.pallas_call(
        functools.partial(_outproj_ffn_kernel, n_x=len(x_args), n_out=len(out_shape)),
        grid=(OUT_TILES + 1,),
        in_specs=x_specs + [
            pl.BlockSpec((1, 6, D), lambda i: (_mod_row(cur(i), tm), 0, 0)),
            pl.BlockSpec((1, 6, D), lambda i: (_mod_row(prev(i), tm), 0, 0)),
            vec, vec, vec,
            act, act, act, act,
            pl.BlockSpec((tm, GW), lambda i: (cur(i), CB_RG)),
            act, act,
            pl.BlockSpec((tm, GW), lambda i: (cur(i), CB_AG)),
            pl.BlockSpec((1, GW), lambda i: (0, 0)),
            pl.BlockSpec((1, D, D), lambda i: (layer, 0, 0), pipeline_mode=once),
            pl.BlockSpec((1, D, D_FF), lambda i: (layer, 0, 0), pipeline_mode=once),
            pl.BlockSpec((1, D, D_FF), lambda i: (layer, 0, 0), pipeline_mode=once),
            pl.BlockSpec((1, D_FF, D), lambda i: (layer, 0, 0), pipeline_mode=once),
        ],
        out_specs=out_specs,
        out_shape=out_shape,
        scratch_shapes=[pltpu.VMEM((2, tm, D), F32), pltpu.VMEM((2, tm, D), BF16)],
        compiler_params=_cparams(("arbitrary",)),
        name="outproj_ffn",
    )(*x_args, mod_l, mod_l, g_post, g_pre2, g_post2, o_pool, o_na, ret_f, ret_b, parts, gla_f, gla_b, parts, ng_lanes,
      w_out_b, wg, wu, wd)
    return tuple(out) if split_out else out[0]


def _rope_tables():
    nf = 16
    inv = (ROPE_BASE ** (-np.arange(nf, dtype=np.float32) / nf)).astype(np.float32)
    tok = np.arange(L_LAT)
    cos = np.zeros((L_LAT, HD), np.float32)
    sin = np.zeros((L_LAT, HD), np.float32)
    for axis, pos in enumerate((tok // GRID_W, tok % GRID_W)):
        ang = pos.astype(np.float32)[:, None] * inv[None, :]
        c, s = np.cos(ang), np.sin(ang)
        cos[:, axis * 32:axis * 32 + 32] = np.concatenate([c, c], axis=1)
        sin[:, axis * 32:axis * 32 + 32] = np.concatenate([-s, s], axis=1)
    return jnp.asarray(np.tile(cos, (1, HEADS))), jnp.asarray(np.tile(sin, (1, HEADS)))


def _block_diag(w):
    g, c, _ = w.shape
    out = jnp.zeros((g * c, g * c), w.dtype)
    for i in range(g):
        out = out.at[i * c:(i + 1) * c, i * c:(i + 1) * c].set(w[i])
    return out


def kernel(x_prompt, x_sample, cache_na_k, cache_na_v, state_ret, state_gla, c, c_ctx, w_mod, b_mod,
           g_pre_mix, g_post_mix, g_pre_ffn, g_post_ffn, w_in, w_out, pool_w, pool_scale, na_rpb,
           ret_decay_logit, gla_gate_up, gla_gate_b, gla_norm_g, w_ffn_gate, w_ffn_up, w_ffn_down):
    x = (x_prompt.reshape(T_CTX, D), x_sample.reshape(T_LAT, D))
    cv8 = jnp.concatenate([c_ctx[None, :], c, jnp.zeros((8 - 1 - B_LAT, D), F32)], axis=0)
    mods = _modulation(cv8, w_mod, b_mod).reshape(DEPTH, 8, 6, D)

    w_in_b = w_in.astype(BF16)
    w_out_b = w_out.astype(BF16)
    wg_b, wu_b, wd_b = w_ffn_gate.astype(BF16), w_ffn_up.astype(BF16), w_ffn_down.astype(BF16)
    gate_up_pad = jnp.pad(gla_gate_up, ((0, 0), (0, 0), (0, 128 - GLA_LOWRANK), (0, 0)))
    rope_tabs = _rope_tables()
    ck = cache_na_k.reshape(B_LAT, DEPTH, PAST, GW)
    cv = cache_na_v.reshape(B_LAT, DEPTH, PAST, GW)
    s0_ret = state_ret.reshape(B_LAT, DEPTH, 2, GW, HD)
    s0_gla = state_gla.reshape(B_LAT, DEPTH, 2, GLA_QK, HD)

    new_k = jnp.zeros((B_CTX, DEPTH, L_CTX, GW), F32)
    new_v = jnp.zeros((B_CTX, DEPTH, L_CTX, GW), F32)
    new_ret = jnp.zeros((B_CTX, DEPTH, 2, GW, HD), F32)
    new_gla = jnp.zeros((B_CTX, DEPTH, 2, GLA_QK, HD), F32)
    for l in range(DEPTH):
        mod_l = mods[l]
        parts = _inproj(x, mod_l, g_pre_mix[l][None, :], w_in_b, l)
        o_pool = _pool(parts, _block_diag(pool_w[l]).astype(BF16), pool_scale[l][None, :])
        o_na, new_k, new_v = _attention(parts, ck, cv, na_rpb[l], l, new_k, new_v)
        lg_lanes = jnp.repeat(ret_decay_logit[l], HD, axis=1)
        rf, rb, new_ret = _retention(parts, lg_lanes, rope_tabs, s0_ret[:, l], l, new_ret)
        gf, gbw, new_gla = _gla(parts, gate_up_pad[l], gla_gate_b[l][:, None, :], s0_gla[:, l], l, new_gla)
        x = _outproj_ffn(x, mod_l, g_post_mix[l][None, :], g_pre_ffn[l][None, :], g_post_ffn[l][None, :],
                         o_pool, o_na, rf, rb, gf, gbw, parts, jnp.tile(gla_norm_g[l], HEADS)[None, :],
                         w_out_b, wg_b, wu_b, wd_b, l, split_out=(l == DEPTH - 1))

    return (x[0].reshape(B_CTX, L_CTX, D), x[1].reshape(B_LAT, L_LAT, D),
            new_k.reshape(B_CTX, DEPTH, L_CTX, HEADS, HD), new_v.reshape(B_CTX, DEPTH, L_CTX, HEADS, HD),
            new_ret.reshape(B_CTX, DEPTH, 2, HEADS, RET_DK, HD), new_gla.reshape(B_CTX, DEPTH, 2, HEADS, GLA_DK, HD))
```

```python
import functools

import numpy as np
import jax
import jax.numpy as jnp
from jax import lax
from jax.experimental import pallas as pl
from jax.experimental.pallas import tpu as pltpu

F32 = jnp.float32
BF16 = jnp.bfloat16

D = 1024
B_CTX, L_CTX = 32, 256
B_LAT, L_LAT = 2, 4096
DEPTH = 4
PAST = 256
GRID_W = 64
GRID_H = L_LAT // GRID_W
T_CTX = B_CTX * L_CTX
T_LAT = B_LAT * L_LAT
T = T_CTX + T_LAT
GW = 256
HEADS = 4
HD = 64
POOL_WINDOWS = (2, 4, 8, 16)
NA_ROWS, NA_COLS = 8, 16
RET_DK = 64
GLA_DK = 32
GLA_LOWRANK = 16
GLA_TAU = 16.0
D_FF = 2816
P_IN = 2832
P_PAD = 2944
P_MAIN = 2816
ROPE_BASE = 10000.0
RMS_EPS = 1e-6
GN_EPS = 1e-5
NEG = -1e30

CB_POOL, CB_NAQ, CB_NAK, CB_NAV, CB_RQ, CB_RK, CB_RV, CB_RG, CB_AQK, CB_AV, CB_AG = range(11)
CB_LR128 = P_IN // 128

SEQ_TILE = L_CTX
RET_SUBS = 4
GLA_SUBS = 2
GLA_CHUNK = 64
GLA_SUB = 16
VMEM_LIMIT = 56 * 1024 * 1024


def _cparams(sem):
    return pltpu.CompilerParams(dimension_semantics=sem, vmem_limit_bytes=VMEM_LIMIT)


def _silu(x):
    return x / (1.0 + jnp.exp(-x))


def _log_sigmoid(z):
    return jnp.minimum(z, 0.0) - jnp.log1p(jnp.exp(-jnp.abs(z)))


def _rms(x, g):
    return x * lax.rsqrt(jnp.mean(x * x, axis=-1, keepdims=True) + RMS_EPS) * g


def _dot(a, b):
    return jnp.dot(a, b, preferred_element_type=F32)


def _dot_nt(a, b):
    return lax.dot_general(a, b, (((1,), (1,)), ((), ())), preferred_element_type=F32)


def _dot_tn(a, b):
    return lax.dot_general(a, b, (((0,), (0,)), ((), ())), preferred_element_type=F32)


def _split_hi_lo(x):
    hi = x.astype(BF16)
    return hi, (x - hi.astype(F32)).astype(BF16)


def _dot_exact01(a01, x):
    hi, lo = _split_hi_lo(x)
    return _dot(a01, hi) + _dot(a01, lo)


def _iota(shape, dim):
    return lax.broadcasted_iota(jnp.int32, shape, dim)


def _expand_heads(x, head_w):
    n, w = x.shape
    xe = jnp.concatenate([x] * HEADS, axis=0)
    rowh = _iota((HEADS * n, w), 0) // n
    laneh = (_iota((HEADS * n, w), 1) // head_w) % HEADS
    return jnp.where(rowh == laneh, xe, jnp.zeros_like(xe))


def _extract_heads(p, n):
    laneh = _iota((n, GW), 1) // HD
    out = p[0:n]
    for h in range(1, HEADS):
        out = jnp.where(laneh == h, p[h * n:(h + 1) * n], out)
    return out


def _head_mean(x, avg):
    hi, lo = _split_hi_lo(x)
    return _dot(hi, avg) + _dot(lo, avg)


def _interleave(*gens):
    results = [None] * len(gens)
    live = list(range(len(gens)))
    while live:
        for i in list(live):
            try:
                next(gens[i])
            except StopIteration as stop:
                results[i] = stop.value
                live.remove(i)
    return results


def _mod_row(i, tm):
    return jnp.where(i < T_CTX // tm, 0, 1 + (i * tm - T_CTX) // L_LAT)


def _mod_kernel(cv_ref, w_ref, b_ref, o_ref):
    s = _silu(cv_ref[...]).astype(BF16)
    o_ref[0] = _dot(s, w_ref[0].astype(BF16)) + b_ref[0]


def _modulation(cv8, w_mod, b_mod):
    tn = 1536
    return pl.pallas_call(
        _mod_kernel,
        grid=(DEPTH, 6 * D // tn),
        in_specs=[
            pl.BlockSpec((8, D), lambda l, j: (0, 0)),
            pl.BlockSpec((1, D, tn), lambda l, j: (l, 0, j)),
            pl.BlockSpec((1, 1, tn), lambda l, j: (l, 0, j)),
        ],
        out_specs=pl.BlockSpec((1, 8, tn), lambda l, j: (l, 0, j)),
        out_shape=jax.ShapeDtypeStruct((DEPTH, 8, 6 * D), F32),
        compiler_params=_cparams(("arbitrary", "arbitrary")),
        name="modulation",
    )(cv8, w_mod, b_mod.reshape(DEPTH, 1, 6 * D))


IN_TM = 512


def _x_specs(x, tm, tile_of=lambda i: i):
    if not isinstance(x, tuple):
        return [pl.BlockSpec((tm, D), lambda i: (tile_of(i), 0))], [x]
    nc = T_CTX // tm
    return ([pl.BlockSpec((tm, D), lambda i: (jnp.minimum(tile_of(i), nc - 1), 0)),
             pl.BlockSpec((tm, D), lambda i: (jnp.maximum(tile_of(i) - nc, 0), 0))], list(x))


def _x_tile(x_refs, tm, tile):
    if len(x_refs) == 1:
        return x_refs[0][...]
    return jnp.where(tile < T_CTX // tm, x_refs[0][...], x_refs[1][...])


def _inproj_kernel(*refs, n_x):
    x_refs, (mod_ref, g_ref, w_ref, wlr_ref, o_ref) = refs[:n_x], refs[n_x:]
    x = _x_tile(x_refs, IN_TM, pl.program_id(0))
    w_lr = jnp.where(_iota((D, P_PAD - P_MAIN), 1) < P_IN - P_MAIN, wlr_ref[0], jnp.zeros((), BF16))

    def half_tile(rows):
        h = _rms(x[rows], g_ref[...]) * (1.0 + mod_ref[0, 1:2, :]) + mod_ref[0, 0:1, :]
        hb = h.astype(BF16)
        yield
        for a in range(0, P_MAIN, 1024):
            b = min(a + 1024, P_MAIN)
            o_ref[rows, a:b] = _dot(hb, w_ref[0, :, a:b])
            yield
        o_ref[rows, P_MAIN:P_PAD] = _dot(hb, w_lr)

    half = IN_TM // 2
    first, second = half_tile(slice(0, half)), half_tile(slice(half, IN_TM))
    next(first)
    _interleave(first, second)


def _inproj(x, mod_l, g_pre, w_in_b, layer):
    tm = IN_TM
    x_specs, x_args = _x_specs(x, tm)
    return pl.pallas_call(
        functools.partial(_inproj_kernel, n_x=len(x_args)),
        grid=(T // tm,),
        in_specs=x_specs + [
            pl.BlockSpec((1, 6, D), lambda i: (_mod_row(i, tm), 0, 0)),
            pl.BlockSpec((1, D), lambda i: (0, 0)),
            pl.BlockSpec((1, D, P_MAIN), lambda i: (layer, 0, 0), pipeline_mode=pl.Buffered(1)),
            pl.BlockSpec((1, D, P_PAD - P_MAIN), lambda i: (layer, 0, P_MAIN // (P_PAD - P_MAIN)),
                         pipeline_mode=pl.Buffered(1)),
        ],
        out_specs=pl.BlockSpec((tm, P_PAD), lambda i: (i, 0)),
        out_shape=jax.ShapeDtypeStruct((T, P_PAD), F32),
        compiler_params=_cparams(("arbitrary",)),
        name="inproj",
    )(*x_args, mod_l, g_pre, w_in_b, w_in_b)


POOL_TM = 1024
POOL_SUB = 256


def _pool_kernel(v_ref, w_ref, scale_ref, o_ref, band_scr, cnt_scr):
    i = pl.program_id(0)
    n = POOL_SUB

    @pl.when(i == 0)
    def _():
        t = _iota((n, n), 0)
        s = _iota((n, n), 1)
        lane_g = _iota((n, GW), 1) // HD
        for kind, seg_len in enumerate((L_CTX, GRID_W)):
            seg0 = t & ~(seg_len - 1)
            seg1 = seg0 + seg_len
            cnt = jnp.zeros((n, GW), F32)
            for gi, win in enumerate(POOL_WINDOWS):
                lo = jnp.maximum(t - win // 2, seg0)
                hi = jnp.minimum(t - win // 2 + win, seg1)
                band_scr[kind, gi] = jnp.where(s >= lo, jnp.where(s < hi, 1.0, 0.0), 0.0).astype(BF16)
                cnt = jnp.where(lane_g == gi, (hi - lo).astype(F32), cnt)
            cnt_scr[kind] = cnt

    kind = jnp.where(i < T_CTX // POOL_TM, 0, 1)

    def piece(rows):
        v = v_ref[rows, :]
        vh, vl = _split_hi_lo(v)
        lane_g = _iota((n, GW), 1) // HD
        yield
        mean = None
        for gi in range(len(POOL_WINDOWS)):
            band = band_scr[kind, gi]
            m = _dot(band, vh) + _dot(band, vl)
            mean = m if mean is None else jnp.where(lane_g == gi, m, mean)
        yield
        d = (mean / cnt_scr[kind] - v).astype(BF16)
        o_ref[rows, :] = (_dot(d, w_ref[...]) * scale_ref[...]).astype(o_ref.dtype)

    _interleave(*[piece(slice(j * n, (j + 1) * n)) for j in range(POOL_TM // n)])


def _pool(parts, w_bd, scale):
    tm = POOL_TM
    n_win = len(POOL_WINDOWS)
    return pl.pallas_call(
        _pool_kernel,
        grid=(T // tm,),
        in_specs=[
            pl.BlockSpec((tm, GW), lambda i: (i, CB_POOL)),
            pl.BlockSpec((GW, GW), lambda i: (0, 0)),
            pl.BlockSpec((1, GW), lambda i: (0, 0)),
        ],
        out_specs=pl.BlockSpec((tm, GW), lambda i: (i, 0)),
        out_shape=jax.ShapeDtypeStruct((T, GW), BF16),
        scratch_shapes=[pltpu.VMEM((2, n_win, POOL_SUB, POOL_SUB), BF16), pltpu.VMEM((2, POOL_SUB, GW), F32)],
        compiler_params=_cparams(("arbitrary",)),
        name="pool",
    )(parts, w_bd, scale)


def _softmax_rows(s):
    m = jnp.max(s, axis=-1, keepdims=True)
    p = jnp.exp(s - m)
    return p / jnp.sum(p, axis=-1, keepdims=True)


def _ctx_attn_rows(q_ref, k_ref, v_ref, o_ref, rows):
    qe = _expand_heads(q_ref[rows, :].astype(BF16), HD)
    yield
    s = _dot_nt(qe, k_ref[rows, :].astype(BF16)) * (HD ** -0.5)
    yield
    p = _softmax_rows(s).astype(BF16)
    yield
    o_ref[rows, :] = _extract_heads(_dot(p, v_ref[rows, :].astype(BF16)), L_CTX).astype(o_ref.dtype)


NA_ROWS_PER_STEP = 8
NA_INTERLEAVE = 8
ATT_ROWS = NA_ROWS_PER_STEP * GRID_W
ATT_CTX_STEPS = T_CTX // ATT_ROWS
ATT_LAT_STEPS = L_LAT // ATT_ROWS
NA_WIN = NA_ROWS * GRID_W
NA_DR = 2 * NA_ROWS - 1
NA_DC = 2 * NA_COLS - 1


def _na_bias_table(rpb_ref, e2_ref):
    shape = (GRID_W, 2 * GRID_W)
    qc = _iota(shape, 0)
    lane = _iota(shape, 1)
    kc = lane % GRID_W
    upper = lane >= GRID_W
    c0 = jnp.clip(qc - NA_COLS // 2, 0, GRID_W - NA_COLS)
    dc = jnp.where((kc >= c0) & (kc < c0 + NA_COLS), kc - qc + (NA_COLS - 1), -1)

    def one(ha, carry):
        h = ha // (NA_DR - 1)
        a = ha % (NA_DR - 1)
        acc = jnp.full(shape, NEG, F32)
        for j in range(NA_DC):
            val = jnp.where(upper, rpb_ref[h * NA_DR + a + 1, j], rpb_ref[h * NA_DR + a, j])
            acc = jnp.where(dc == j, val, acc)
        e2_ref[h, a] = acc
        return carry

    lax.fori_loop(0, HEADS * (NA_DR - 1), one, 0)


def _attn_kernel(q_ref, k_ref, v_ref, kseq_ref, vseq_ref, ck_ref, cv_ref, rpb_ref, kprev_ref, vprev_ref,
                 o_ref, ko_ref, vo_ref, kb_ref, vb_ref, e2_ref):
    del kprev_ref, vprev_ref
    s = pl.program_id(0)

    @pl.when(s < ATT_CTX_STEPS)
    def _():
        for i in range(ATT_ROWS // L_CTX):
            ko_ref[i, 0] = k_ref[i * L_CTX:(i + 1) * L_CTX, :]
            vo_ref[i, 0] = v_ref[i * L_CTX:(i + 1) * L_CTX, :]
        _interleave(*[_ctx_attn_rows(q_ref, k_ref, v_ref, o_ref, slice(i * L_CTX, (i + 1) * L_CTX))
                      for i in range(ATT_ROWS // L_CTX)])

    @pl.when(s >= ATT_CTX_STEPS)
    def _():
        step = (s - ATT_CTX_STEPS) % ATT_LAT_STEPS

        @pl.when(s == ATT_CTX_STEPS)
        def _():
            _na_bias_table(rpb_ref, e2_ref)

        @pl.when(step == 0)
        def _():
            kb_ref[...] = kseq_ref[...].astype(BF16)
            vb_ref[...] = vseq_ref[...].astype(BF16)

        _na_rows(step, q_ref, ck_ref, cv_ref, o_ref, kb_ref, vb_ref, e2_ref)


def _na_rows(step, q_ref, ck_ref, cv_ref, o_ref, kb_ref, vb_ref, e2_ref):
    ckb = ck_ref[0, 0].astype(BF16)
    cvb = cv_ref[0, 0].astype(BF16)
    scale = HD ** -0.5

    def one_row(rr):
        r = step * NA_ROWS_PER_STEP + rr
        r0 = jnp.clip(r - NA_ROWS // 2, 0, GRID_H - NA_ROWS)
        base = r0 - r + (NA_ROWS - 1)
        q0 = pl.multiple_of(rr * GRID_W, GRID_W)
        k0 = pl.multiple_of(r0 * GRID_W, GRID_W)
        qe = _expand_heads(q_ref[pl.ds(q0, GRID_W), :].astype(BF16), HD)
        kw = kb_ref[pl.ds(k0, NA_WIN), :]
        vw = vb_ref[pl.ds(k0, NA_WIN), :]
        bias = jnp.concatenate(
            [jnp.concatenate([e2_ref[h, base + 2 * p] for p in range(NA_ROWS // 2)], axis=1)
             for h in range(HEADS)], axis=0)
        yield
        s_loc = _dot_nt(qe, kw) * scale + bias
        s_ctx = _dot_nt(qe, ckb) * scale
        yield
        m = jnp.maximum(jnp.max(s_loc, axis=-1, keepdims=True), jnp.max(s_ctx, axis=-1, keepdims=True))
        p_loc = jnp.exp(s_loc - m)
        p_ctx = jnp.exp(s_ctx - m)
        inv = 1.0 / (jnp.sum(p_loc, axis=-1, keepdims=True) + jnp.sum(p_ctx, axis=-1, keepdims=True))
        yield
        pv = _dot((p_loc * inv).astype(BF16), vw) + _dot((p_ctx * inv).astype(BF16), cvb)
        yield
        o_ref[pl.ds(q0, GRID_W), :] = _extract_heads(pv, GRID_W).astype(o_ref.dtype)

    def row_group(i, carry):
        _interleave(*[one_row(NA_INTERLEAVE * i + j) for j in range(NA_INTERLEAVE)])
        return carry

    lax.fori_loop(0, NA_ROWS_PER_STEP // NA_INTERLEAVE, row_group, 0)


def _attention(parts, ck, cv, rpb, layer, new_k, new_v):
    lat_req = lambda s: jnp.maximum(s - ATT_CTX_STEPS, 0) // ATT_LAT_STEPS
    seq_blk0 = T_CTX // L_LAT
    per_step = ATT_ROWS // L_CTX
    cache_spec = pl.BlockSpec((per_step, 1, L_CTX, GW), lambda s: (jnp.minimum(s, ATT_CTX_STEPS - 1), layer, 0, 0))
    return pl.pallas_call(
        _attn_kernel,
        grid=(T // ATT_ROWS,),
        in_specs=[
            pl.BlockSpec((ATT_ROWS, GW), lambda s: (s, CB_NAQ)),
            pl.BlockSpec((ATT_ROWS, GW), lambda s: (s, CB_NAK)),
            pl.BlockSpec((ATT_ROWS, GW), lambda s: (s, CB_NAV)),
            pl.BlockSpec((L_LAT, GW), lambda s: (seq_blk0 + lat_req(s), CB_NAK)),
            pl.BlockSpec((L_LAT, GW), lambda s: (seq_blk0 + lat_req(s), CB_NAV)),
            pl.BlockSpec((1, 1, PAST, GW), lambda s: (lat_req(s), layer, 0, 0)),
            pl.BlockSpec((1, 1, PAST, GW), lambda s: (lat_req(s), layer, 0, 0)),
            pl.BlockSpec(memory_space=pltpu.SMEM),
            pl.BlockSpec(memory_space=pl.ANY),
            pl.BlockSpec(memory_space=pl.ANY),
        ],
        out_specs=[pl.BlockSpec((ATT_ROWS, GW), lambda s: (s, 0)), cache_spec, cache_spec],
        out_shape=[jax.ShapeDtypeStruct((T, GW), BF16), jax.ShapeDtypeStruct(new_k.shape, F32),
                   jax.ShapeDtypeStruct(new_v.shape, F32)],
        input_output_aliases={8: 1, 9: 2},
        scratch_shapes=[pltpu.VMEM((L_LAT, GW), BF16), pltpu.VMEM((L_LAT, GW), BF16),
                        pltpu.VMEM((HEADS, NA_DR - 1, GRID_W, 2 * GRID_W), F32)],
        compiler_params=_cparams(("arbitrary",)),
        name="attention",
    )(parts, parts, parts, parts, parts, ck, cv, rpb.reshape(HEADS * NA_DR, NA_DC), new_k, new_v)


def _rope(x, cos, sin_signed):
    lane = _iota(x.shape, 1)
    partner = jnp.where(lane % 32 < 16, pltpu.roll(x, GW - 16, 1), pltpu.roll(x, 16, 1))
    return x * cos + partner * sin_signed


def _seq_steps(subs):
    blk = subs * SEQ_TILE
    return blk, T // blk, T_CTX // blk, L_LAT // blk


def _seq_step(s, subs):
    _, _, lat0, lat_blocks = _seq_steps(subs)
    is_lat = s >= lat0
    u = jnp.maximum(s - lat0, 0)
    b = u // lat_blocks
    t = u % lat_blocks
    bwd = jnp.where(is_lat, lat0 + b * lat_blocks + (lat_blocks - 1 - t), s)
    return is_lat, b, t, s, bwd


def _sum_head_blocks(st):
    return st[:, 0:HD] + st[:, HD:2 * HD] + st[:, 2 * HD:3 * HD] + st[:, 3 * HD:4 * HD]


def _ret_kernel(qf_ref, kf_ref, vf_ref, qb_ref, kb_ref, vb_ref, cf_ref, sf_ref, cb_ref, sb_ref, lg_ref, s0_ref,
                st_prev_ref, of_ref, ob_ref, st_ref, sf_scr, sb_scr, w_scr, dq_scr, dk_scr):
    del st_prev_ref
    s = pl.program_id(0)
    is_lat, _, t, _, _ = _seq_step(s, RET_SUBS)
    n = SEQ_TILE
    blockdiag = _iota((GW, GW), 0) // HD == _iota((GW, GW), 1) // HD
    lg = _log_sigmoid(lg_ref[...])

    @pl.when(s == 0)
    def _():
        ti = _iota((n, GW), 0).astype(F32)
        i_ = _iota((n, n), 0).astype(F32)
        j_ = _iota((n, n), 1).astype(F32)
        for d in range(2):
            lgd = lg[d:d + 1, :]
            diff = (i_ - j_) if d == 0 else (j_ - i_)
            pos = ti if d == 0 else (n - 1.0) - ti
            w_scr[d] = jnp.concatenate(
                [jnp.where(diff >= 0, jnp.exp(jnp.maximum(diff, 0.0) * lgd[:, h * HD:h * HD + 1]), 0.0)
                 for h in range(HEADS)], axis=0)
            dq_scr[d] = jnp.exp((pos + 1.0) * lgd)
            dk_scr[d] = jnp.exp((n - 1.0 - pos) * lgd)

    @pl.when(jnp.logical_and(is_lat, t == 0))
    def _():
        for d, scr in ((0, sf_scr), (1, sb_scr)):
            s0 = jnp.concatenate([s0_ref[0, d]] * HEADS, axis=1)
            scr[...] = jnp.where(blockdiag, s0, 0.0)

    fwd_refs = (qf_ref, kf_ref, vf_ref, cf_ref, sf_ref, of_ref)
    bwd_refs = (qb_ref, kb_ref, vb_ref, cb_ref, sb_ref, ob_ref)

    def tile(d, refs, sub, scr):
        q_ref, k_ref, v_ref, c_ref, s_ref, o_ref = refs
        rows = slice(sub * n, (sub + 1) * n)
        q = q_ref[rows, :] * (RET_DK ** -0.5)
        k = k_ref[rows, :]
        if scr is not None:
            q = _rope(q, c_ref[rows, :], s_ref[rows, :])
            k = _rope(k, c_ref[rows, :], s_ref[rows, :])
        vb16 = v_ref[rows, :].astype(BF16)
        yield
        a = _dot_nt(_expand_heads(q.astype(BF16), HD), k.astype(BF16)) * w_scr[d]
        yield
        o = _extract_heads(_dot(a.astype(BF16), vb16), n)
        yield
        if scr is not None:
            s_old = scr[...]
            o = o + _dot(q.astype(BF16), s_old.astype(BF16)) * dq_scr[d]
        o_ref[rows, :] = o
        yield
        upd = jnp.where(blockdiag, _dot_tn((k * dk_scr[d]).astype(BF16), vb16), 0.0)
        if scr is None:
            st_ref[sub, 0, d] = _sum_head_blocks(upd)
        else:
            scr[...] = s_old * jnp.exp(float(n) * lg[d:d + 1, :]) + upd

    def chain(d, refs, order, scr):
        for sub in order:
            yield from tile(d, refs, sub, scr)

    @pl.when(jnp.logical_not(is_lat))
    def _():
        _interleave(*[tile(d, (fwd_refs, bwd_refs)[d], sub, None) for sub in range(RET_SUBS) for d in range(2)])

    @pl.when(is_lat)
    def _():
        _interleave(chain(0, fwd_refs, range(RET_SUBS), sf_scr),
                    chain(1, bwd_refs, reversed(range(RET_SUBS)), sb_scr))


def _retention(parts, lg_lanes, rope_tabs, s0, layer, new_state):
    n = SEQ_TILE
    blk, steps, lat0, lat_blocks = _seq_steps(RET_SUBS)
    step = lambda s: _seq_step(s, RET_SUBS)
    fwd = lambda s: step(s)[3]
    bwd = lambda s: step(s)[4]
    in_specs = [pl.BlockSpec((blk, GW), lambda s, c=c: (fwd(s), c)) for c in (CB_RQ, CB_RK, CB_RV)]
    in_specs += [pl.BlockSpec((blk, GW), lambda s, c=c: (bwd(s), c)) for c in (CB_RQ, CB_RK, CB_RV)]
    in_specs += [pl.BlockSpec((blk, GW), lambda s: (step(s)[2], 0))] * 2
    in_specs += [pl.BlockSpec((blk, GW), lambda s: (lat_blocks - 1 - step(s)[2], 0))] * 2
    in_specs += [pl.BlockSpec((2, GW), lambda s: (0, 0)),
                 pl.BlockSpec((1, 2, GW, HD), lambda s: (step(s)[1], 0, 0, 0)),
                 pl.BlockSpec(memory_space=pl.ANY)]
    return pl.pallas_call(
        _ret_kernel,
        input_output_aliases={len(in_specs) - 1: 2},
        grid=(steps,),
        in_specs=in_specs,
        out_specs=[
            pl.BlockSpec((blk, GW), lambda s: (fwd(s), 0)),
            pl.BlockSpec((blk, GW), lambda s: (bwd(s), 0)),
            pl.BlockSpec((RET_SUBS, 1, 2, GW, HD), lambda s: (jnp.minimum(s, lat0 - 1), layer, 0, 0, 0)),
        ],
        out_shape=[
            jax.ShapeDtypeStruct((T, GW), F32),
            jax.ShapeDtypeStruct((T, GW), F32),
            jax.ShapeDtypeStruct(new_state.shape, F32),
        ],
        scratch_shapes=[pltpu.VMEM((GW, GW), F32), pltpu.VMEM((GW, GW), F32),
                        pltpu.VMEM((2, HEADS * n, n), F32), pltpu.VMEM((2, n, GW), F32),
                        pltpu.VMEM((2, n, GW), F32)],
        compiler_params=_cparams(("arbitrary",)),
        name="retention",
    )(parts, parts, parts, parts, parts, parts, rope_tabs[0], rope_tabs[1], rope_tabs[0], rope_tabs[1],
      lg_lanes, s0, new_state)


GLA_QK = HEADS * GLA_DK
N_SUB = GLA_CHUNK // GLA_SUB
GLA_SAFE_DECAY = 60.0


def _gla_tile(q, k, v, la, st, rev):
    n, c = SEQ_TILE, GLA_CHUNK
    n_chunks = n // c
    ri = _iota((n, n), 0)
    ci = _iota((n, n), 1)
    same_chunk = ri // c == ci // c
    causal = (ci >= ri) if rev else (ci <= ri)
    tri = jnp.where(same_chunk, jnp.where(causal, 1.0, 0.0), 0.0).astype(BF16)
    b = _dot_exact01(tri, la)
    yield

    def rows_of(idx, count):
        if idx is None:
            return jnp.zeros((count, GLA_QK), F32)
        return jnp.broadcast_to(b[idx:idx + 1, :], (count, GLA_QK))

    def ref_row(cc, s):
        if rev:
            return cc * c + (s + 1) * GLA_SUB if s < N_SUB - 1 else None
        return cc * c + s * GLA_SUB - 1 if s > 0 else None

    end_rows = [cc * c if rev else cc * c + c - 1 for cc in range(n_chunks)]
    own_ref = jnp.concatenate([rows_of(ref_row(cc, s), GLA_SUB) for cc in range(n_chunks) for s in range(N_SUB)],
                              axis=0)
    b_end = jnp.concatenate([rows_of(r, c) for r in end_rows], axis=0)
    sub = (_iota((n, GLA_QK), 0) // GLA_SUB) % N_SUB
    qh = q * jnp.exp(b - own_ref)
    q_parts, k_parts = [], []
    for s in range(N_SUB):
        kvalid = (sub >= s) if rev else (sub <= s)
        ref_s = jnp.concatenate([rows_of(ref_row(cc, s), c) for cc in range(n_chunks)], axis=0)
        q_parts.append(jnp.where(sub == s, qh, 0.0))
        k_parts.append(jnp.where(kvalid, k * jnp.exp(jnp.minimum(ref_s - b, GLA_SAFE_DECAY)), 0.0))
    q_cat = _expand_heads(jnp.concatenate(q_parts, axis=1).astype(BF16), GLA_DK)
    k_cat = jnp.concatenate(k_parts, axis=1).astype(BF16)
    yield
    a = _dot_nt(q_cat, k_cat)
    yield
    qi = _iota((HEADS * n, n), 0) % n
    kj = _iota((HEADS * n, n), 1)
    keep = (qi // c == kj // c) & ((kj >= qi) if rev else (kj <= qi))
    vb16 = v.astype(BF16)
    o_intra = _extract_heads(_dot(jnp.where(keep, a, 0.0).astype(BF16), vb16), n)
    yield

    qt = (q * jnp.exp(b)).astype(BF16)
    kt = (k * jnp.exp(b_end - b)).astype(BF16)
    blockdiag = _iota((GW, GLA_QK), 0) // HD == _iota((GW, GLA_QK), 1) // GLA_DK
    upd = [jnp.where(blockdiag, _dot_tn(vb16[cc * c:(cc + 1) * c], kt[cc * c:(cc + 1) * c]), 0.0)
           for cc in range(n_chunks)]
    yield
    o_inter = [None] * n_chunks
    for cc in (reversed(range(n_chunks)) if rev else range(n_chunks)):
        o_inter[cc] = _dot_nt(qt[cc * c:(cc + 1) * c], st.astype(BF16))
        st = st * jnp.exp(b[end_rows[cc]:end_rows[cc] + 1, :]) + upd[cc]
        yield
    return o_intra + jnp.concatenate(o_inter, axis=0), st


def _gla_chunk(q, k, v, la, st, rev):
    c = GLA_CHUNK
    ri = _iota((c, c), 0)
    ci = _iota((c, c), 1)
    tri = jnp.where((ci >= ri) if rev else (ci <= ri), 1.0, 0.0).astype(BF16)
    b = _dot_exact01(tri, la)
    b_end = b[0:1, :] if rev else b[c - 1:c, :]
    row = _iota((c, GLA_QK), 0)
    sub = row // GLA_SUB
    off = row % GLA_SUB

    o = _dot_nt((q * jnp.exp(b)).astype(BF16), st.astype(BF16))
    kt = (k * jnp.exp(b_end - b)).astype(BF16)
    lane_h = _iota((GW, GLA_QK), 1) // GLA_DK
    row_h = _iota((GW, GLA_QK), 0) // HD
    st_new = st * jnp.exp(b_end) + jnp.where(row_h == lane_h, _dot_tn(v.astype(BF16), kt), 0.0)

    q_parts, k_parts = [], []
    for s in range(1, N_SUB):
        if rev:
            qsub, brow = N_SUB - 1 - s, b[(N_SUB - s) * GLA_SUB:(N_SUB - s) * GLA_SUB + 1, :]
            kvalid = sub > qsub
        else:
            qsub, brow = s, b[s * GLA_SUB - 1:s * GLA_SUB, :]
            kvalid = sub < qsub
        q_parts.append(jnp.where(sub == qsub, q * jnp.exp(jnp.where(sub == qsub, b - brow, 0.0)), 0.0))
        k_parts.append(jnp.where(kvalid, k * jnp.exp(jnp.where(kvalid, brow - b, 0.0)), 0.0))
    q_cat = _expand_heads(jnp.concatenate(q_parts, axis=1).astype(BF16), GLA_DK)
    k_cat = jnp.concatenate(k_parts, axis=1).astype(BF16)
    a_off = _dot_nt(q_cat, k_cat)
    o = o + _extract_heads(_dot(a_off.astype(BF16), v.astype(BF16)), c)

    red = jnp.where(_iota((GLA_QK, GW), 0) // GLA_DK == _iota((GLA_QK, GW), 1) // HD, 1.0, 0.0).astype(BF16)
    rowv = _iota((c, GW), 0) % GLA_SUB
    for dl in range(GLA_SUB):
        if dl == 0:
            x = q * k
            vs = v
        else:
            sh = dl if not rev else c - dl
            valid = (off + dl < GLA_SUB) if rev else (off >= dl)
            ks = pltpu.roll(k, sh, 0)
            bs = pltpu.roll(b, sh, 0)
            vs = pltpu.roll(v, sh, 0)
            x = jnp.where(valid, q * ks * jnp.exp(jnp.where(valid, b - bs, 0.0)), 0.0)
            validv = (rowv + dl < GLA_SUB) if rev else (rowv >= dl)
            vs = jnp.where(validv, vs, 0.0)
        o = o + _dot(x.astype(BF16), red) * vs
    return o, st_new


def _gla_kernel(qkf_ref, vf_ref, lrf_ref, qkb_ref, vb_ref, lrb_ref, lrf_next_ref, lrb_next_ref, gu_ref, gb_ref,
                s0_ref, st_prev_ref, of_ref, ob_ref, st_ref, sf_scr, sb_scr, la_scr, decay_scr):
    del st_prev_ref
    step = pl.program_id(0)
    slot = step % 2
    is_lat, _, t, _, _ = _seq_step(step, GLA_SUBS)
    lane_h = _iota((GW, GLA_QK), 1) // GLA_DK
    row_h = _iota((GW, GLA_QK), 0) // HD
    blockdiag = row_h == lane_h

    @pl.when(jnp.logical_and(is_lat, t == 0))
    def _():
        for d, scr in ((0, sf_scr), (1, sb_scr)):
            s0t = jnp.concatenate([s0_ref[0, d].T] * HEADS, axis=0)
            scr[...] = jnp.where(blockdiag, s0t, 0.0)

    n, c = SEQ_TILE, GLA_CHUNK
    n_chunks = n // c

    def gates(lrf, lrb, dst):
        zf = _dot(lrf[...].astype(BF16), gu_ref[0].astype(BF16)) + gb_ref[0]
        zb = _dot(lrb[...].astype(BF16), gu_ref[1].astype(BF16)) + gb_ref[1]
        yield
        laf = _log_sigmoid(zf) / GLA_TAU
        lab = _log_sigmoid(zb) / GLA_TAU
        la_scr[dst, 0] = laf
        la_scr[dst, 1] = lab
        yield
        decay_scr[dst] = GLA_SUB * jnp.max(jnp.maximum(-laf, -lab))

    @pl.when(step == 0)
    def _():
        _interleave(gates(lrf_ref, lrb_ref, 0))

    next_gates = lambda: gates(lrf_next_ref, lrb_next_ref, 1 - slot)

    la = (la_scr[slot, 0], la_scr[slot, 1])
    decay = decay_scr[slot]
    refs = ((qkf_ref, vf_ref, of_ref, sf_scr), (qkb_ref, vb_ref, ob_ref, sb_scr))
    zero_state = jnp.zeros((GW, GLA_QK), F32)

    def tile(d, sub, st):
        qk_ref, v_ref, o_ref, _ = refs[d]
        rows = slice(sub * n, (sub + 1) * n)
        o, st = yield from _gla_tile(qk_ref[rows, 0:GLA_QK] * (GLA_DK ** -0.5), qk_ref[rows, GLA_QK:2 * GLA_QK],
                                     v_ref[rows, :], la[d][rows, :], st, rev=(d == 1))
        o_ref[rows, :] = o
        return st

    def tile_any_gates(d, sub, st):
        qk_ref, v_ref, o_ref, _ = refs[d]
        for cc in (range(n_chunks) if d == 0 else reversed(range(n_chunks))):
            rows = slice(sub * n + cc * c, sub * n + (cc + 1) * c)
            q = qk_ref[rows, 0:GLA_QK] * (GLA_DK ** -0.5)
            k = qk_ref[rows, GLA_QK:2 * GLA_QK]
            o, st = _gla_chunk(q, k, v_ref[rows, :], la[d][rows, :], st, rev=(d == 1))
            o_ref[rows, :] = o
            yield
        return st

    def context_request(method, d, sub):
        st = yield from method(d, sub, zero_state)
        st_ref[sub, 0, d] = _sum_head_blocks(st.T)

    def latent_block(method, d):
        st = refs[d][3][...]
        for sub in (range(GLA_SUBS) if d == 0 else reversed(range(GLA_SUBS))):
            st = yield from method(d, sub, st)
        refs[d][3][...] = st

    safe = decay <= GLA_SAFE_DECAY
    for method, cond in ((tile, safe), (tile_any_gates, jnp.logical_not(safe))):
        @pl.when(jnp.logical_and(cond, jnp.logical_not(is_lat)))
        def _():
            _interleave(*[context_request(method, d, sub) for sub in range(GLA_SUBS) for d in range(2)],
                        next_gates())

        @pl.when(jnp.logical_and(cond, is_lat))
        def _():
            _interleave(latent_block(method, 0), latent_block(method, 1), next_gates())


def _gla(parts, gate_up_pad, gate_b, s0, layer, new_state):
    n, steps, lat0, _ = _seq_steps(GLA_SUBS)
    in_specs = []
    for blk in (3, 4):
        in_specs += [
            pl.BlockSpec((n, GW), lambda s, blk=blk: (_seq_step(s, GLA_SUBS)[blk], CB_AQK)),
            pl.BlockSpec((n, GW), lambda s, blk=blk: (_seq_step(s, GLA_SUBS)[blk], CB_AV)),
            pl.BlockSpec((n, 128), lambda s, blk=blk: (_seq_step(s, GLA_SUBS)[blk], CB_LR128)),
        ]
    nxt = lambda s: jnp.minimum(s + 1, steps - 1)
    in_specs += [pl.BlockSpec((n, 128), lambda s, blk=blk: (_seq_step(nxt(s), GLA_SUBS)[blk], CB_LR128)) for blk in (3, 4)]
    in_specs += [pl.BlockSpec((2, 128, GLA_QK), lambda s: (0, 0, 0)),
                 pl.BlockSpec((2, 1, GLA_QK), lambda s: (0, 0, 0)),
                 pl.BlockSpec((1, 2, GLA_QK, HD), lambda s: (_seq_step(s, GLA_SUBS)[1], 0, 0, 0)),
                 pl.BlockSpec(memory_space=pl.ANY)]
    return pl.pallas_call(
        _gla_kernel,
        input_output_aliases={len(in_specs) - 1: 2},
        grid=(steps,),
        in_specs=in_specs,
        out_specs=[
            pl.BlockSpec((n, GW), lambda s: (_seq_step(s, GLA_SUBS)[3], 0)),
            pl.BlockSpec((n, GW), lambda s: (_seq_step(s, GLA_SUBS)[4], 0)),
            pl.BlockSpec((GLA_SUBS, 1, 2, GLA_QK, HD), lambda s: (jnp.minimum(s, lat0 - 1), layer, 0, 0, 0)),
        ],
        out_shape=[
            jax.ShapeDtypeStruct((T, GW), F32),
            jax.ShapeDtypeStruct((T, GW), F32),
            jax.ShapeDtypeStruct(new_state.shape, F32),
        ],
        scratch_shapes=[pltpu.VMEM((GW, GLA_QK), F32), pltpu.VMEM((GW, GLA_QK), F32),
                        pltpu.VMEM((2, 2, n, GLA_QK), F32), pltpu.SMEM((2,), F32)],
        compiler_params=_cparams(("arbitrary",)),
        name="gla",
    )(parts, parts, parts, parts, parts, parts, parts, parts, gate_up_pad, gate_b, s0, new_state)


OUT_TM = 512
FFN_TF = 1408
FFN_PASSES = D_FF // FFN_TF
OUT_TILES = T // OUT_TM


def _run_in_order(order):
    results = {}
    for gen in order:
        try:
            next(gen)
        except StopIteration as stop:
            results[gen] = stop.value
    return results


def _outproj_ffn_kernel(*refs, n_x, n_out):
    x_refs = refs[:n_x]
    (mod_ref, modp_ref, gpost_ref, gpre2_ref, gpost2_ref, pool_ref, na_ref, rf_ref, rb_ref, rg_ref, af_ref, ab_ref,
     ag_ref, ng_ref, w_ref, wg_ref, wu_ref, wd_ref) = refs[n_x:n_x + 18]
    o_refs = refs[n_x + 18:n_x + 18 + n_out]
    x1_scr, hb_scr = refs[n_x + 18 + n_out:]
    i = pl.program_id(0)
    wslot = i % 2
    rslot = 1 - wslot

    @pl.when(i == 0)
    def _():
        x1_scr[1] = jnp.zeros((OUT_TM, D), F32)
        hb_scr[1] = jnp.zeros((OUT_TM, D), BF16)

    avg = jnp.where(_iota((GW, GW), 0) // HD == _iota((GW, GW), 1) // HD, 1.0 / HD, 0.0).astype(BF16)
    x_tile = _x_tile(x_refs, OUT_TM, jnp.minimum(i, OUT_TILES - 1))

    def mixer_tail(rows):
        r = rf_ref[rows, :] + rb_ref[rows, :]
        r = r - _head_mean(r, avg)
        r = r * lax.rsqrt(_head_mean(r * r, avg) + GN_EPS) * _silu(rg_ref[rows, :])
        a = af_ref[rows, :] + ab_ref[rows, :]
        a = a * lax.rsqrt(_head_mean(a * a, avg) + RMS_EPS) * ng_ref[...] * _silu(ag_ref[rows, :])
        yield
        y = _dot(pool_ref[rows, :].astype(BF16), w_ref[0, 0:GW, :])
        y = y + _dot(na_ref[rows, :].astype(BF16), w_ref[0, GW:2 * GW, :])
        y = y + _dot(r.astype(BF16), w_ref[0, 2 * GW:3 * GW, :])
        y = y + _dot(a.astype(BF16), w_ref[0, 3 * GW:4 * GW, :])
        yield
        x1 = x_tile[rows] + mod_ref[0, 2:3, :] * _rms(y, gpost_ref[...])
        x1_scr[wslot, rows, :] = x1
        hb_scr[wslot, rows, :] = (_rms(x1, gpre2_ref[...]) * (1.0 + mod_ref[0, 4:5, :])
                                  + mod_ref[0, 3:4, :]).astype(BF16)

    def ffn(rows):
        hb = hb_scr[rslot, rows, :]
        y = None
        for j in range(FFN_PASSES):
            cols = slice(j * FFN_TF, (j + 1) * FFN_TF)
            act = (_silu(_dot(hb, wg_ref[0, :, cols])) * _dot(hb, wu_ref[0, :, cols])).astype(BF16)
            yield
            part = _dot(act, wd_ref[0, cols, :])
            y = part if y is None else y + part
            yield
        return x1_scr[rslot, rows, :] + modp_ref[0, 5:6, :] * _rms(y, gpost2_ref[...])

    half = OUT_TM // 2
    fa, fb = ffn(slice(0, half)), ffn(slice(half, OUT_TM))
    ta, tb = mixer_tail(slice(0, half)), mixer_tail(slice(half, OUT_TM))
    done = _run_in_order([fa, ta, fa, ta, fa, ta, fa, fb, tb, fa, fb, tb, fb, tb, fb, fb])
    x2 = jnp.concatenate([done[fa], done[fb]], axis=0)
    if len(o_refs) == 1:
        o_refs[0][...] = x2
    else:
        is_ctx = i - 1 < T_CTX // OUT_TM

        @pl.when(is_ctx)
        def _():
            o_refs[0][...] = x2

        @pl.when(jnp.logical_not(is_ctx))
        def _():
            o_refs[1][...] = x2


def _outproj_ffn(x, mod_l, g_post, g_pre2, g_post2, o_pool, o_na, ret_f, ret_b, gla_f, gla_b, parts, ng_lanes,
                 w_out_b, wg, wu, wd, layer, split_out):
    tm = OUT_TM
    cur = lambda i: jnp.minimum(i, OUT_TILES - 1)
    prev = lambda i: jnp.maximum(i - 1, 0)
    act = pl.BlockSpec((tm, GW), lambda i: (cur(i), 0))
    vec = pl.BlockSpec((1, D), lambda i: (0, 0))
    once = pl.Buffered(1)
    x_specs, x_args = _x_specs(x, tm, cur)
    if split_out:
        out_specs, _ = _x_specs((None, None), tm, prev)
        out_shape = [jax.ShapeDtypeStruct((T_CTX, D), F32), jax.ShapeDtypeStruct((T_LAT, D), F32)]
    else:
        out_specs = [pl.BlockSpec((tm, D), lambda i: (prev(i), 0))]
        out_shape = [jax.ShapeDtypeStruct((T, D), F32)]
    out = pl.pallas_call(
        functools.partial(_outproj_ffn_kernel, n_x=len(x_args), n_out=len(out_shape)),
        grid=(OUT_TILES + 1,),
        in_specs=x_specs + [
            pl.BlockSpec((1, 6, D), lambda i: (_mod_row(cur(i), tm), 0, 0)),
            pl.BlockSpec((1, 6, D), lambda i: (_mod_row(prev(i), tm), 0, 0)),
            vec, vec, vec,
            act, act, act, act,
            pl.BlockSpec((tm, GW), lambda i: (cur(i), CB_RG)),
            act, act,
            pl.BlockSpec((tm, GW), lambda i: (cur(i), CB_AG)),
            pl.BlockSpec((1, GW), lambda i: (0, 0)),
            pl.BlockSpec((1, D, D), lambda i: (layer, 0, 0), pipeline_mode=once),
            pl.BlockSpec((1, D, D_FF), lambda i: (layer, 0, 0), pipeline_mode=once),
            pl.BlockSpec((1, D, D_FF), lambda i: (layer, 0, 0), pipeline_mode=once),
            pl.BlockSpec((1, D_FF, D), lambda i: (layer, 0, 0), pipeline_mode=once),
        ],
        out_specs=out_specs,
        out_shape=out_shape,
        scratch_shapes=[pltpu.VMEM((2, tm, D), F32), pltpu.VMEM((2, tm, D), BF16)],
        compiler_params=_cparams(("arbitrary",)),
        name="outproj_ffn",
    )(*x_args, mod_l, mod_l, g_post, g_pre2, g_post2, o_pool, o_na, ret_f, ret_b, parts, gla_f, gla_b, parts, ng_lanes,
      w_out_b, wg, wu, wd)
    return tuple(out) if split_out else out[0]


def _rope_tables():
    nf = 16
    inv = (ROPE_BASE ** (-np.arange(nf, dtype=np.float32) / nf)).astype(np.float32)
    tok = np.arange(L_LAT)
    cos = np.zeros((L_LAT, HD), np.float32)
    sin = np.zeros((L_LAT, HD), np.float32)
    for axis, pos in enumerate((tok // GRID_W, tok % GRID_W)):
        ang = pos.astype(np.float32)[:, None] * inv[None, :]
        c, s = np.cos(ang), np.sin(ang)
        cos[:, axis * 32:axis * 32 + 32] = np.concatenate([c, c], axis=1)
        sin[:, axis * 32:axis * 32 + 32] = np.concatenate([-s, s], axis=1)
    return jnp.asarray(np.tile(cos, (1, HEADS))), jnp.asarray(np.tile(sin, (1, HEADS)))


def _block_diag(w):
    g, c, _ = w.shape
    out = jnp.zeros((g * c, g * c), w.dtype)
    for i in range(g):
        out = out.at[i * c:(i + 1) * c, i * c:(i + 1) * c].set(w[i])
    return out


def kernel(x_prompt, x_sample, cache_na_k, cache_na_v, state_ret, state_gla, c, c_ctx, w_mod, b_mod,
           g_pre_mix, g_post_mix, g_pre_ffn, g_post_ffn, w_in, w_out, pool_w, pool_scale, na_rpb,
           ret_decay_logit, gla_gate_up, gla_gate_b, gla_norm_g, w_ffn_gate, w_ffn_up, w_ffn_down):
    x = (x_prompt.reshape(T_CTX, D), x_sample.reshape(T_LAT, D))
    cv8 = jnp.concatenate([c_ctx[None, :], c, jnp.zeros((8 - 1 - B_LAT, D), F32)], axis=0)
    mods = _modulation(cv8, w_mod, b_mod).reshape(DEPTH, 8, 6, D)

    w_in_b = w_in.astype(BF16)
    w_out_b = w_out.astype(BF16)
    wg_b, wu_b, wd_b = w_ffn_gate.astype(BF16), w_ffn_up.astype(BF16), w_ffn_down.astype(BF16)
    gate_up_pad = jnp.pad(gla_gate_up, ((0, 0), (0, 0), (0, 128 - GLA_LOWRANK), (0, 0)))
    rope_tabs = _rope_tables()
    ck = cache_na_k.reshape(B_LAT, DEPTH, PAST, GW)
    cv = cache_na_v.reshape(B_LAT, DEPTH, PAST, GW)
    s0_ret = state_ret.reshape(B_LAT, DEPTH, 2, GW, HD)
    s0_gla = state_gla.reshape(B_LAT, DEPTH, 2, GLA_QK, HD)

    new_k = jnp.zeros((B_CTX, DEPTH, L_CTX, GW), F32)
    new_v = jnp.zeros((B_CTX, DEPTH, L_CTX, GW), F32)
    new_ret = jnp.zeros((B_CTX, DEPTH, 2, GW, HD), F32)
    new_gla = jnp.zeros((B_CTX, DEPTH, 2, GLA_QK, HD), F32)
    for l in range(DEPTH):
        mod_l = mods[l]
        parts = _inproj(x, mod_l, g_pre_mix[l][None, :], w_in_b, l)
        o_pool = _pool(parts, _block_diag(pool_w[l]).astype(BF16), pool_scale[l][None, :])
        o_na, new_k, new_v = _attention(parts, ck, cv, na_rpb[l], l, new_k, new_v)
        lg_lanes = jnp.repeat(ret_decay_logit[l], HD, axis=1)
        rf, rb, new_ret = _retention(parts, lg_lanes, rope_tabs, s0_ret[:, l], l, new_ret)
        gf, gbw, new_gla = _gla(parts, gate_up_pad[l], gla_gate_b[l][:, None, :], s0_gla[:, l], l, new_gla)
        x = _outproj_ffn(x, mod_l, g_post_mix[l][None, :], g_pre_ffn[l][None, :], g_post_ffn[l][None, :],
                         o_pool, o_na, rf, rb, gf, gbw, parts, jnp.tile(gla_norm_g[l], HEADS)[None, :],
                         w_out_b, wg_b, wu_b, wd_b, l, split_out=(l == DEPTH - 1))

    return (x[0].reshape(B_CTX, L_CTX, D), x[1].reshape(B_LAT, L_LAT, D),
            new_k.reshape(B_CTX, DEPTH, L_CTX, HEADS, HD), new_v.reshape(B_CTX, DEPTH, L_CTX, HEADS, HD),
            new_ret.reshape(B_CTX, DEPTH, 2, HEADS, RET_DK, HD), new_gla.reshape(B_CTX, DEPTH, 2, HEADS, GLA_DK, HD))
```
